```python
import jax, jax.numpy as jnp
from jax import lax
import numpy as np

D_MODEL = 1024
BATCH = 16
SEQ = 4096
DEPTH = 4
DEC_BATCH = 8
DEC_SEQ = 64
PAST_LEN = 2048

CHUNK = 64
HG_HEADS = 4
HG_DK = 128
HG_DV = 128
HG_WIDTH = HG_HEADS * HG_DK
ATT_HEADS = 8
ATT_DIM = 64
ATT_WIDTH = ATT_HEADS * ATT_DIM
IDX_HEADS = 4
IDX_DIM = 64
TOPK_MAX = 256
QBLOCK = CHUNK
ATT_SCALE = ATT_DIM ** -0.5
IDX_SCALE = IDX_DIM ** -0.5
IDX_W_SCALE = IDX_HEADS ** -0.5
N_GROUPS = 4
EXPERTS_PER_GROUP = 4
D_EXPERT = 256
TOP_K_WITHIN = 2
DN_ALPHA = (2 * DEPTH) ** 0.25
DN_BETA = (8 * DEPTH) ** -0.25
LN_EPS = 1e-5
RMS_EPS = 1e-6
IN_SIZES = (HG_WIDTH, HG_WIDTH, HG_HEADS * HG_DV, HG_HEADS * HG_DV,
            ATT_WIDTH, ATT_WIDTH, ATT_WIDTH, IDX_HEADS * IDX_DIM, IDX_DIM, IDX_HEADS,
            D_MODEL, D_MODEL)
IN_COLS = sum(IN_SIZES)

kernel_name = "hgrn2_dsa_hiermoe_streaming_step"

F32 = jnp.float32


def split_in(h):
    parts, off = [], 0
    for s in IN_SIZES:
        parts.append(h[..., off:off + s])
        off += s
    return parts


def layer_norm(x, g, b):
    xf = x.astype(F32)
    mu = jnp.mean(xf, -1, keepdims=True)
    var = jnp.mean(jnp.square(xf - mu), -1, keepdims=True)
    y = (xf - mu) * lax.rsqrt(var + LN_EPS) * g.astype(F32) + b.astype(F32)
    return y.astype(x.dtype)


def hgrn_lower_bounds(lb_logits):
    p = jax.nn.softmax(lb_logits.astype(F32), axis=0)
    return jnp.concatenate([jnp.zeros_like(p[:1]), jnp.cumsum(p[1:], axis=0)], axis=0)


def hgrn2_chunk(S0, q, logf, k, v):
    T = q.shape[1]
    Bc = jnp.cumsum(logf, axis=1)
    tpos = jnp.arange(T)
    causal = (tpos[:, None] >= tpos[None, :])[None, :, :, None, None]
    diff = Bc[:, :, None] - Bc[:, None, :]
    decay = jnp.exp(jnp.where(causal, diff, -jnp.inf))
    A = jnp.einsum('btshk,bshk->bhts', q[:, :, None] * decay, k)
    o = jnp.einsum('bhts,bshv->bthv', A, v) + jnp.einsum('bthk,bhkv->bthv', q * jnp.exp(Bc), S0)
    BT = Bc[:, -1]
    S = jnp.exp(BT)[..., None] * S0 + jnp.einsum('bshk,bshv->bhkv', k * jnp.exp(BT[:, None] - Bc), v)
    return o, S


def hgrn2_scan(S0, q, logf, k, v):
    B, T = q.shape[:2]
    n = T // CHUNK
    to_chunks = lambda a: jnp.moveaxis(a.reshape(B, n, CHUNK, *a.shape[2:]), 1, 0)

    def step(S, xs):
        o, S = hgrn2_chunk(S, *xs)
        return S, o

    S, o = lax.scan(step, S0, (to_chunks(q), to_chunks(logf), to_chunks(k), to_chunks(v)))
    o = jnp.moveaxis(o, 0, 1).reshape(B, T, HG_HEADS, HG_DV)
    return o, S


def dsa_attend(q, qi, wi, q_pos, k, v, ki, k_pos, topk):
    s_idx = jnp.einsum('bthd,bsd->bths', qi.astype(F32), ki.astype(F32)) * IDX_SCALE
    score = jnp.einsum('bth,bths->bts', wi.astype(F32) * IDX_W_SCALE, jax.nn.relu(s_idx))
    admissible = (k_pos[None, :] // CHUNK) <= (q_pos[:, None] // CHUNK)
    score = jnp.where(admissible[None], score, -jnp.inf)
    top_val, top_idx = lax.top_k(score, topk)
    valid = top_val > -jnp.inf
    gather = jax.vmap(lambda rows, idx: rows[idx])
    k_sel = gather(k, top_idx).astype(F32)
    v_sel = gather(v, top_idx).astype(F32)
    logits = jnp.einsum('bthd,btjhd->bthj', q.astype(F32), k_sel) * ATT_SCALE
    logits = jnp.where(valid[:, :, None, :], logits, -jnp.inf)
    p = jax.nn.softmax(logits, axis=-1)
    return jnp.einsum('bthj,btjhd->bthd', p, v_sel).astype(q.dtype)


def dsa_prompt(q, qi, wi, k, v, ki):
    B, T = q.shape[:2]
    topk = min(TOPK_MAX, T // 4)
    pos = jnp.arange(T)
    nb = T // QBLOCK
    blk = lambda a: jnp.moveaxis(a.reshape(B, nb, QBLOCK, *a.shape[2:]), 1, 0)

    def one(args):
        qb, qib, wib, pb = args
        return dsa_attend(qb, qib, wib, pb, k, v, ki, pos, topk)

    out = lax.map(one, (blk(q), blk(qi), blk(wi), pos.reshape(nb, QBLOCK)))
    return jnp.moveaxis(out, 0, 1).reshape(B, T, ATT_HEADS, ATT_DIM)


def token_mixer(x, w_in, lb, hg_norm_w, w_br_hg, w_br_att, w_out, hg_state, past_k, past_v, past_ki):
    B, T, _ = x.shape
    (hq, hf, hi, hg, aq, ak, av, iq, ik, iw, g_hg, g_att) = split_in(jnp.einsum('btd,dc->btc', x, w_in))
    z = hf.astype(F32)
    logf = jnp.logaddexp(jnp.log(lb), jnp.log1p(-lb) + jax.nn.log_sigmoid(z))
    kk = (1.0 - lb) * jax.nn.sigmoid(-z)
    q_h = jax.nn.silu(hq.astype(F32)).reshape(B, T, HG_HEADS, HG_DK)
    f_h = logf.reshape(B, T, HG_HEADS, HG_DK)
    k_h = kk.reshape(B, T, HG_HEADS, HG_DK)
    v_h = hi.astype(F32).reshape(B, T, HG_HEADS, HG_DV)
    if hg_state is None:
        S0 = jnp.zeros((B, HG_HEADS, HG_DK, HG_DV), F32)
        o_hg, S = hgrn2_scan(S0, q_h, f_h, k_h, v_h)
    else:
        o_hg, S = hgrn2_chunk(hg_state.astype(F32), q_h, f_h, k_h, v_h)
    o_hg = o_hg * lax.rsqrt(jnp.mean(o_hg * o_hg, -1, keepdims=True) + RMS_EPS) * hg_norm_w.astype(F32)
    y_hg = (o_hg.reshape(B, T, HG_HEADS * HG_DV) * jax.nn.silu(hg.astype(F32))).astype(x.dtype)
    q_a = aq.reshape(B, T, ATT_HEADS, ATT_DIM)
    k_a = ak.reshape(B, T, ATT_HEADS, ATT_DIM)
    v_a = av.reshape(B, T, ATT_HEADS, ATT_DIM)
    qi = iq.reshape(B, T, IDX_HEADS, IDX_DIM)
    if past_k is None:
        o_att = dsa_prompt(q_a, qi, iw, k_a, v_a, ik)
    else:
        P = past_k.shape[1]
        k_all = jnp.concatenate([past_k.astype(k_a.dtype), k_a], axis=1)
        v_all = jnp.concatenate([past_v.astype(v_a.dtype), v_a], axis=1)
        ki_all = jnp.concatenate([past_ki.astype(ik.dtype), ik], axis=1)
        L = P + T
        o_att = dsa_attend(q_a, qi, iw, P + jnp.arange(T), k_all, v_all, ki_all, jnp.arange(L),
                           min(TOPK_MAX, L // 4))
    br_hg = jnp.einsum('btc,cd->btd', y_hg, w_br_hg)
    br_att = jnp.einsum('btc,cd->btd', o_att.reshape(B, T, ATT_WIDTH), w_br_att)
    merged = jax.nn.sigmoid(g_hg) * br_hg + jax.nn.sigmoid(g_att) * br_att
    out = jnp.einsum('btd,de->bte', merged, w_out)
    return out, k_a, v_a, ik, S.astype(x.dtype)


def hier_moe(x, w_rg, b_rg, w_re, b_re, w_gate, w_up, w_down):
    B, T, D = x.shape
    xt = x.reshape(B * T, D)
    g_logits = (xt @ w_rg + b_rg).astype(F32)
    g_prob = jax.nn.softmax(g_logits, -1)
    g_sel = jnp.argmax(g_logits, -1)
    g_w = jnp.take_along_axis(g_prob, g_sel[:, None], axis=1)
    e_logits = (xt @ w_re + b_re).astype(F32).reshape(-1, N_GROUPS, EXPERTS_PER_GROUP)
    e_logits = jnp.take_along_axis(e_logits, g_sel[:, None, None], axis=1)[:, 0]
    e_prob = jax.nn.softmax(e_logits, -1)
    top_p, top_i = lax.top_k(e_prob, TOP_K_WITHIN)
    top_p = top_p / jnp.sum(top_p, -1, keepdims=True)
    e_w = jnp.einsum('nj,nje->ne', top_p, jax.nn.one_hot(top_i, EXPERTS_PER_GROUP, dtype=F32))
    combine = jax.nn.one_hot(g_sel, N_GROUPS, dtype=F32)[:, :, None] * (g_w * e_w)[:, None, :]
    y = jnp.zeros((B * T, D), F32)
    for g in range(N_GROUPS):
        h = jax.nn.silu(jnp.einsum('nd,edf->nef', xt, w_gate[g])) * jnp.einsum('nd,edf->nef', xt, w_up[g])
        y = y + jnp.einsum('nef,efd->nd', h * combine[:, g, :, None].astype(h.dtype), w_down[g]).astype(F32)
    return y.astype(x.dtype).reshape(B, T, D)


def setup_inputs(seed: int = 0) -> dict:
    key = jax.random.key(seed)
    ks = jax.random.split(key, 28)
    nrm = lambda k, shape, s: jax.random.normal(k, shape, jnp.float32) * s
    G, E, F = N_GROUPS, EXPERTS_PER_GROUP, D_EXPERT
    return {
        "x_prompt": nrm(ks[0], (BATCH, SEQ, D_MODEL), 1.0),
        "x_sample": nrm(ks[1], (DEC_BATCH, DEC_SEQ, D_MODEL), 1.0),
        "cache_k": nrm(ks[2], (DEPTH, DEC_BATCH, PAST_LEN, ATT_HEADS, ATT_DIM), 1.0),
        "cache_v": nrm(ks[3], (DEPTH, DEC_BATCH, PAST_LEN, ATT_HEADS, ATT_DIM), 1.0),
        "cache_idx_k": nrm(ks[4], (DEPTH, DEC_BATCH, PAST_LEN, IDX_DIM), 1.0),
        "state_hgrn": nrm(ks[5], (DEPTH, DEC_BATCH, HG_HEADS, HG_DK, HG_DV), 0.5),
        "w_in": nrm(ks[6], (DEPTH, D_MODEL, IN_COLS), D_MODEL ** -0.5),
        "hg_lb_logits": nrm(ks[7], (DEPTH, HG_WIDTH), 0.5),
        "hg_norm_w": 1.0 + nrm(ks[8], (DEPTH, HG_DV), 0.02),
        "w_br_hg": nrm(ks[9], (DEPTH, HG_HEADS * HG_DV, D_MODEL), (HG_HEADS * HG_DV) ** -0.5 * DN_BETA),
        "w_br_att": nrm(ks[10], (DEPTH, ATT_WIDTH, D_MODEL), ATT_WIDTH ** -0.5 * DN_BETA),
        "w_out": nrm(ks[11], (DEPTH, D_MODEL, D_MODEL), D_MODEL ** -0.5 * DN_BETA),
        "ln1_g": 1.0 + nrm(ks[12], (DEPTH, D_MODEL), 0.02),
        "ln1_b": nrm(ks[13], (DEPTH, D_MODEL), 0.02),
        "ln2_g": 1.0 + nrm(ks[14], (DEPTH, D_MODEL), 0.02),
        "ln2_b": nrm(ks[15], (DEPTH, D_MODEL), 0.02),
        "w_rg": nrm(ks[16], (DEPTH, D_MODEL, G), D_MODEL ** -0.5),
        "b_rg": nrm(ks[17], (DEPTH, G), 0.01),
        "w_re": nrm(ks[18], (DEPTH, D_MODEL, G * E), D_MODEL ** -0.5),
        "b_re": nrm(ks[19], (DEPTH, G * E), 0.01),
        "w_gate": nrm(ks[20], (DEPTH, G, E, D_MODEL, F), D_MODEL ** -0.5),
        "w_up": nrm(ks[21], (DEPTH, G, E, D_MODEL, F), D_MODEL ** -0.5),
        "w_down": nrm(ks[22], (DEPTH, G, E, F, D_MODEL), F ** -0.5 * DN_BETA),
    }


def reference(x_prompt, x_sample, cache_k, cache_v, cache_idx_k, state_hgrn,
              w_in, hg_lb_logits, hg_norm_w, w_br_hg, w_br_att, w_out,
              ln1_g, ln1_b, ln2_g, ln2_b, w_rg, b_rg, w_re, b_re, w_gate, w_up, w_down):
    lbs = hgrn_lower_bounds(hg_lb_logits)
    xp, xs = x_prompt, x_sample
    kp_l, vp_l, ikp_l, sp_l = [], [], [], []
    ks_l, vs_l, iks_l, ss_l = [], [], [], []
    for l in range(DEPTH):
        mp, kp, vp, ikp, sp = token_mixer(xp, w_in[l], lbs[l], hg_norm_w[l], w_br_hg[l], w_br_att[l],
                                          w_out[l], None, None, None, None)
        ms, ks, vs, iks, ss = token_mixer(xs, w_in[l], lbs[l], hg_norm_w[l], w_br_hg[l], w_br_att[l],
                                          w_out[l], state_hgrn[l], cache_k[l], cache_v[l], cache_idx_k[l])
        kp_l.append(kp); vp_l.append(vp); ikp_l.append(ikp); sp_l.append(sp)
        ks_l.append(ks); vs_l.append(vs); iks_l.append(iks); ss_l.append(ss.astype(state_hgrn.dtype))
        xp = layer_norm(DN_ALPHA * xp + mp, ln1_g[l], ln1_b[l])
        xs = layer_norm(DN_ALPHA * xs + ms, ln1_g[l], ln1_b[l])
        xp = layer_norm(DN_ALPHA * xp + hier_moe(xp, w_rg[l], b_rg[l], w_re[l], b_re[l], w_gate[l], w_up[l], w_down[l]),
                        ln2_g[l], ln2_b[l])
        xs = layer_norm(DN_ALPHA * xs + hier_moe(xs, w_rg[l], b_rg[l], w_re[l], b_re[l], w_gate[l], w_up[l], w_down[l]),
                        ln2_g[l], ln2_b[l])
    new_k_prompt = jnp.stack(kp_l)
    new_v_prompt = jnp.stack(vp_l)
    new_idx_k_prompt = jnp.stack(ikp_l)
    new_state_hgrn_prompt = jnp.stack(sp_l)
    new_k_sample = jnp.stack(ks_l)
    new_v_sample = jnp.stack(vs_l)
    new_idx_k_sample = jnp.stack(iks_l)
    new_state_hgrn_sample = jnp.stack(ss_l)
    return (xp, xs, new_k_prompt, new_v_prompt, new_idx_k_prompt, new_state_hgrn_prompt,
            new_k_sample, new_v_sample, new_idx_k_sample, new_state_hgrn_sample)
```

```python
import functools

import jax
import jax.numpy as jnp
from jax import lax
from jax.experimental import pallas as pl
from jax.experimental.pallas import tpu as pltpu

F32 = jnp.float32
I32 = jnp.int32
MXU_DTYPE = jnp.bfloat16

D_MODEL = 1024
DEPTH = 4
CHUNK = 64
HG_HEADS = 4
HG_DK = 128
HG_DV = 128
HG_WIDTH = HG_HEADS * HG_DK
ATT_HEADS = 8
ATT_DIM = 64
ATT_WIDTH = ATT_HEADS * ATT_DIM
IDX_HEADS = 4
IDX_DIM = 64
TOPK_MAX = 256
ATT_SCALE = ATT_DIM ** -0.5
IDX_SCALE = IDX_DIM ** -0.5
IDX_W_SCALE = IDX_HEADS ** -0.5
N_GROUPS = 4
EXPERTS_PER_GROUP = 4
N_EXPERTS = N_GROUPS * EXPERTS_PER_GROUP
D_EXPERT = 256
DN_ALPHA = (2 * DEPTH) ** 0.25
LN_EPS = 1e-5
RMS_EPS = 1e-6
IN_SIZES = (HG_WIDTH, HG_WIDTH, HG_HEADS * HG_DV, HG_HEADS * HG_DV,
            ATT_WIDTH, ATT_WIDTH, ATT_WIDTH, IDX_HEADS * IDX_DIM, IDX_DIM, IDX_HEADS,
            D_MODEL, D_MODEL)

LANES = 128
SUBLANES = 8
SUBCHUNK = 16
VMEM_LIMIT = 56 * 1024 * 1024
INT_MIN = -2 ** 31
NEG_BIG = -1e30
ROUTER_LANES = LANES
EXPERT_LANE0 = N_GROUPS


def _params(*sem):
    return pltpu.CompilerParams(dimension_semantics=sem, vmem_limit_bytes=VMEM_LIMIT)


def _mm(a, b):
    return jnp.dot(a.astype(MXU_DTYPE), b.astype(MXU_DTYPE), preferred_element_type=F32)


def _mm_nt(a, b):
    return lax.dot_general(a.astype(MXU_DTYPE), b.astype(MXU_DTYPE), (((1,), (1,)), ((), ())),
                           preferred_element_type=F32)


def _sigmoid(x):
    return 1.0 / (1.0 + jnp.exp(-x))


def _silu(x):
    return x * _sigmoid(x)


def _layer_norm(r, g, b):
    mu = jnp.mean(r, axis=-1, keepdims=True)
    d = r - mu
    var = jnp.mean(d * d, axis=-1, keepdims=True)
    return d * lax.rsqrt(var + LN_EPS) * g + b


def _in_proj_kernel(x_ref, wh_ref, wa_ref, wi_ref, wg_ref, waqT_ref, wavT_ref, wiqT_ref, wiwT_ref,
                    lbp_ref,
                    qh_ref, lf_ref, kk_ref, vh_ref, og_ref, k32_ref, v32_ref, kb_ref, ik32_ref, kib_ref,
                    aqT_ref, vT_ref, iqT_ref, iwT_ref, sgh_ref, sga_ref):
    xb = x_ref[...].astype(MXU_DTYPE)
    W = HG_WIDTH
    qh_ref[...] = _silu(_mm(xb, wh_ref[:, 0:W]))
    z = _mm(xb, wh_ref[:, W:2 * W])
    log_lb = lbp_ref[0:1, :]
    log_1mlb = lbp_ref[1:2, :]
    one_mlb = lbp_ref[2:3, :]
    log_sig = jnp.minimum(z, 0.0) - jnp.log(1.0 + jnp.exp(-jnp.abs(z)))
    b = log_1mlb + log_sig
    lf_ref[...] = jnp.maximum(log_lb, b) + jnp.log(1.0 + jnp.exp(-jnp.abs(log_lb - b)))
    kk_ref[...] = one_mlb * _sigmoid(-z)
    vh_ref[...] = _mm(xb, wh_ref[:, 2 * W:3 * W])
    og_ref[...] = _silu(_mm(xb, wh_ref[:, 3 * W:4 * W]))
    A = ATT_WIDTH
    k = _mm(xb, wa_ref[:, 0:A])
    k32_ref[...] = k
    kb_ref[...] = k.astype(kb_ref.dtype)
    v32_ref[...] = _mm(xb, wa_ref[:, A:2 * A])
    aqT_ref[...] = (_mm_nt(waqT_ref[...], xb) * ATT_SCALE).astype(aqT_ref.dtype)
    kt = vT_ref.shape[-1]
    for t in range(vT_ref.shape[0]):
        vT_ref[t] = _mm_nt(wavT_ref[...], xb[t * kt:(t + 1) * kt, :]).astype(vT_ref.dtype)
    ik = _mm(xb, wi_ref[...])[:, 0:IDX_DIM]
    ik32_ref[...] = ik
    kib_ref[...] = ik.astype(kib_ref.dtype)
    iqT_ref[...] = _mm_nt(wiqT_ref[...], xb).astype(iqT_ref.dtype)
    iwT_ref[...] = _mm_nt(wiwT_ref[...], xb) * (IDX_SCALE * IDX_W_SCALE)
    D = D_MODEL
    sgh_ref[...] = _sigmoid(_mm(xb, wg_ref[:, 0:D]))
    sga_ref[...] = _sigmoid(_mm(xb, wg_ref[:, D:2 * D]))


def _in_proj(x, lw, tm, kt):
    n = x.shape[0]
    tm = min(tm, n)
    assert tm % kt == 0 and n % tm == 0
    grid = (n // tm,)
    full = lambda a: pl.BlockSpec(a.shape, lambda i: (0,) * a.ndim)
    rows = lambda c: pl.BlockSpec((tm, c), lambda i: (i, 0))
    cols = lambda r: pl.BlockSpec((r, tm), lambda i: (0, i))
    weights = (lw["w_h"], lw["w_a"], lw["w_i"], lw["w_g"], lw["w_aqT"], lw["w_avT"], lw["w_iqT"], lw["w_iwT"],
               lw["lbp"])
    out_shape = (
        jax.ShapeDtypeStruct((n, HG_WIDTH), F32),
        jax.ShapeDtypeStruct((n, HG_WIDTH), F32),
        jax.ShapeDtypeStruct((n, HG_WIDTH), F32),
        jax.ShapeDtypeStruct((n, HG_WIDTH), F32),
        jax.ShapeDtypeStruct((n, HG_WIDTH), F32),
        jax.ShapeDtypeStruct((n, ATT_WIDTH), F32),
        jax.ShapeDtypeStruct((n, ATT_WIDTH), F32),
        jax.ShapeDtypeStruct((n, ATT_WIDTH), MXU_DTYPE),
        jax.ShapeDtypeStruct((n, IDX_DIM), F32),
        jax.ShapeDtypeStruct((n, IDX_DIM), MXU_DTYPE),
        jax.ShapeDtypeStruct((ATT_WIDTH, n), MXU_DTYPE),
        jax.ShapeDtypeStruct((n // kt, ATT_WIDTH, kt), MXU_DTYPE),
        jax.ShapeDtypeStruct((IDX_HEADS * IDX_DIM, n), MXU_DTYPE),
        jax.ShapeDtypeStruct((SUBLANES, n), F32),
        jax.ShapeDtypeStruct((n, D_MODEL), F32),
        jax.ShapeDtypeStruct((n, D_MODEL), F32),
    )
    out_specs = (rows(HG_WIDTH),) * 5 + (rows(ATT_WIDTH),) * 3 + (rows(IDX_DIM),) * 2 + (
        cols(ATT_WIDTH), pl.BlockSpec((tm // kt, ATT_WIDTH, kt), lambda i: (i, 0, 0)),
        cols(IDX_HEADS * IDX_DIM), cols(SUBLANES), rows(D_MODEL), rows(D_MODEL))
    return pl.pallas_call(
        _in_proj_kernel,
        grid=grid,
        in_specs=[rows(D_MODEL)] + [full(w) for w in weights],
        out_specs=out_specs,
        out_shape=out_shape,
        compiler_params=_params("parallel"),
        name="in_proj",
    )(x, *weights)


def _hgrn_kernel(q_ref, f_ref, k_ref, v_ref, og_ref, nw_ref, s0_ref, y_ref, s_ref, st_scr, *, n_chunks):
    g = pl.program_id(2)

    @pl.when(g == 0)
    def _():
        st_scr[...] = s0_ref[...].T

    C, SC = CHUNK, SUBCHUNK
    row = lax.broadcasted_iota(I32, (C, C), 0)
    col = lax.broadcasted_iota(I32, (C, C), 1)
    tri = (row >= col).astype(F32)
    row_c = lax.broadcasted_iota(I32, (C, 1), 0)
    row_s = lax.broadcasted_iota(I32, (SC, 1), 0)
    lane_c = lax.broadcasted_iota(I32, (1, C), 1)
    nw = nw_ref[...]

    def chunk(c, carry):
        sl = pl.ds(pl.multiple_of(c * C, C), C)
        q = q_ref[sl, :]
        lf = f_ref[sl, :]
        k = k_ref[sl, :]
        v = v_ref[sl, :]
        bc = jnp.dot(tri, lf, precision=lax.Precision.HIGHEST, preferred_element_type=F32)
        bt = bc[C - 1:C, :]
        st = st_scr[...]
        o = _mm_nt(q * jnp.exp(bc), st)
        kd = (k * jnp.exp(bt - bc)).astype(MXU_DTYPE)
        st_scr[...] = st * jnp.exp(bt) + _mm(v.T, kd)
        blocks = []
        for i in range(C // SC):
            r0 = i * SC
            q_i = q[r0:r0 + SC, :]
            k_i = k[r0:r0 + SC, :]
            bc_i = bc[r0:r0 + SC, :]
            if i == 0:
                a_i = jnp.zeros((SC, C), F32)
            else:
                anchor = bc[r0:r0 + 1, :]
                qd = q_i * jnp.exp(bc_i - anchor)
                kd_i = jnp.where(row_c < r0, k * jnp.exp(jnp.minimum(anchor - bc, 0.0)), 0.0)
                a_i = _mm_nt(qd, kd_i)
            for s in range(SC):
                e = jnp.where(row_s >= s, jnp.exp(jnp.minimum(bc_i - bc_i[s:s + 1, :], 0.0)), 0.0)
                colv = jnp.sum(q_i * e * k_i[s:s + 1, :], axis=1, keepdims=True)
                a_i = jnp.where(lane_c == r0 + s, colv, a_i)
            blocks.append(a_i)
        a = jnp.concatenate(blocks, axis=0)
        o = o + _mm(a, v)
        o = o * lax.rsqrt(jnp.mean(o * o, axis=-1, keepdims=True) + RMS_EPS) * nw
        y_ref[sl, :] = (o * og_ref[sl, :]).astype(y_ref.dtype)
        return carry

    lax.fori_loop(0, n_chunks, chunk, 0)

    @pl.when(g == pl.num_programs(2) - 1)
    def _():
        s_ref[...] = st_scr[...].T


def _hgrn(qh, lf, kk, vh, og, norm_w, s0, batch, seq, chunks_per_step):
    n_chunks = seq // CHUNK
    g_sz = min(chunks_per_step, n_chunks)
    steps = n_chunks // g_sz
    tb = g_sz * CHUNK
    r3 = lambda a: a.reshape(batch, seq, HG_WIDTH)
    tok = pl.BlockSpec((None, tb, HG_DK), lambda b, h, g: (b, g, h))
    st = pl.BlockSpec((None, None, HG_DK, HG_DV), lambda b, h, g: (b, h, 0, 0))
    y, s = pl.pallas_call(
        functools.partial(_hgrn_kernel, n_chunks=g_sz),
        grid=(batch, HG_HEADS, steps),
        in_specs=[tok, tok, tok, tok, tok, pl.BlockSpec((1, HG_DV), lambda b, h, g: (0, 0)), st],
        out_specs=(tok, st),
        out_shape=(jax.ShapeDtypeStruct((batch, seq, HG_WIDTH), MXU_DTYPE),
                   jax.ShapeDtypeStruct((batch, HG_HEADS, HG_DK, HG_DV), F32)),
        scratch_shapes=[pltpu.VMEM((HG_DV, HG_DK), F32)],
        compiler_params=_params("parallel", "parallel", "arbitrary"),
        name="hgrn",
    )(r3(qh), r3(lf), r3(kk), r3(vh), r3(og), norm_w.reshape(1, HG_DV), s0)
    return y.reshape(batch * seq, HG_WIDTH), s


def _dsa_kernel(kib_ref, iqT_ref, iwT_ref, kb_ref, aqT_ref, vT_ref, o_ref, keys_scr, acc_scr,
                *, past, topk, tq, kt):
    i = pl.program_id(1)
    qpos0 = past + i * tq
    nk = (qpos0 + tq + kt - 1) // kt
    lane_q = lax.broadcasted_iota(I32, (1, tq), 1)
    qchunk = (qpos0 + lane_q) // CHUNK
    row_k = lax.broadcasted_iota(I32, (kt, 1), 0)
    tile = lambda j: pl.ds(pl.multiple_of(j * kt, kt), kt)

    def score_tile(j, carry):
        ki = kib_ref[tile(j), :]
        sc = jnp.zeros((kt, tq), F32)
        for h in range(IDX_HEADS):
            raw = jnp.dot(ki, iqT_ref[h * IDX_DIM:(h + 1) * IDX_DIM, :], preferred_element_type=F32)
            sc = sc + jnp.maximum(raw, 0.0) * iwT_ref[h:h + 1, :]
        bits = lax.bitcast_convert_type(sc, I32)
        bits = jnp.where(bits == INT_MIN, 0, bits)
        key = bits ^ ((bits >> 31) & 0x7FFFFFFF)
        kchunk = (j * kt + row_k) // CHUNK
        keys_scr[tile(j), :] = jnp.where(kchunk <= qchunk, key, INT_MIN)
        return carry

    lax.fori_loop(0, nk, score_tile, 0)

    def count(pred):
        def body(j, acc):
            m = pred(keys_scr[tile(j), :], j * kt + row_k).astype(I32)
            return acc + jnp.sum(m.reshape(kt // SUBLANES, SUBLANES, tq), axis=0)
        acc = lax.fori_loop(0, nk, body, jnp.zeros((SUBLANES, tq), I32))
        return jnp.sum(acc, axis=0, keepdims=True)

    def radix(it, lo):
        cand = lo + jnp.left_shift(jnp.int32(1), 31 - it)
        cnt = count(lambda t, pos: t >= cand)
        return jnp.where(cnt >= topk, cand, lo)

    thr = lax.fori_loop(0, 32, radix, jnp.full((1, tq), INT_MIN, I32))
    has_k = thr > INT_MIN
    n_gt = count(lambda t, pos: t > thr)
    n_ge = count(lambda t, pos: t >= thr)
    need = topk - n_gt
    surplus = jnp.logical_and(has_k, n_ge > topk)

    lmax_bits = max(1, (keys_scr.shape[0]).bit_length())

    def bis(it, jb):
        cand = jb + jnp.left_shift(jnp.int32(1), lmax_bits - 1 - it)
        cnt = count(lambda t, pos: jnp.logical_and(t == thr, pos < cand))
        return jnp.where(cnt <= need, cand, jb)

    n_bis = jnp.where(jnp.max(surplus.astype(I32)) > 0, lmax_bits, 0)
    jbound = lax.fori_loop(0, n_bis, bis, jnp.zeros((1, tq), I32))
    jbound = jnp.where(surplus, jbound, jnp.where(has_k, 2 ** 30, 0))

    acc_scr[...] = jnp.zeros_like(acc_scr)
    half = lax.broadcasted_iota(I32, (LANES, 1), 0) // ATT_DIM
    q_heads = []
    for h in range(ATT_HEADS):
        pair = aqT_ref[(h // 2) * LANES:(h // 2 + 1) * LANES, :]
        q_heads.append(jnp.where(half == (h % 2), pair, jnp.zeros_like(pair)))

    def attend(j, carry):
        ms, ls = carry
        t = keys_scr[tile(j), :]
        pos = j * kt + row_k
        sel = jnp.logical_or(t > thr, jnp.logical_and(t == thr, pos < jbound))
        new_m, new_l = [], []
        for h in range(ATT_HEADS):
            kh = kb_ref[tile(j), (h // 2) * LANES:(h // 2 + 1) * LANES]
            lg = jnp.where(sel, jnp.dot(kh, q_heads[h], preferred_element_type=F32), NEG_BIG)
            m_new = jnp.maximum(ms[h], jnp.max(lg, axis=0, keepdims=True))
            p = jnp.where(sel, jnp.exp(lg - m_new), 0.0)
            alpha = jnp.exp(ms[h] - m_new)
            new_l.append(alpha * ls[h] + jnp.sum(p, axis=0, keepdims=True))
            new_m.append(m_new)
            rows = slice(h * ATT_DIM, (h + 1) * ATT_DIM)
            pv = jnp.dot(vT_ref[j, rows, :], p.astype(MXU_DTYPE), preferred_element_type=F32)
            acc_scr[rows, :] = acc_scr[rows, :] * alpha + pv
        return tuple(new_m), tuple(new_l)

    init = (tuple(jnp.full((1, tq), NEG_BIG, F32) for _ in range(ATT_HEADS)),
            tuple(jnp.zeros((1, tq), F32) for _ in range(ATT_HEADS)))
    _, ls = lax.fori_loop(0, nk, attend, init)
    for h in range(ATT_HEADS):
        rows = slice(h * ATT_DIM, (h + 1) * ATT_DIM)
        acc_scr[rows, :] = acc_scr[rows, :] / ls[h]
    o_ref[...] = acc_scr[...].T.astype(o_ref.dtype)


def _dsa(kib, iqT, iwT, kb, aqT, vT, batch, n_q, l_pad, past, topk, tq, kt):
    nq = n_q // tq
    qcol = lambda r: pl.BlockSpec((r, tq), lambda b, i: (0, b * nq + i))
    return pl.pallas_call(
        functools.partial(_dsa_kernel, past=past, topk=topk, tq=tq, kt=kt),
        grid=(batch, nq),
        in_specs=[pl.BlockSpec((None, l_pad, IDX_DIM), lambda b, i: (b, 0, 0)),
                  qcol(IDX_HEADS * IDX_DIM), qcol(SUBLANES),
                  pl.BlockSpec((None, l_pad, ATT_WIDTH), lambda b, i: (b, 0, 0)),
                  qcol(ATT_WIDTH),
                  pl.BlockSpec((l_pad // kt, ATT_WIDTH, kt), lambda b, i: (b, 0, 0))],
        out_specs=pl.BlockSpec((tq, ATT_WIDTH), lambda b, i: (b * nq + i, 0)),
        out_shape=jax.ShapeDtypeStruct((batch * n_q, ATT_WIDTH), MXU_DTYPE),
        scratch_shapes=[pltpu.VMEM((l_pad, tq), I32), pltpu.VMEM((ATT_WIDTH, tq), F32)],
        compiler_params=_params("parallel", "arbitrary"),
        name="dsa",
    )(kib, iqT, iwT, kb, aqT, vT)


def _out_proj_kernel(yh_ref, oa_ref, sgh_ref, sga_ref, x_ref, wbh_ref, wba_ref, wo_ref, g_ref, b_ref,
                     wr_ref, br_ref, x1_ref, comb_ref):
    br_hg = jnp.dot(yh_ref[...], wbh_ref[...], preferred_element_type=F32)
    br_att = jnp.dot(oa_ref[...], wba_ref[...], preferred_element_type=F32)
    merged = sgh_ref[...] * br_hg + sga_ref[...] * br_att
    out = _mm(merged, wo_ref[...])
    x1 = _layer_norm(DN_ALPHA * x_ref[...] + out, g_ref[...], b_ref[...])
    x1_ref[...] = x1
    lg = _mm(x1, wr_ref[...]) + br_ref[...]
    lane = lax.broadcasted_iota(I32, lg.shape, 1).astype(F32)
    ninf = -jnp.inf
    gmask = lane < N_GROUPS
    gl = jnp.where(gmask, lg, ninf)
    gmax = jnp.max(gl, axis=1, keepdims=True)
    gsel = jnp.min(jnp.where(gl == gmax, lane, float(ROUTER_LANES)), axis=1, keepdims=True)
    g_w = 1.0 / jnp.sum(jnp.where(gmask, jnp.exp(gl - gmax), 0.0), axis=1, keepdims=True)
    e0 = EXPERT_LANE0 + EXPERTS_PER_GROUP * gsel
    emask = jnp.logical_and(lane >= e0, lane < e0 + EXPERTS_PER_GROUP)
    el = jnp.where(emask, lg, ninf)
    emax = jnp.max(el, axis=1, keepdims=True)
    ee = jnp.where(emask, jnp.exp(el - emax), 0.0)
    prob = ee / jnp.sum(ee, axis=1, keepdims=True)
    pm = jnp.where(emask, prob, -1.0)
    p1 = jnp.max(pm, axis=1, keepdims=True)
    i1 = jnp.min(jnp.where(pm == p1, lane, float(ROUTER_LANES)), axis=1, keepdims=True)
    pm2 = jnp.where(lane == i1, -1.0, pm)
    p2 = jnp.max(pm2, axis=1, keepdims=True)
    i2 = jnp.min(jnp.where(pm2 == p2, lane, float(ROUTER_LANES)), axis=1, keepdims=True)
    tot = p1 + p2
    comb_ref[...] = (jnp.where(lane == i1, g_w * (p1 / tot), 0.0)
                     + jnp.where(lane == i2, g_w * (p2 / tot), 0.0))


def _out_proj(yh, oa, sgh, sga, x, lw, tm):
    n = x.shape[0]
    tm = min(tm, n)
    full = lambda a: pl.BlockSpec(a.shape, lambda i: (0,) * a.ndim)
    rows = lambda c: pl.BlockSpec((tm, c), lambda i: (i, 0))
    weights = (lw["w_br_hg"], lw["w_br_att"], lw["w_out"], lw["ln1_g"], lw["ln1_b"], lw["w_r"], lw["b_r"])
    return pl.pallas_call(
        _out_proj_kernel,
        grid=(n // tm,),
        in_specs=[rows(HG_WIDTH), rows(ATT_WIDTH), rows(D_MODEL), rows(D_MODEL), rows(D_MODEL)]
        + [full(w) for w in weights],
        out_specs=(rows(D_MODEL), rows(ROUTER_LANES)),
        out_shape=(jax.ShapeDtypeStruct((n, D_MODEL), F32), jax.ShapeDtypeStruct((n, ROUTER_LANES), F32)),
        compiler_params=_params("parallel"),
        name="out_proj",
    )(yh, oa, sgh, sga, x, *weights)


def _moe_kernel(x_ref, comb_ref, wgu_ref, wd_ref, g_ref, b_ref, o_ref, xb_scr, hb_scr, acc_scr):
    grp = pl.program_id(1)

    @pl.when(grp == 0)
    def _():
        xb_scr[...] = x_ref[...].astype(xb_scr.dtype)
        acc_scr[...] = jnp.zeros_like(acc_scr)

    xb = xb_scr[...]
    comb = comb_ref[...]
    lane = lax.broadcasted_iota(I32, comb.shape, 1)
    F = D_EXPERT
    for e in range(EXPERTS_PER_GROUP):
        gu = jnp.dot(xb, wgu_ref[e], preferred_element_type=F32)
        h = _silu(gu[:, 0:F]) * gu[:, F:2 * F]
        c = jnp.sum(jnp.where(lane == EXPERT_LANE0 + grp * EXPERTS_PER_GROUP + e, comb, 0.0),
                    axis=1, keepdims=True)
        hb_scr[:, e * F:(e + 1) * F] = (h * c).astype(hb_scr.dtype)
    acc_scr[...] += jnp.dot(hb_scr[...], wd_ref[...], preferred_element_type=F32)

    @pl.when(grp == pl.num_programs(1) - 1)
    def _():
        o_ref[...] = _layer_norm(DN_ALPHA * x_ref[...] + acc_scr[...], g_ref[...], b_ref[...])


def _moe(x1, comb, lw, tm):
    n = x1.shape[0]
    tm = min(tm, n)
    E, F = EXPERTS_PER_GROUP, D_EXPERT
    return pl.pallas_call(
        _moe_kernel,
        grid=(n // tm, N_GROUPS),
        in_specs=[pl.BlockSpec((tm, D_MODEL), lambda i, g: (i, 0)),
                  pl.BlockSpec((tm, ROUTER_LANES), lambda i, g: (i, 0)),
                  pl.BlockSpec((None, E, D_MODEL, 2 * F), lambda i, g: (g, 0, 0, 0)),
                  pl.BlockSpec((None, E * F, D_MODEL), lambda i, g: (g, 0, 0)),
                  pl.BlockSpec((1, D_MODEL), lambda i, g: (0, 0)),
                  pl.BlockSpec((1, D_MODEL), lambda i, g: (0, 0))],
        out_specs=pl.BlockSpec((tm, D_MODEL), lambda i, g: (i, 0)),
        out_shape=jax.ShapeDtypeStruct((n, D_MODEL), F32),
        scratch_shapes=[pltpu.VMEM((tm, D_MODEL), MXU_DTYPE), pltpu.VMEM((tm, E * F), MXU_DTYPE),
                        pltpu.VMEM((tm, D_MODEL), F32)],
        compiler_params=_params("parallel", "arbitrary"),
        name="moe",
    )(x1, comb, lw["w_gu"], lw["w_d"], lw["ln2_g"], lw["ln2_b"])


def _layer_weights(l, lbs, w_in, hg_norm_w, w_br_hg, w_br_att, w_out, ln1_g, ln1_b, ln2_g, ln2_b,
                   w_rg, b_rg, w_re, b_re, w_gate, w_up, w_down):
    md = MXU_DTYPE
    offs = [0]
    for s in IN_SIZES:
        offs.append(offs[-1] + s)
    w = w_in[l]
    seg = lambda a, b: w[:, offs[a]:offs[b]]
    idx_cols = jnp.concatenate([seg(8, 9), jnp.zeros((D_MODEL, LANES - IDX_DIM), F32)], axis=1)
    iw_rows = jnp.concatenate([seg(9, 10).T, jnp.zeros((SUBLANES - IDX_HEADS, D_MODEL), F32)], axis=0)
    lb = lbs[l]
    lbp = jnp.concatenate([jnp.log(lb)[None], jnp.log1p(-lb)[None], (1.0 - lb)[None],
                           jnp.zeros((SUBLANES - 3, HG_WIDTH), F32)], axis=0)
    w_r = jnp.concatenate([w_rg[l], w_re[l], jnp.zeros((D_MODEL, ROUTER_LANES - N_GROUPS - N_EXPERTS), F32)], axis=1)
    b_r = jnp.concatenate([b_rg[l], b_re[l], jnp.zeros((ROUTER_LANES - N_GROUPS - N_EXPERTS,), F32)])[None]
    return {
        "w_h": seg(0, 4).astype(md), "w_a": seg(5, 7).astype(md), "w_i": idx_cols.astype(md),
        "w_g": seg(10, 12).astype(md),
        "w_aqT": seg(4, 5).T.astype(md), "w_avT": seg(6, 7).T.astype(md), "w_iqT": seg(7, 8).T.astype(md),
        "w_iwT": iw_rows.astype(md), "lbp": lbp,
        "norm_w": hg_norm_w[l],
        "w_br_hg": w_br_hg[l].astype(md), "w_br_att": w_br_att[l].astype(md), "w_out": w_out[l].astype(md),
        "ln1_g": ln1_g[l][None], "ln1_b": ln1_b[l][None], "ln2_g": ln2_g[l][None], "ln2_b": ln2_b[l][None],
        "w_r": w_r.astype(md), "b_r": b_r,
        "w_gu": jnp.concatenate([w_gate[l], w_up[l]], axis=-1).astype(md),
        "w_d": w_down[l].reshape(N_GROUPS, EXPERTS_PER_GROUP * D_EXPERT, D_MODEL).astype(md),
    }


def _lower_bounds(lb_logits):
    p = jax.nn.softmax(lb_logits.astype(F32), axis=0)
    return jnp.concatenate([jnp.zeros_like(p[:1]), jnp.cumsum(p[1:], axis=0)], axis=0)


TM_IN = 256
TM_OUT = 512
TM_MOE = 1024
HGRN_CHUNKS_PER_STEP = 16
DSA_TQ = 128
DSA_KT = 256


def _mixer_and_ffn(x, lw, batch, seq, s0, past):
    tq, kt = DSA_TQ, DSA_KT
    (qh, lf, kk, vh, og, k32, v32, kb, ik32, kib, aqT, vT, iqT, iwT, sgh, sga) = _in_proj(x, lw, TM_IN, kt)
    y_hg, s_new = _hgrn(qh, lf, kk, vh, og, lw["norm_w"], s0, batch, seq, HGRN_CHUNKS_PER_STEP)
    if past is None:
        l_tot = seq
        l_pad = -(-l_tot // kt) * kt
        assert l_pad == l_tot and seq % tq == 0
        o_att = _dsa(kib.reshape(batch, seq, IDX_DIM), iqT, iwT, kb.reshape(batch, seq, ATT_WIDTH), aqT, vT,
                     batch, seq, l_pad, 0, min(TOPK_MAX, l_tot // 4), tq, kt)
    else:
        k_past, v_past, ki_past = past
        p_len = k_past.shape[1]
        l_tot = p_len + seq
        n_q = -(-seq // tq) * tq
        l_pad = -(-(p_len + n_q) // kt) * kt
        padk = lambda a, new: jnp.concatenate(
            [a, new, jnp.zeros((batch, l_pad - l_tot, a.shape[2]), a.dtype)], axis=1)
        k_all = padk(k_past, kb.reshape(batch, seq, ATT_WIDTH))
        ki_all = padk(ki_past, kib.reshape(batch, seq, IDX_DIM))
        v_all = padk(v_past, v32.astype(v_past.dtype).reshape(batch, seq, ATT_WIDTH))
        vT_all = jnp.transpose(v_all.reshape(batch * (l_pad // kt), kt, ATT_WIDTH), (0, 2, 1))
        padq = lambda a: jnp.pad(a.reshape(a.shape[0], batch, seq), ((0, 0), (0, 0), (0, n_q - seq))
                                 ).reshape(a.shape[0], batch * n_q)
        o_pad = _dsa(ki_all, padq(iqT), padq(iwT), k_all, padq(aqT), vT_all,
                     batch, n_q, l_pad, p_len, min(TOPK_MAX, l_tot // 4), tq, kt)
        o_att = o_pad.reshape(batch, n_q, ATT_WIDTH)[:, :seq].reshape(batch * seq, ATT_WIDTH)
    x1, comb = _out_proj(y_hg, o_att, sgh, sga, x, lw, TM_OUT)
    x2 = _moe(x1, comb, lw, TM_MOE)
    return x2, k32, v32, ik32, s_new


def kernel(x_prompt, x_sample, cache_k, cache_v, cache_idx_k, state_hgrn, w_in, hg_lb_logits, hg_norm_w,
           w_br_hg, w_br_att, w_out, ln1_g, ln1_b, ln2_g, ln2_b, w_rg, b_rg, w_re, b_re, w_gate, w_up, w_down):
    bp, tp, d = x_prompt.shape
    bs, ts, _ = x_sample.shape
    p_len = cache_k.shape[2]
    lbs = _lower_bounds(hg_lb_logits)
    xp = x_prompt.reshape(bp * tp, d)
    xs = x_sample.reshape(bs * ts, d)
    zeros_state = jnp.zeros((bp, HG_HEADS, HG_DK, HG_DV), F32)
    outs_p, outs_s = [], []
    for l in range(DEPTH):
        lw = _layer_weights(l, lbs, w_in, hg_norm_w, w_br_hg, w_br_att, w_out, ln1_g, ln1_b, ln2_g, ln2_b,
                            w_rg, b_rg, w_re, b_re, w_gate, w_up, w_down)
        xp, kp, vp, ikp, sp = _mixer_and_ffn(xp, lw, bp, tp, zeros_state, None)
        past = (cache_k[l].reshape(bs, p_len, ATT_WIDTH).astype(MXU_DTYPE),
                cache_v[l].reshape(bs, p_len, ATT_WIDTH).astype(MXU_DTYPE),
                cache_idx_k[l].astype(MXU_DTYPE))
        xs, ks, vs, iks, ss = _mixer_and_ffn(xs, lw, bs, ts, state_hgrn[l].astype(F32), past)
        outs_p.append((kp, vp, ikp, sp))
        outs_s.append((ks, vs, iks, ss))

    def stack(outs, b, t):
        k = jnp.stack([o[0] for o in outs]).reshape(DEPTH, b, t, ATT_HEADS, ATT_DIM)
        v = jnp.stack([o[1] for o in outs]).reshape(DEPTH, b, t, ATT_HEADS, ATT_DIM)
        ik = jnp.stack([o[2] for o in outs]).reshape(DEPTH, b, t, IDX_DIM)
        s = jnp.stack([o[3] for o in outs])
        return k, v, ik, s

    kp, vp, ikp, sp = stack(outs_p, bp, tp)
    ks, vs, iks, ss = stack(outs_s, bs, ts)
    return (xp.reshape(bp, tp, d), xs.reshape(bs, ts, d), kp, vp, ikp, sp, ks, vs, iks,
            ss.astype(state_hgrn.dtype))
```

```python
import functools

import jax
import jax.numpy as jnp
from jax import lax
from jax.experimental import pallas as pl
from jax.experimental.pallas import tpu as pltpu

F32 = jnp.float32
I32 = jnp.int32
MXU_DTYPE = jnp.bfloat16

D_MODEL = 1024
DEPTH = 4
CHUNK = 64
HG_HEADS = 4
HG_DK = 128
HG_DV = 128
HG_WIDTH = HG_HEADS * HG_DK
ATT_HEADS = 8
ATT_DIM = 64
ATT_WIDTH = ATT_HEADS * ATT_DIM
IDX_HEADS = 4
IDX_DIM = 64
TOPK_MAX = 256
ATT_SCALE = ATT_DIM ** -0.5
IDX_SCALE = IDX_DIM ** -0.5
IDX_W_SCALE = IDX_HEADS ** -0.5
N_GROUPS = 4
EXPERTS_PER_GROUP = 4
N_EXPERTS = N_GROUPS * EXPERTS_PER_GROUP
D_EXPERT = 256
DN_ALPHA = (2 * DEPTH) ** 0.25
LN_EPS = 1e-5
RMS_EPS = 1e-6
IN_SIZES = (HG_WIDTH, HG_WIDTH, HG_HEADS * HG_DV, HG_HEADS * HG_DV,
            ATT_WIDTH, ATT_WIDTH, ATT_WIDTH, IDX_HEADS * IDX_DIM, IDX_DIM, IDX_HEADS,
            D_MODEL, D_MODEL)

LANES = 128
SUBLANES = 8
SUBCHUNK = 16
VMEM_LIMIT = 56 * 1024 * 1024
INT_MIN = -2 ** 31
NEG_BIG = -1e30
ROUTER_LANES = LANES
EXPERT_LANE0 = N_GROUPS


def _params(*sem):
    return pltpu.CompilerParams(dimension_semantics=sem, vmem_limit_bytes=VMEM_LIMIT)


def _mm(a, b):
    return jnp.dot(a.astype(MXU_DTYPE), b.astype(MXU_DTYPE), preferred_element_type=F32)


def _mm_nt(a, b):
    return lax.dot_general(a.astype(MXU_DTYPE), b.astype(MXU_DTYPE), (((1,), (1,)), ((), ())),
                           preferred_element_type=F32)


def _sigmoid(x):
    return 1.0 / (1.0 + jnp.exp(-x))


def _silu(x):
    return x * _sigmoid(x)


def _layer_norm(r, g, b):
    mu = jnp.mean(r, axis=-1, keepdims=True)
    d = r - mu
    var = jnp.mean(d * d, axis=-1, keepdims=True)
    return d * lax.rsqrt(var + LN_EPS) * g + b


def _in_proj_kernel(x_ref, wh_ref, wa_ref, wi_ref, wg_ref, waqT_ref, wavT_ref, wiqT_ref, wiwT_ref,
                    lbp_ref, *refs):
    (qh_ref, lf_ref, kk_ref, vh_ref, og_ref, k32_ref, v32_ref, kb_ref, ik32_ref, kib_ref,
     aqT_ref, vT_ref, iqT_ref, iwT_ref, sgh_ref, sga_ref) = refs[-16:]
    xb = x_ref[...].astype(MXU_DTYPE)
    W = HG_WIDTH
    qh_ref[...] = _silu(_mm(xb, wh_ref[:, 0:W]))
    z = _mm(xb, wh_ref[:, W:2 * W])
    log_lb = lbp_ref[0:1, :]
    log_1mlb = lbp_ref[1:2, :]
    one_mlb = lbp_ref[2:3, :]
    log_sig = jnp.minimum(z, 0.0) - jnp.log(1.0 + jnp.exp(-jnp.abs(z)))
    b = log_1mlb + log_sig
    lf_ref[...] = jnp.maximum(log_lb, b) + jnp.log(1.0 + jnp.exp(-jnp.abs(log_lb - b)))
    kk_ref[...] = one_mlb * _sigmoid(-z)
    vh_ref[...] = _mm(xb, wh_ref[:, 2 * W:3 * W])
    og_ref[...] = _silu(_mm(xb, wh_ref[:, 3 * W:4 * W]))
    A = ATT_WIDTH
    k = _mm(xb, wa_ref[:, 0:A])
    k32_ref[...] = k
    kb_ref[...] = k.astype(kb_ref.dtype)
    v32_ref[...] = _mm(xb, wa_ref[:, A:2 * A])
    aqT_ref[...] = (_mm_nt(waqT_ref[...], xb) * ATT_SCALE).astype(aqT_ref.dtype)
    kt = vT_ref.shape[-1]
    for t in range(vT_ref.shape[0]):
        vT_ref[t] = _mm_nt(wavT_ref[...], xb[t * kt:(t + 1) * kt, :]).astype(vT_ref.dtype)
    ik = _mm(xb, wi_ref[...])[:, 0:IDX_DIM]
    ik32_ref[...] = ik
    kib_ref[...] = ik.astype(kib_ref.dtype)
    iqT_ref[...] = _mm_nt(wiqT_ref[...], xb).astype(iqT_ref.dtype)
    iwT_ref[...] = _mm_nt(wiwT_ref[...], xb) * (IDX_SCALE * IDX_W_SCALE)
    D = D_MODEL
    sgh_ref[...] = _sigmoid(_mm(xb, wg_ref[:, 0:D]))
    sga_ref[...] = _sigmoid(_mm(xb, wg_ref[:, D:2 * D]))


def _in_proj(x, lw, tm, kt, layer, bufs):
    n = x.shape[0]
    tm = min(tm, n)
    assert tm % kt == 0 and n % tm == 0
    grid = (n // tm,)
    full = lambda a: pl.BlockSpec(a.shape, lambda i: (0,) * a.ndim)
    rows = lambda c: pl.BlockSpec((tm, c), lambda i: (i, 0))
    cols = lambda r: pl.BlockSpec((r, tm), lambda i: (0, i))
    lrows = lambda c: pl.BlockSpec((None, tm, c), lambda i: (layer, i, 0))
    weights = (lw["w_h"], lw["w_a"], lw["w_i"], lw["w_g"], lw["w_aqT"], lw["w_avT"], lw["w_iqT"], lw["w_iwT"],
               lw["lbp"])
    out_shape = (
        jax.ShapeDtypeStruct((n, HG_WIDTH), F32),
        jax.ShapeDtypeStruct((n, HG_WIDTH), F32),
        jax.ShapeDtypeStruct((n, HG_WIDTH), F32),
        jax.ShapeDtypeStruct((n, HG_WIDTH), F32),
        jax.ShapeDtypeStruct((n, HG_WIDTH), F32),
        jax.ShapeDtypeStruct((DEPTH, n, ATT_WIDTH), F32),
        jax.ShapeDtypeStruct((DEPTH, n, ATT_WIDTH), F32),
        jax.ShapeDtypeStruct((n, ATT_WIDTH), MXU_DTYPE),
        jax.ShapeDtypeStruct((DEPTH, n, IDX_DIM), F32),
        jax.ShapeDtypeStruct((n, IDX_DIM), MXU_DTYPE),
        jax.ShapeDtypeStruct((ATT_WIDTH, n), MXU_DTYPE),
        jax.ShapeDtypeStruct((n // kt, ATT_WIDTH, kt), MXU_DTYPE),
        jax.ShapeDtypeStruct((IDX_HEADS * IDX_DIM, n), MXU_DTYPE),
        jax.ShapeDtypeStruct((SUBLANES, n), F32),
        jax.ShapeDtypeStruct((n, D_MODEL), F32),
        jax.ShapeDtypeStruct((n, D_MODEL), F32),
    )
    out_specs = (rows(HG_WIDTH),) * 5 + (lrows(ATT_WIDTH), lrows(ATT_WIDTH), rows(ATT_WIDTH),
                                         lrows(IDX_DIM), rows(IDX_DIM)) + (
        cols(ATT_WIDTH), pl.BlockSpec((tm // kt, ATT_WIDTH, kt), lambda i: (i, 0, 0)),
        cols(IDX_HEADS * IDX_DIM), cols(SUBLANES), rows(D_MODEL), rows(D_MODEL))
    in_specs = [rows(D_MODEL)] + [full(w) for w in weights]
    aliases = {}
    if bufs is not None:
        first = len(in_specs)
        in_specs += [pl.BlockSpec(memory_space=pl.ANY)] * len(bufs)
        aliases = {first: 5, first + 1: 6, first + 2: 8}
    return pl.pallas_call(
        _in_proj_kernel,
        grid=grid,
        in_specs=in_specs,
        out_specs=out_specs,
        out_shape=out_shape,
        input_output_aliases=aliases,
        compiler_params=_params("parallel"),
        name="in_proj",
    )(x, *weights, *(bufs or ()))


def _hgrn_kernel(q_ref, f_ref, k_ref, v_ref, og_ref, nw_ref, s0_ref, y_ref, s_ref, st_scr, *, n_chunks):
    g = pl.program_id(1)

    @pl.when(g == 0)
    def _():
        for h in range(HG_HEADS):
            st_scr[h] = s0_ref[h].T

    C, SC, R8 = CHUNK, SUBCHUNK, SUBLANES
    row = lax.broadcasted_iota(I32, (C, C), 0)
    col = lax.broadcasted_iota(I32, (C, C), 1)
    tri = (row >= col).astype(F32)
    row_c = lax.broadcasted_iota(I32, (C, 1), 0)
    row_8 = lax.broadcasted_iota(I32, (R8, 1), 0)
    lane_c = lax.broadcasted_iota(I32, (1, C), 1)
    ones = jnp.ones((HG_DK, LANES), MXU_DTYPE)
    nw = nw_ref[...]

    heads = range(HG_HEADS)
    hcols = [slice(h * HG_DK, (h + 1) * HG_DK) for h in heads]

    def chunk(c, carry):
        sl = pl.ds(pl.multiple_of(c * C, C), C)
        q = [q_ref[sl, hc] for hc in hcols]
        k = [k_ref[sl, hc] for hc in hcols]
        v = [v_ref[sl, hc] for hc in hcols]
        bc = [jnp.dot(tri, f_ref[sl, hc], precision=lax.Precision.HIGHEST, preferred_element_type=F32)
              for hc in hcols]
        bt = [b[C - 1:C, :] for b in bc]
        o = [_mm_nt(q[h] * jnp.exp(bc[h]), st_scr[h]) for h in heads]
        upd = [_mm(v[h].T, k[h] * jnp.exp(bt[h] - bc[h])) for h in heads]
        for h in heads:
            st_scr[h] = st_scr[h] * jnp.exp(bt[h]) + upd[h]
        a_off = [[None] * HG_HEADS for _ in range(C // SC)]
        for i in range(1, C // SC):
            r0 = i * SC
            for h in heads:
                anchor = bc[h][r0:r0 + 1, :]
                qd = q[h][r0:r0 + SC, :] * jnp.exp(bc[h][r0:r0 + SC, :] - anchor)
                kd_i = jnp.where(row_c < r0, k[h] * jnp.exp(jnp.minimum(anchor - bc[h], 0.0)), 0.0)
                a_off[i][h] = _mm_nt(qd, kd_i)
        red = []
        for h in heads:
            prods = []
            for i in range(C // SC):
                r0 = i * SC
                q_i, k_i, bc_i = q[h][r0:r0 + SC, :], k[h][r0:r0 + SC, :], bc[h][r0:r0 + SC, :]
                for s in range(SC):
                    lo = (s // R8) * R8
                    k_s = k_i[s:s + 1, :]
                    b_s = bc_i[s:s + 1, :]
                    e = jnp.where(row_8 + lo >= s, jnp.exp(bc_i[lo:lo + R8, :] - b_s), 0.0)
                    prods.append(q_i[lo:lo + R8, :] * e * k_s)
                    for r in range(lo + R8, SC, R8):
                        prods.append(q_i[r:r + R8, :] * jnp.exp(bc_i[r:r + R8, :] - b_s) * k_s)
            red.append(_mm(jnp.concatenate(prods, axis=0), ones))
        for h in heads:
            blocks = []
            off = 0
            for i in range(C // SC):
                r0 = i * SC
                a_i = jnp.zeros((SC, C), F32) if i == 0 else a_off[i][h]
                groups = [a_i[r:r + R8, :] for r in range(0, SC, R8)]
                for s in range(SC):
                    for r in range((s // R8) * R8, SC, R8):
                        groups[r // R8] = jnp.where(lane_c == r0 + s, red[h][off:off + R8, 0:C], groups[r // R8])
                        off += R8
                blocks.extend(groups)
            o[h] = o[h] + _mm(jnp.concatenate(blocks, axis=0), v[h])
        for h in heads:
            on = o[h] * lax.rsqrt(jnp.mean(o[h] * o[h], axis=-1, keepdims=True) + RMS_EPS) * nw
            y_ref[sl, hcols[h]] = (on * og_ref[sl, hcols[h]]).astype(y_ref.dtype)
        return carry

    lax.fori_loop(0, n_chunks, chunk, 0)

    @pl.when(g == pl.num_programs(1) - 1)
    def _():
        for h in range(HG_HEADS):
            s_ref[h] = st_scr[h].T


def _hgrn(qh, lf, kk, vh, og, norm_w, s0, batch, seq, chunks_per_step):
    n_chunks = seq // CHUNK
    g_sz = min(chunks_per_step, n_chunks)
    steps = n_chunks // g_sz
    tb = g_sz * CHUNK
    r3 = lambda a: a.reshape(batch, seq, HG_WIDTH)
    tok = pl.BlockSpec((None, tb, HG_WIDTH), lambda b, g: (b, g, 0))
    st = pl.BlockSpec((None, HG_HEADS, HG_DK, HG_DV), lambda b, g: (b, 0, 0, 0))
    y, s = pl.pallas_call(
        functools.partial(_hgrn_kernel, n_chunks=g_sz),
        grid=(batch, steps),
        in_specs=[tok, tok, tok, tok, tok, pl.BlockSpec((1, HG_DV), lambda b, g: (0, 0)), st],
        out_specs=(tok, st),
        out_shape=(jax.ShapeDtypeStruct((batch, seq, HG_WIDTH), MXU_DTYPE),
                   jax.ShapeDtypeStruct((batch, HG_HEADS, HG_DK, HG_DV), F32)),
        scratch_shapes=[pltpu.VMEM((HG_HEADS, HG_DV, HG_DK), F32)],
        compiler_params=_params("parallel", "arbitrary"),
        name="hgrn",
    )(r3(qh), r3(lf), r3(kk), r3(vh), r3(og), norm_w.reshape(1, HG_DV), s0)
    return y.reshape(batch * seq, HG_WIDTH), s


def _dsa_kernel(kib_ref, iqT_ref, iwT_ref, kb_ref, aqT_ref, vT_ref, o_ref, keys_scr, acc_scr, qh_scr, p_scr,
                *, past, topk, tq, kt):
    i = pl.program_id(1)
    qpos0 = past + i * tq
    nk = (qpos0 + tq + kt - 1) // kt
    lane_q = lax.broadcasted_iota(I32, (1, tq), 1)
    qchunk = (qpos0 + lane_q) // CHUNK
    row_k = lax.broadcasted_iota(I32, (kt, 1), 0)
    tile = lambda j: pl.ds(pl.multiple_of(j * kt, kt), kt)
    fold = lambda a: a.reshape(kt // SUBLANES, SUBLANES, tq)

    def score_tile(j, masked):
        ki = kib_ref[tile(j), :]
        sc = jnp.zeros((kt, tq), F32)
        for h in range(IDX_HEADS):
            raw = jnp.dot(ki, iqT_ref[h * IDX_DIM:(h + 1) * IDX_DIM, :], preferred_element_type=F32)
            sc = sc + jnp.maximum(raw, 0.0) * iwT_ref[h:h + 1, :]
        bits = lax.bitcast_convert_type(sc, I32)
        key = jnp.where(bits < 0, INT_MIN - bits, bits)
        if masked:
            kchunk = (j * kt + row_k) // CHUNK
            key = jnp.where(kchunk <= qchunk, key, INT_MIN)
        keys_scr[tile(j), :] = key

    def score_full(j, carry):
        score_tile(j, False)
        return carry

    lax.fori_loop(0, nk - 1, score_full, 0)
    score_tile(nk - 1, True)

    def count(pred):
        def one(j, acc):
            m = pred(keys_scr[tile(j), :], j * kt + row_k).astype(I32)
            return acc + jnp.sum(fold(m), axis=0)

        def two(jj, acc):
            return one(2 * jj + 1, one(2 * jj, acc))

        acc = lax.fori_loop(0, nk // 2, two, jnp.zeros((SUBLANES, tq), I32))
        acc = lax.fori_loop(2 * (nk // 2), nk, one, acc)
        return jnp.sum(acc, axis=0, keepdims=True)

    def radix(it, carry):
        lo, n_lo = carry
        cand = lo + jnp.left_shift(jnp.int32(1), 31 - it)
        cnt = count(lambda t, pos: t >= cand)
        take = cnt >= topk
        return jnp.where(take, cand, lo), jnp.where(take, cnt, n_lo)

    thr, n_ge = lax.fori_loop(0, 32, radix, (jnp.full((1, tq), INT_MIN, I32), jnp.zeros((1, tq), I32)))
    has_k = thr > INT_MIN
    n_gt = count(lambda t, pos: t > thr)
    need = topk - n_gt
    surplus = jnp.logical_and(has_k, n_ge > topk)

    lmax_bits = max(1, (keys_scr.shape[0]).bit_length())

    def bis(it, jb):
        cand = jb + jnp.left_shift(jnp.int32(1), lmax_bits - 1 - it)
        cnt = count(lambda t, pos: jnp.logical_and(t == thr, pos < cand))
        return jnp.where(cnt <= need, cand, jb)

    n_bis = jnp.where(jnp.max(surplus.astype(I32)) > 0, lmax_bits, 0)
    jbound = lax.fori_loop(0, n_bis, bis, jnp.zeros((1, tq), I32))
    jbound = jnp.where(surplus, jbound, jnp.where(has_k, 2 ** 30, 0))

    half = lax.broadcasted_iota(I32, (LANES, 1), 0) // ATT_DIM
    for h in range(ATT_HEADS):
        pair = aqT_ref[(h // 2) * LANES:(h // 2 + 1) * LANES, :]
        qh_scr[h] = jnp.where(half == (h % 2), pair, jnp.zeros_like(pair))

    def logits(j, h):
        kh = kb_ref[tile(j), (h // 2) * LANES:(h // 2 + 1) * LANES]
        return jnp.dot(kh, qh_scr[h], preferred_element_type=F32)

    def max_tile(j, mx):
        t = keys_scr[tile(j), :]
        pos = j * kt + row_k
        sel = jnp.logical_or(t > thr, jnp.logical_and(t == thr, pos < jbound))
        bias = jnp.where(sel, 0.0, NEG_BIG)
        keys_scr[tile(j), :] = lax.bitcast_convert_type(bias, I32)
        return tuple(jnp.maximum(mx[h], jnp.max(fold(logits(j, h) + bias), axis=0)) for h in range(ATT_HEADS))

    mx = lax.fori_loop(0, nk, max_tile, tuple(jnp.full((SUBLANES, tq), NEG_BIG, F32) for _ in range(ATT_HEADS)))
    m = [jnp.max(mx[h], axis=0, keepdims=True) for h in range(ATT_HEADS)]

    acc_scr[...] = jnp.zeros_like(acc_scr)

    def attend(j, ls):
        bias = lax.bitcast_convert_type(keys_scr[tile(j), :], F32)
        out = []
        for h in range(ATT_HEADS):
            p = jnp.exp(logits(j, h) + bias - m[h])
            out.append(ls[h] + jnp.sum(fold(p), axis=0))
            p_scr[h] = p.astype(p_scr.dtype)
        for h in range(ATT_HEADS):
            rows = slice(h * ATT_DIM, (h + 1) * ATT_DIM)
            acc_scr[rows, :] += jnp.dot(vT_ref[j, rows, :], p_scr[h], preferred_element_type=F32)
        return tuple(out)

    ls = lax.fori_loop(0, nk, attend, tuple(jnp.zeros((SUBLANES, tq), F32) for _ in range(ATT_HEADS)))
    for h in range(ATT_HEADS):
        rows = slice(h * ATT_DIM, (h + 1) * ATT_DIM)
        acc_scr[rows, :] = acc_scr[rows, :] / jnp.sum(ls[h], axis=0, keepdims=True)
    o_ref[...] = acc_scr[...].T.astype(o_ref.dtype)


def _dsa(kib, iqT, iwT, kb, aqT, vT, batch, n_q, l_pad, past, topk, tq, kt):
    assert kt % tq == 0 and past % tq == 0 and tq % CHUNK == 0 and n_q % tq == 0 and l_pad % kt == 0
    nq = n_q // tq
    qcol = lambda r: pl.BlockSpec((r, tq), lambda b, i: (0, b * nq + i))
    return pl.pallas_call(
        functools.partial(_dsa_kernel, past=past, topk=topk, tq=tq, kt=kt),
        grid=(batch, nq),
        in_specs=[pl.BlockSpec((None, l_pad, IDX_DIM), lambda b, i: (b, 0, 0)),
                  qcol(IDX_HEADS * IDX_DIM), qcol(SUBLANES),
                  pl.BlockSpec((None, l_pad, ATT_WIDTH), lambda b, i: (b, 0, 0)),
                  qcol(ATT_WIDTH),
                  pl.BlockSpec((l_pad // kt, ATT_WIDTH, kt), lambda b, i: (b, 0, 0))],
        out_specs=pl.BlockSpec((tq, ATT_WIDTH), lambda b, i: (b * nq + i, 0)),
        out_shape=jax.ShapeDtypeStruct((batch * n_q, ATT_WIDTH), MXU_DTYPE),
        scratch_shapes=[pltpu.VMEM((l_pad, tq), I32), pltpu.VMEM((ATT_WIDTH, tq), F32),
                        pltpu.VMEM((ATT_HEADS, LANES, tq), MXU_DTYPE),
                        pltpu.VMEM((ATT_HEADS, kt, tq), MXU_DTYPE)],
        compiler_params=_params("parallel", "arbitrary"),
        name="dsa",
    )(kib, iqT, iwT, kb, aqT, vT)


def _out_proj_kernel(yh_ref, oa_ref, sgh_ref, sga_ref, x_ref, wbh_ref, wba_ref, wo_ref, g_ref, b_ref,
                     wr_ref, br_ref, x1_ref, comb_ref):
    br_hg = jnp.dot(yh_ref[...], wbh_ref[...], preferred_element_type=F32)
    br_att = jnp.dot(oa_ref[...], wba_ref[...], preferred_element_type=F32)
    merged = sgh_ref[...] * br_hg + sga_ref[...] * br_att
    out = _mm(merged, wo_ref[...])
    x1 = _layer_norm(DN_ALPHA * x_ref[...] + out, g_ref[...], b_ref[...])
    x1_ref[...] = x1
    lg = _mm(x1, wr_ref[...]) + br_ref[...]
    lane = lax.broadcasted_iota(I32, lg.shape, 1).astype(F32)
    ninf = -jnp.inf
    gmask = lane < N_GROUPS
    gl = jnp.where(gmask, lg, ninf)
    gmax = jnp.max(gl, axis=1, keepdims=True)
    gsel = jnp.min(jnp.where(gl == gmax, lane, float(ROUTER_LANES)), axis=1, keepdims=True)
    g_w = 1.0 / jnp.sum(jnp.where(gmask, jnp.exp(gl - gmax), 0.0), axis=1, keepdims=True)
    e0 = EXPERT_LANE0 + EXPERTS_PER_GROUP * gsel
    emask = jnp.logical_and(lane >= e0, lane < e0 + EXPERTS_PER_GROUP)
    el = jnp.where(emask, lg, ninf)
    emax = jnp.max(el, axis=1, keepdims=True)
    ee = jnp.where(emask, jnp.exp(el - emax), 0.0)
    prob = ee / jnp.sum(ee, axis=1, keepdims=True)
    pm = jnp.where(emask, prob, -1.0)
    p1 = jnp.max(pm, axis=1, keepdims=True)
    i1 = jnp.min(jnp.where(pm == p1, lane, float(ROUTER_LANES)), axis=1, keepdims=True)
    pm2 = jnp.where(lane == i1, -1.0, pm)
    p2 = jnp.max(pm2, axis=1, keepdims=True)
    i2 = jnp.min(jnp.where(pm2 == p2, lane, float(ROUTER_LANES)), axis=1, keepdims=True)
    tot = p1 + p2
    comb_ref[...] = (jnp.where(lane == i1, g_w * (p1 / tot), 0.0)
                     + jnp.where(lane == i2, g_w * (p2 / tot), 0.0))


def _out_proj(yh, oa, sgh, sga, x, lw, tm):
    n = x.shape[0]
    tm = min(tm, n)
    full = lambda a: pl.BlockSpec(a.shape, lambda i: (0,) * a.ndim)
    rows = lambda c: pl.BlockSpec((tm, c), lambda i: (i, 0))
    weights = (lw["w_br_hg"], lw["w_br_att"], lw["w_out"], lw["ln1_g"], lw["ln1_b"], lw["w_r"], lw["b_r"])
    return pl.pallas_call(
        _out_proj_kernel,
        grid=(n // tm,),
        in_specs=[rows(HG_WIDTH), rows(ATT_WIDTH), rows(D_MODEL), rows(D_MODEL), rows(D_MODEL)]
        + [full(w) for w in weights],
        out_specs=(rows(D_MODEL), rows(ROUTER_LANES)),
        out_shape=(jax.ShapeDtypeStruct((n, D_MODEL), F32), jax.ShapeDtypeStruct((n, ROUTER_LANES), F32)),
        compiler_params=_params("parallel"),
        name="out_proj",
    )(yh, oa, sgh, sga, x, *weights)


def _moe_kernel(x_ref, comb_ref, wgu_ref, wd_ref, g_ref, b_ref, o_ref, xb_scr, hb_scr, acc_scr):
    grp = pl.program_id(1)

    @pl.when(grp == 0)
    def _():
        xb_scr[...] = x_ref[...].astype(xb_scr.dtype)
        acc_scr[...] = jnp.zeros_like(acc_scr)

    xb = xb_scr[...]
    comb = comb_ref[...]
    lane = lax.broadcasted_iota(I32, comb.shape, 1)
    F = D_EXPERT
    for e in range(EXPERTS_PER_GROUP):
        gu = jnp.dot(xb, wgu_ref[e], preferred_element_type=F32)
        h = _silu(gu[:, 0:F]) * gu[:, F:2 * F]
        c = jnp.sum(jnp.where(lane == EXPERT_LANE0 + grp * EXPERTS_PER_GROUP + e, comb, 0.0),
                    axis=1, keepdims=True)
        hb_scr[:, e * F:(e + 1) * F] = (h * c).astype(hb_scr.dtype)
    acc_scr[...] += jnp.dot(hb_scr[...], wd_ref[...], preferred_element_type=F32)

    @pl.when(grp == pl.num_programs(1) - 1)
    def _():
        o_ref[...] = _layer_norm(DN_ALPHA * x_ref[...] + acc_scr[...], g_ref[...], b_ref[...])


def _moe(x1, comb, lw, tm):
    n = x1.shape[0]
    tm = min(tm, n)
    E, F = EXPERTS_PER_GROUP, D_EXPERT
    return pl.pallas_call(
        _moe_kernel,
        grid=(n // tm, N_GROUPS),
        in_specs=[pl.BlockSpec((tm, D_MODEL), lambda i, g: (i, 0)),
                  pl.BlockSpec((tm, ROUTER_LANES), lambda i, g: (i, 0)),
                  pl.BlockSpec((None, E, D_MODEL, 2 * F), lambda i, g: (g, 0, 0, 0)),
                  pl.BlockSpec((None, E * F, D_MODEL), lambda i, g: (g, 0, 0)),
                  pl.BlockSpec((1, D_MODEL), lambda i, g: (0, 0)),
                  pl.BlockSpec((1, D_MODEL), lambda i, g: (0, 0))],
        out_specs=pl.BlockSpec((tm, D_MODEL), lambda i, g: (i, 0)),
        out_shape=jax.ShapeDtypeStruct((n, D_MODEL), F32),
        scratch_shapes=[pltpu.VMEM((tm, D_MODEL), MXU_DTYPE), pltpu.VMEM((tm, E * F), MXU_DTYPE),
                        pltpu.VMEM((tm, D_MODEL), F32)],
        compiler_params=_params("parallel", "arbitrary"),
        name="moe",
    )(x1, comb, lw["w_gu"], lw["w_d"], lw["ln2_g"], lw["ln2_b"])


def _layer_weights(l, lbs, w_in, hg_norm_w, w_br_hg, w_br_att, w_out, ln1_g, ln1_b, ln2_g, ln2_b,
                   w_rg, b_rg, w_re, b_re, w_gate, w_up, w_down):
    md = MXU_DTYPE
    offs = [0]
    for s in IN_SIZES:
        offs.append(offs[-1] + s)
    w = w_in[l]
    seg = lambda a, b: w[:, offs[a]:offs[b]]
    idx_cols = jnp.concatenate([seg(8, 9), jnp.zeros((D_MODEL, LANES - IDX_DIM), F32)], axis=1)
    iw_rows = jnp.concatenate([seg(9, 10).T, jnp.zeros((SUBLANES - IDX_HEADS, D_MODEL), F32)], axis=0)
    lb = lbs[l]
    lbp = jnp.concatenate([jnp.log(lb)[None], jnp.log1p(-lb)[None], (1.0 - lb)[None],
                           jnp.zeros((SUBLANES - 3, HG_WIDTH), F32)], axis=0)
    w_r = jnp.concatenate([w_rg[l], w_re[l], jnp.zeros((D_MODEL, ROUTER_LANES - N_GROUPS - N_EXPERTS), F32)], axis=1)
    b_r = jnp.concatenate([b_rg[l], b_re[l], jnp.zeros((ROUTER_LANES - N_GROUPS - N_EXPERTS,), F32)])[None]
    return {
        "w_h": seg(0, 4).astype(md), "w_a": seg(5, 7).astype(md), "w_i": idx_cols.astype(md),
        "w_g": seg(10, 12).astype(md),
        "w_aqT": seg(4, 5).T.astype(md), "w_avT": seg(6, 7).T.astype(md), "w_iqT": seg(7, 8).T.astype(md),
        "w_iwT": iw_rows.astype(md), "lbp": lbp,
        "norm_w": hg_norm_w[l],
        "w_br_hg": w_br_hg[l].astype(md), "w_br_att": w_br_att[l].astype(md), "w_out": w_out[l].astype(md),
        "ln1_g": ln1_g[l][None], "ln1_b": ln1_b[l][None], "ln2_g": ln2_g[l][None], "ln2_b": ln2_b[l][None],
        "w_r": w_r.astype(md), "b_r": b_r,
        "w_gu": jnp.concatenate([w_gate[l], w_up[l]], axis=-1).astype(md),
        "w_d": w_down[l].reshape(N_GROUPS, EXPERTS_PER_GROUP * D_EXPERT, D_MODEL).astype(md),
    }


def _lower_bounds(lb_logits):
    p = jax.nn.softmax(lb_logits.astype(F32), axis=0)
    return jnp.concatenate([jnp.zeros_like(p[:1]), jnp.cumsum(p[1:], axis=0)], axis=0)


TM_IN = 256
TM_OUT = 512
TM_MOE = 1024
HGRN_CHUNKS_PER_STEP = 16
DSA_TQ = 128
DSA_KT = 256


def _mixer_and_ffn(x, lw, batch, seq, s0, past, layer, bufs):
    tq, kt = DSA_TQ, DSA_KT
    (qh, lf, kk, vh, og, k_buf, v_buf, kb, ik_buf, kib, aqT, vT, iqT, iwT, sgh, sga) = _in_proj(
        x, lw, TM_IN, kt, layer, bufs)
    y_hg, s_new = _hgrn(qh, lf, kk, vh, og, lw["norm_w"], s0, batch, seq, HGRN_CHUNKS_PER_STEP)
    if past is None:
        l_tot = seq
        l_pad = -(-l_tot // kt) * kt
        assert l_pad == l_tot and seq % tq == 0
        o_att = _dsa(kib.reshape(batch, seq, IDX_DIM), iqT, iwT, kb.reshape(batch, seq, ATT_WIDTH), aqT, vT,
                     batch, seq, l_pad, 0, min(TOPK_MAX, l_tot // 4), tq, kt)
    else:
        k_past, v_past, ki_past = past
        p_len = k_past.shape[1]
        l_tot = p_len + seq
        n_q = -(-seq // tq) * tq
        l_pad = -(-(p_len + n_q) // kt) * kt
        padk = lambda a, new: jnp.concatenate(
            [a, new, jnp.zeros((batch, l_pad - l_tot, a.shape[2]), a.dtype)], axis=1)
        k_all = padk(k_past, kb.reshape(batch, seq, ATT_WIDTH))
        ki_all = padk(ki_past, kib.reshape(batch, seq, IDX_DIM))
        v_all = padk(v_past, v_buf[layer].astype(v_past.dtype).reshape(batch, seq, ATT_WIDTH))
        vT_all = jnp.transpose(v_all.reshape(batch * (l_pad // kt), kt, ATT_WIDTH), (0, 2, 1))
        padq = lambda a: jnp.pad(a.reshape(a.shape[0], batch, seq), ((0, 0), (0, 0), (0, n_q - seq))
                                 ).reshape(a.shape[0], batch * n_q)
        o_pad = _dsa(ki_all, padq(iqT), padq(iwT), k_all, padq(aqT), vT_all,
                     batch, n_q, l_pad, p_len, min(TOPK_MAX, l_tot // 4), tq, kt)
        o_att = o_pad.reshape(batch, n_q, ATT_WIDTH)[:, :seq].reshape(batch * seq, ATT_WIDTH)
    x1, comb = _out_proj(y_hg, o_att, sgh, sga, x, lw, TM_OUT)
    x2 = _moe(x1, comb, lw, TM_MOE)
    return x2, (k_buf, v_buf, ik_buf), s_new


def kernel(x_prompt, x_sample, cache_k, cache_v, cache_idx_k, state_hgrn, w_in, hg_lb_logits, hg_norm_w,
           w_br_hg, w_br_att, w_out, ln1_g, ln1_b, ln2_g, ln2_b, w_rg, b_rg, w_re, b_re, w_gate, w_up, w_down):
    bp, tp, d = x_prompt.shape
    bs, ts, _ = x_sample.shape
    p_len = cache_k.shape[2]
    lbs = _lower_bounds(hg_lb_logits)
    xp = x_prompt.reshape(bp * tp, d)
    xs = x_sample.reshape(bs * ts, d)
    zeros_state = jnp.zeros((bp, HG_HEADS, HG_DK, HG_DV), F32)
    bufs_p, bufs_s, st_p, st_s = None, None, [], []
    for l in range(DEPTH):
        lw = _layer_weights(l, lbs, w_in, hg_norm_w, w_br_hg, w_br_att, w_out, ln1_g, ln1_b, ln2_g, ln2_b,
                            w_rg, b_rg, w_re, b_re, w_gate, w_up, w_down)
        xp, bufs_p, sp = _mixer_and_ffn(xp, lw, bp, tp, zeros_state, None, l, bufs_p)
        past = (cache_k[l].reshape(bs, p_len, ATT_WIDTH).astype(MXU_DTYPE),
                cache_v[l].reshape(bs, p_len, ATT_WIDTH).astype(MXU_DTYPE),
                cache_idx_k[l].astype(MXU_DTYPE))
        xs, bufs_s, ss = _mixer_and_ffn(xs, lw, bs, ts, state_hgrn[l].astype(F32), past, l, bufs_s)
        st_p.append(sp)
        st_s.append(ss)

    def shaped(bufs, b, t):
        k, v, ik = bufs
        return (k.reshape(DEPTH, b, t, ATT_HEADS, ATT_DIM), v.reshape(DEPTH, b, t, ATT_HEADS, ATT_DIM),
                ik.reshape(DEPTH, b, t, IDX_DIM))

    kp, vp, ikp = shaped(bufs_p, bp, tp)
    ks, vs, iks = shaped(bufs_s, bs, ts)
    return (xp.reshape(bp, tp, d), xs.reshape(bs, ts, d), kp, vp, ikp, jnp.stack(st_p), ks, vs, iks,
            jnp.stack(st_s).astype(state_hgrn.dtype))
```

```python
import functools

import jax
import jax.numpy as jnp
from jax import lax
from jax.experimental import pallas as pl
from jax.experimental.pallas import tpu as pltpu

F32 = jnp.float32
I32 = jnp.int32
MXU_DTYPE = jnp.bfloat16

D_MODEL = 1024
DEPTH = 4
CHUNK = 64
HG_HEADS = 4
HG_DK = 128
HG_DV = 128
HG_WIDTH = HG_HEADS * HG_DK
ATT_HEADS = 8
ATT_DIM = 64
ATT_WIDTH = ATT_HEADS * ATT_DIM
IDX_HEADS = 4
IDX_DIM = 64
TOPK_MAX = 256
ATT_SCALE = ATT_DIM ** -0.5
IDX_SCALE = IDX_DIM ** -0.5
IDX_W_SCALE = IDX_HEADS ** -0.5
N_GROUPS = 4
EXPERTS_PER_GROUP = 4
N_EXPERTS = N_GROUPS * EXPERTS_PER_GROUP
D_EXPERT = 256
DN_ALPHA = (2 * DEPTH) ** 0.25
LN_EPS = 1e-5
RMS_EPS = 1e-6
IN_SIZES = (HG_WIDTH, HG_WIDTH, HG_HEADS * HG_DV, HG_HEADS * HG_DV,
            ATT_WIDTH, ATT_WIDTH, ATT_WIDTH, IDX_HEADS * IDX_DIM, IDX_DIM, IDX_HEADS,
            D_MODEL, D_MODEL)

LANES = 128
SUBLANES = 8
PACKED_ROWS = 16
SUBCHUNK = 16
VMEM_LIMIT = 56 * 1024 * 1024
INT_MIN = -2 ** 31
NEG_BIG = -1e30
ROUTER_LANES = LANES
EXPERT_LANE0 = N_GROUPS


def _params(*sem):
    return pltpu.CompilerParams(dimension_semantics=sem, vmem_limit_bytes=VMEM_LIMIT)


def _mm(a, b):
    return jnp.dot(a.astype(MXU_DTYPE), b.astype(MXU_DTYPE), preferred_element_type=F32)


def _mm_nt(a, b):
    return lax.dot_general(a.astype(MXU_DTYPE), b.astype(MXU_DTYPE), (((1,), (1,)), ((), ())),
                           preferred_element_type=F32)


def _sigmoid(x):
    return 1.0 / (1.0 + jnp.exp(-x))


def _silu(x):
    return x * _sigmoid(x)


def _layer_norm(r, g, b):
    mu = jnp.mean(r, axis=-1, keepdims=True)
    d = r - mu
    var = jnp.mean(d * d, axis=-1, keepdims=True)
    return d * lax.rsqrt(var + LN_EPS) * g + b


def _in_proj_kernel(x_ref, wh_ref, wa_ref, wi_ref, wg_ref, waqT_ref, wavT_ref, wiqT_ref, wiwT_ref,
                    lbp_ref, *refs):
    (qh_ref, lf_ref, kk_ref, vh_ref, og_ref, k32_ref, v32_ref, kb_ref, ik32_ref, kib_ref,
     aqT_ref, vT_ref, iqT_ref, iwT_ref, sgh_ref, sga_ref) = refs[-16:]
    xb = x_ref[...].astype(MXU_DTYPE)
    W = HG_WIDTH
    qh_ref[...] = _silu(_mm(xb, wh_ref[:, 0:W]))
    z = _mm(xb, wh_ref[:, W:2 * W])
    log_lb = lbp_ref[0:1, :]
    log_1mlb = lbp_ref[1:2, :]
    one_mlb = lbp_ref[2:3, :]
    log_sig = jnp.minimum(z, 0.0) - jnp.log(1.0 + jnp.exp(-jnp.abs(z)))
    b = log_1mlb + log_sig
    lf_ref[...] = jnp.maximum(log_lb, b) + jnp.log(1.0 + jnp.exp(-jnp.abs(log_lb - b)))
    kk_ref[...] = one_mlb * _sigmoid(-z)
    vh_ref[...] = _mm(xb, wh_ref[:, 2 * W:3 * W])
    og_ref[...] = _silu(_mm(xb, wh_ref[:, 3 * W:4 * W]))
    A = ATT_WIDTH
    k = _mm(xb, wa_ref[:, 0:A])
    k32_ref[...] = k
    kb_ref[...] = k.astype(kb_ref.dtype)
    v32_ref[...] = _mm(xb, wa_ref[:, A:2 * A])
    aqT_ref[...] = (_mm_nt(waqT_ref[...], xb) * ATT_SCALE).astype(aqT_ref.dtype)
    kt = vT_ref.shape[-1]
    for t in range(vT_ref.shape[0]):
        vT_ref[t] = _mm_nt(wavT_ref[...], xb[t * kt:(t + 1) * kt, :]).astype(vT_ref.dtype)
    ik = _mm(xb, wi_ref[...])[:, 0:IDX_DIM]
    ik32_ref[...] = ik
    kib_ref[...] = ik.astype(kib_ref.dtype)
    iqT_ref[...] = _mm_nt(wiqT_ref[...], xb).astype(iqT_ref.dtype)
    iwT_ref[...] = _mm_nt(wiwT_ref[...], xb) * (IDX_SCALE * IDX_W_SCALE)
    D = D_MODEL
    sgh_ref[...] = _sigmoid(_mm(xb, wg_ref[:, 0:D]))
    sga_ref[...] = _sigmoid(_mm(xb, wg_ref[:, D:2 * D]))


def _in_proj(x, lw, tm, kt, layer, bufs):
    n = x.shape[0]
    tm = min(tm, n)
    assert tm % kt == 0 and n % tm == 0
    grid = (n // tm,)
    full = lambda a: pl.BlockSpec(a.shape, lambda i: (0,) * a.ndim)
    rows = lambda c: pl.BlockSpec((tm, c), lambda i: (i, 0))
    cols = lambda r: pl.BlockSpec((r, tm), lambda i: (0, i))
    lrows = lambda c: pl.BlockSpec((None, tm, c), lambda i: (layer, i, 0))
    weights = (lw["w_h"], lw["w_a"], lw["w_i"], lw["w_g"], lw["w_aqT"], lw["w_avT"], lw["w_iqT"], lw["w_iwT"],
               lw["lbp"])
    out_shape = (
        jax.ShapeDtypeStruct((n, HG_WIDTH), F32),
        jax.ShapeDtypeStruct((n, HG_WIDTH), F32),
        jax.ShapeDtypeStruct((n, HG_WIDTH), F32),
        jax.ShapeDtypeStruct((n, HG_WIDTH), F32),
        jax.ShapeDtypeStruct((n, HG_WIDTH), F32),
        jax.ShapeDtypeStruct((DEPTH, n, ATT_WIDTH), F32),
        jax.ShapeDtypeStruct((DEPTH, n, ATT_WIDTH), F32),
        jax.ShapeDtypeStruct((n, ATT_WIDTH), MXU_DTYPE),
        jax.ShapeDtypeStruct((DEPTH, n, IDX_DIM), F32),
        jax.ShapeDtypeStruct((n, IDX_DIM), MXU_DTYPE),
        jax.ShapeDtypeStruct((ATT_WIDTH, n), MXU_DTYPE),
        jax.ShapeDtypeStruct((n // kt, ATT_WIDTH, kt), MXU_DTYPE),
        jax.ShapeDtypeStruct((IDX_HEADS * IDX_DIM, n), MXU_DTYPE),
        jax.ShapeDtypeStruct((SUBLANES, n), F32),
        jax.ShapeDtypeStruct((n, D_MODEL), F32),
        jax.ShapeDtypeStruct((n, D_MODEL), F32),
    )
    out_specs = (rows(HG_WIDTH),) * 5 + (lrows(ATT_WIDTH), lrows(ATT_WIDTH), rows(ATT_WIDTH),
                                         lrows(IDX_DIM), rows(IDX_DIM)) + (
        cols(ATT_WIDTH), pl.BlockSpec((tm // kt, ATT_WIDTH, kt), lambda i: (i, 0, 0)),
        cols(IDX_HEADS * IDX_DIM), cols(SUBLANES), rows(D_MODEL), rows(D_MODEL))
    in_specs = [rows(D_MODEL)] + [full(w) for w in weights]
    aliases = {}
    if bufs is not None:
        first = len(in_specs)
        in_specs += [pl.BlockSpec(memory_space=pl.ANY)] * len(bufs)
        aliases = {first: 5, first + 1: 6, first + 2: 8}
    return pl.pallas_call(
        _in_proj_kernel,
        grid=grid,
        in_specs=in_specs,
        out_specs=out_specs,
        out_shape=out_shape,
        input_output_aliases=aliases,
        compiler_params=_params("parallel"),
        name="in_proj",
    )(x, *weights, *(bufs or ()))


def _hgrn_kernel(q_ref, f_ref, k_ref, v_ref, og_ref, nw_ref, s0_ref, y_ref, s_ref, st_scr, *, n_chunks):
    g = pl.program_id(1)

    @pl.when(g == 0)
    def _():
        for h in range(HG_HEADS):
            st_scr[h] = s0_ref[h].T

    C, SC, R8 = CHUNK, SUBCHUNK, SUBLANES
    row = lax.broadcasted_iota(I32, (C, C), 0)
    col = lax.broadcasted_iota(I32, (C, C), 1)
    tri = (row >= col).astype(F32)
    row_c = lax.broadcasted_iota(I32, (C, 1), 0)
    row_8 = lax.broadcasted_iota(I32, (R8, 1), 0)
    lane_c = lax.broadcasted_iota(I32, (1, C), 1)
    ones = jnp.ones((HG_DK, LANES), MXU_DTYPE)
    nw = nw_ref[...]

    heads = range(HG_HEADS)
    hcols = [slice(h * HG_DK, (h + 1) * HG_DK) for h in heads]

    def chunk(c, carry):
        sl = pl.ds(pl.multiple_of(c * C, C), C)
        q = [q_ref[sl, hc] for hc in hcols]
        k = [k_ref[sl, hc] for hc in hcols]
        v = [v_ref[sl, hc] for hc in hcols]
        bc = [jnp.dot(tri, f_ref[sl, hc], precision=lax.Precision.HIGHEST, preferred_element_type=F32)
              for hc in hcols]
        bt = [b[C - 1:C, :] for b in bc]
        o = [_mm_nt(q[h] * jnp.exp(bc[h]), st_scr[h]) for h in heads]
        upd = [_mm(v[h].T, k[h] * jnp.exp(bt[h] - bc[h])) for h in heads]
        for h in heads:
            st_scr[h] = st_scr[h] * jnp.exp(bt[h]) + upd[h]
        a_off = [[None] * HG_HEADS for _ in range(C // SC)]
        for i in range(1, C // SC):
            r0 = i * SC
            for h in heads:
                anchor = bc[h][r0:r0 + 1, :]
                qd = q[h][r0:r0 + SC, :] * jnp.exp(bc[h][r0:r0 + SC, :] - anchor)
                kd_i = jnp.where(row_c < r0, k[h] * jnp.exp(jnp.minimum(anchor - bc[h], 0.0)), 0.0)
                a_off[i][h] = _mm_nt(qd, kd_i)
        red = []
        for h in heads:
            prods = []
            for i in range(C // SC):
                r0 = i * SC
                q_i, k_i, bc_i = q[h][r0:r0 + SC, :], k[h][r0:r0 + SC, :], bc[h][r0:r0 + SC, :]
                for s in range(SC):
                    lo = (s // R8) * R8
                    k_s = k_i[s:s + 1, :]
                    b_s = bc_i[s:s + 1, :]
                    e = jnp.where(row_8 + lo >= s, jnp.exp(bc_i[lo:lo + R8, :] - b_s), 0.0)
                    prods.append(q_i[lo:lo + R8, :] * e * k_s)
                    for r in range(lo + R8, SC, R8):
                        prods.append(q_i[r:r + R8, :] * jnp.exp(bc_i[r:r + R8, :] - b_s) * k_s)
            red.append(_mm(jnp.concatenate(prods, axis=0), ones))
        for h in heads:
            blocks = []
            off = 0
            for i in range(C // SC):
                r0 = i * SC
                a_i = jnp.zeros((SC, C), F32) if i == 0 else a_off[i][h]
                groups = [a_i[r:r + R8, :] for r in range(0, SC, R8)]
                for s in range(SC):
                    for r in range((s // R8) * R8, SC, R8):
                        groups[r // R8] = jnp.where(lane_c == r0 + s, red[h][off:off + R8, 0:C], groups[r // R8])
                        off += R8
                blocks.extend(groups)
            o[h] = o[h] + _mm(jnp.concatenate(blocks, axis=0), v[h])
        for h in heads:
            on = o[h] * lax.rsqrt(jnp.mean(o[h] * o[h], axis=-1, keepdims=True) + RMS_EPS) * nw
            y_ref[sl, hcols[h]] = (on * og_ref[sl, hcols[h]]).astype(y_ref.dtype)
        return carry

    lax.fori_loop(0, n_chunks, chunk, 0)

    @pl.when(g == pl.num_programs(1) - 1)
    def _():
        for h in range(HG_HEADS):
            s_ref[h] = st_scr[h].T


def _hgrn(qh, lf, kk, vh, og, norm_w, s0, batch, seq, chunks_per_step):
    n_chunks = seq // CHUNK
    g_sz = min(chunks_per_step, n_chunks)
    steps = n_chunks // g_sz
    tb = g_sz * CHUNK
    r3 = lambda a: a.reshape(batch, seq, HG_WIDTH)
    tok = pl.BlockSpec((None, tb, HG_WIDTH), lambda b, g: (b, g, 0))
    st = pl.BlockSpec((None, HG_HEADS, HG_DK, HG_DV), lambda b, g: (b, 0, 0, 0))
    y, s = pl.pallas_call(
        functools.partial(_hgrn_kernel, n_chunks=g_sz),
        grid=(batch, steps),
        in_specs=[tok, tok, tok, tok, tok, pl.BlockSpec((1, HG_DV), lambda b, g: (0, 0)), st],
        out_specs=(tok, st),
        out_shape=(jax.ShapeDtypeStruct((batch, seq, HG_WIDTH), MXU_DTYPE),
                   jax.ShapeDtypeStruct((batch, HG_HEADS, HG_DK, HG_DV), F32)),
        scratch_shapes=[pltpu.VMEM((HG_HEADS, HG_DV, HG_DK), F32)],
        compiler_params=_params("parallel", "arbitrary"),
        name="hgrn",
    )(r3(qh), r3(lf), r3(kk), r3(vh), r3(og), norm_w.reshape(1, HG_DV), s0)
    return y.reshape(batch * seq, HG_WIDTH), s


def _dsa_kernel(kib_ref, iqT_ref, iwT_ref, kb_ref, aqT_ref, vT_ref, o_ref, keys_scr, acc_scr, qh_scr, p_scr,
                lg_scr, half_scr,
                *, past, topk, tq, kt):
    i = pl.program_id(1)
    qpos0 = past + i * tq
    nk = (qpos0 + tq + kt - 1) // kt
    lane_q = lax.broadcasted_iota(I32, (1, tq), 1)
    qchunk = (qpos0 + lane_q) // CHUNK
    row_k = lax.broadcasted_iota(I32, (kt, 1), 0)
    tile = lambda j: pl.ds(pl.multiple_of(j * kt, kt), kt)
    fold = lambda a: a.reshape(kt // SUBLANES, SUBLANES, tq)

    def score_tile(j, masked):
        ki = kib_ref[tile(j), :]
        sc = jnp.zeros((kt, tq), F32)
        for h in range(IDX_HEADS):
            raw = jnp.dot(ki, iqT_ref[h * IDX_DIM:(h + 1) * IDX_DIM, :], preferred_element_type=F32)
            sc = sc + jnp.maximum(raw, 0.0) * iwT_ref[h:h + 1, :]
        bits = lax.bitcast_convert_type(sc, I32)
        key = jnp.where(bits < 0, INT_MIN - bits, bits)
        if masked:
            kchunk = (j * kt + row_k) // CHUNK
            key = jnp.where(kchunk <= qchunk, key, INT_MIN)
        keys_scr[tile(j), :] = key

    def score_full(j, carry):
        score_tile(j, False)
        return carry

    lax.fori_loop(0, nk - 1, score_full, 0)
    score_tile(nk - 1, True)

    def count(pred):
        def one(j, acc):
            m = pred(keys_scr[tile(j), :], j * kt + row_k).astype(I32)
            return acc + jnp.sum(fold(m), axis=0)

        def two(jj, acc):
            return one(2 * jj + 1, one(2 * jj, acc))

        acc = lax.fori_loop(0, nk // 2, two, jnp.zeros((SUBLANES, tq), I32))
        acc = lax.fori_loop(2 * (nk // 2), nk, one, acc)
        return jnp.sum(acc, axis=0, keepdims=True)

    I16 = jnp.int16
    HALF_BITS, HALF_MIN = 16, -2 ** 15
    fold16 = lambda a: a.reshape(kt // PACKED_ROWS, PACKED_ROWS, tq)

    def count16(cand):
        c16 = jnp.broadcast_to(cand, (PACKED_ROWS, tq)).astype(I16)

        def one(j, acc):
            m = (fold16(half_scr[tile(j), :]) >= c16).astype(I16)
            for r in range(kt // PACKED_ROWS):
                acc = acc + m[r]
            return acc

        def two(jj, acc):
            return one(2 * jj + 1, one(2 * jj, acc))

        acc = lax.fori_loop(0, nk // 2, two, jnp.zeros((PACKED_ROWS, tq), I16))
        acc = lax.fori_loop(2 * (nk // 2), nk, one, acc)
        return jnp.sum(acc.astype(I32), axis=0, keepdims=True)

    def select16(kth, n_init):
        def step(it, carry):
            lo, n_lo = carry
            cand = lo + jnp.left_shift(jnp.int32(1), HALF_BITS - 1 - it)
            cnt = count16(cand)
            take = cnt >= kth
            return jnp.where(take, cand, lo), jnp.where(take, cnt, n_lo)
        return lax.fori_loop(0, HALF_BITS, step, (jnp.full((1, tq), HALF_MIN, I32), n_init))

    def split_hi(j, carry):
        half_scr[tile(j), :] = (keys_scr[tile(j), :] >> HALF_BITS).astype(I16)
        return carry

    lax.fori_loop(0, nk, split_hi, 0)
    t_hi, n_ge_hi = select16(topk, jnp.zeros((1, tq), I32))

    def split_lo(j, acc):
        t = keys_scr[tile(j), :]
        hi = t >> HALF_BITS
        lo_s = (t & (2 ** HALF_BITS - 1)) + HALF_MIN
        half_scr[tile(j), :] = jnp.where(hi == t_hi, lo_s, HALF_MIN).astype(I16)
        return acc + jnp.sum(fold((hi > t_hi).astype(I32)), axis=0)

    n_above = jnp.sum(lax.fori_loop(0, nk, split_lo, jnp.zeros((SUBLANES, tq), I32)), axis=0, keepdims=True)
    t_lo, n_eq_hi = select16(topk - n_above, n_ge_hi - n_above)
    thr = t_hi * 2 ** HALF_BITS + (t_lo - HALF_MIN)
    n_ge = n_above + n_eq_hi
    has_k = thr > INT_MIN
    n_gt = count(lambda t, pos: t > thr)
    need = topk - n_gt
    surplus = jnp.logical_and(has_k, n_ge > topk)

    lmax_bits = max(1, (keys_scr.shape[0]).bit_length())

    def bis(it, jb):
        cand = jb + jnp.left_shift(jnp.int32(1), lmax_bits - 1 - it)
        cnt = count(lambda t, pos: jnp.logical_and(t == thr, pos < cand))
        return jnp.where(cnt <= need, cand, jb)

    n_bis = jnp.where(jnp.max(surplus.astype(I32)) > 0, lmax_bits, 0)
    jbound = lax.fori_loop(0, n_bis, bis, jnp.zeros((1, tq), I32))
    jbound = jnp.where(surplus, jbound, jnp.where(has_k, 2 ** 30, 0))

    half = lax.broadcasted_iota(I32, (LANES, 1), 0) // ATT_DIM
    for h in range(ATT_HEADS):
        pair = aqT_ref[(h // 2) * LANES:(h // 2 + 1) * LANES, :]
        qh_scr[h] = jnp.where(half == (h % 2), pair, jnp.zeros_like(pair))

    def logits(j, h):
        kh = kb_ref[tile(j), (h // 2) * LANES:(h // 2 + 1) * LANES]
        return jnp.dot(kh, qh_scr[h], preferred_element_type=F32)

    acc_scr[...] = jnp.zeros_like(acc_scr)
    heads = range(ATT_HEADS)

    def attend(j, carry):
        ms, ls = carry
        t = keys_scr[tile(j), :]
        pos = j * kt + row_k
        sel = jnp.logical_or(t > thr, jnp.logical_and(t == thr, pos < jbound))
        bias = jnp.where(sel, 0.0, NEG_BIG)
        tmax = []
        for h in heads:
            x = logits(j, h) + bias
            lg_scr[h] = x
            tmax.append(jnp.max(fold(x), axis=0))
        m_new = [jnp.maximum(ms[h], jnp.max(tmax[h], axis=0, keepdims=True)) for h in heads]
        alpha = [jnp.exp(ms[h] - m_new[h]) for h in heads]
        l_new = []
        for h in heads:
            p = jnp.exp(lg_scr[h] - m_new[h])
            l_new.append(alpha[h] * ls[h] + jnp.sum(fold(p), axis=0))
            p_scr[h] = p.astype(p_scr.dtype)
        for h in heads:
            rows = slice(h * ATT_DIM, (h + 1) * ATT_DIM)
            acc_scr[rows, :] = acc_scr[rows, :] * alpha[h] + jnp.dot(vT_ref[j, rows, :], p_scr[h],
                                                                     preferred_element_type=F32)
        return tuple(m_new), tuple(l_new)

    init = (tuple(jnp.full((1, tq), NEG_BIG, F32) for _ in heads),
            tuple(jnp.zeros((SUBLANES, tq), F32) for _ in heads))
    _, ls = lax.fori_loop(0, nk, attend, init)
    for h in heads:
        rows = slice(h * ATT_DIM, (h + 1) * ATT_DIM)
        acc_scr[rows, :] = acc_scr[rows, :] / jnp.sum(ls[h], axis=0, keepdims=True)
    o_ref[...] = acc_scr[...].T.astype(o_ref.dtype)


def _dsa(kib, iqT, iwT, kb, aqT, vT, batch, n_q, l_pad, past, topk, tq, kt):
    assert kt % tq == 0 and past % tq == 0 and tq % CHUNK == 0 and n_q % tq == 0 and l_pad % kt == 0
    nq = n_q // tq
    qcol = lambda r: pl.BlockSpec((r, tq), lambda b, i: (0, b * nq + i))
    return pl.pallas_call(
        functools.partial(_dsa_kernel, past=past, topk=topk, tq=tq, kt=kt),
        grid=(batch, nq),
        in_specs=[pl.BlockSpec((None, l_pad, IDX_DIM), lambda b, i: (b, 0, 0)),
                  qcol(IDX_HEADS * IDX_DIM), qcol(SUBLANES),
                  pl.BlockSpec((None, l_pad, ATT_WIDTH), lambda b, i: (b, 0, 0)),
                  qcol(ATT_WIDTH),
                  pl.BlockSpec((l_pad // kt, ATT_WIDTH, kt), lambda b, i: (b, 0, 0))],
        out_specs=pl.BlockSpec((tq, ATT_WIDTH), lambda b, i: (b * nq + i, 0)),
        out_shape=jax.ShapeDtypeStruct((batch * n_q, ATT_WIDTH), MXU_DTYPE),
        scratch_shapes=[pltpu.VMEM((l_pad, tq), I32), pltpu.VMEM((ATT_WIDTH, tq), F32),
                        pltpu.VMEM((ATT_HEADS, LANES, tq), MXU_DTYPE),
                        pltpu.VMEM((ATT_HEADS, kt, tq), MXU_DTYPE),
                        pltpu.VMEM((ATT_HEADS, kt, tq), F32),
                        pltpu.VMEM((l_pad, tq), jnp.int16)],
        compiler_params=_params("parallel", "arbitrary"),
        name="dsa",
    )(kib, iqT, iwT, kb, aqT, vT)


def _out_proj_kernel(yh_ref, oa_ref, sgh_ref, sga_ref, x_ref, wbh_ref, wba_ref, wo_ref, g_ref, b_ref,
                     wr_ref, br_ref, x1_ref, comb_ref):
    br_hg = jnp.dot(yh_ref[...], wbh_ref[...], preferred_element_type=F32)
    br_att = jnp.dot(oa_ref[...], wba_ref[...], preferred_element_type=F32)
    merged = sgh_ref[...] * br_hg + sga_ref[...] * br_att
    out = _mm(merged, wo_ref[...])
    x1 = _layer_norm(DN_ALPHA * x_ref[...] + out, g_ref[...], b_ref[...])
    x1_ref[...] = x1
    lg = _mm(x1, wr_ref[...]) + br_ref[...]
    lane = lax.broadcasted_iota(I32, lg.shape, 1).astype(F32)
    ninf = -jnp.inf
    gmask = lane < N_GROUPS
    gl = jnp.where(gmask, lg, ninf)
    gmax = jnp.max(gl, axis=1, keepdims=True)
    gsel = jnp.min(jnp.where(gl == gmax, lane, float(ROUTER_LANES)), axis=1, keepdims=True)
    g_w = 1.0 / jnp.sum(jnp.where(gmask, jnp.exp(gl - gmax), 0.0), axis=1, keepdims=True)
    e0 = EXPERT_LANE0 + EXPERTS_PER_GROUP * gsel
    emask = jnp.logical_and(lane >= e0, lane < e0 + EXPERTS_PER_GROUP)
    el = jnp.where(emask, lg, ninf)
    emax = jnp.max(el, axis=1, keepdims=True)
    ee = jnp.where(emask, jnp.exp(el - emax), 0.0)
    prob = ee / jnp.sum(ee, axis=1, keepdims=True)
    pm = jnp.where(emask, prob, -1.0)
    p1 = jnp.max(pm, axis=1, keepdims=True)
    i1 = jnp.min(jnp.where(pm == p1, lane, float(ROUTER_LANES)), axis=1, keepdims=True)
    pm2 = jnp.where(lane == i1, -1.0, pm)
    p2 = jnp.max(pm2, axis=1, keepdims=True)
    i2 = jnp.min(jnp.where(pm2 == p2, lane, float(ROUTER_LANES)), axis=1, keepdims=True)
    tot = p1 + p2
    comb_ref[...] = (jnp.where(lane == i1, g_w * (p1 / tot), 0.0)
                     + jnp.where(lane == i2, g_w * (p2 / tot), 0.0))


def _out_proj(yh, oa, sgh, sga, x, lw, tm):
    n = x.shape[0]
    tm = min(tm, n)
    full = lambda a: pl.BlockSpec(a.shape, lambda i: (0,) * a.ndim)
    rows = lambda c: pl.BlockSpec((tm, c), lambda i: (i, 0))
    weights = (lw["w_br_hg"], lw["w_br_att"], lw["w_out"], lw["ln1_g"], lw["ln1_b"], lw["w_r"], lw["b_r"])
    return pl.pallas_call(
        _out_proj_kernel,
        grid=(n // tm,),
        in_specs=[rows(HG_WIDTH), rows(ATT_WIDTH), rows(D_MODEL), rows(D_MODEL), rows(D_MODEL)]
        + [full(w) for w in weights],
        out_specs=(rows(D_MODEL), rows(ROUTER_LANES)),
        out_shape=(jax.ShapeDtypeStruct((n, D_MODEL), F32), jax.ShapeDtypeStruct((n, ROUTER_LANES), F32)),
        compiler_params=_params("parallel"),
        name="out_proj",
    )(yh, oa, sgh, sga, x, *weights)


def _moe_kernel(x_ref, comb_ref, wgu_ref, wd_ref, g_ref, b_ref, o_ref, xb_scr, hb_scr, acc_scr):
    grp = pl.program_id(1)

    @pl.when(grp == 0)
    def _():
        xb_scr[...] = x_ref[...].astype(xb_scr.dtype)
        acc_scr[...] = jnp.zeros_like(acc_scr)

    xb = xb_scr[...]
    comb = comb_ref[...]
    lane = lax.broadcasted_iota(I32, comb.shape, 1)
    F = D_EXPERT
    for e in range(EXPERTS_PER_GROUP):
        gu = jnp.dot(xb, wgu_ref[e], preferred_element_type=F32)
        h = _silu(gu[:, 0:F]) * gu[:, F:2 * F]
        c = jnp.sum(jnp.where(lane == EXPERT_LANE0 + grp * EXPERTS_PER_GROUP + e, comb, 0.0),
                    axis=1, keepdims=True)
        hb_scr[:, e * F:(e + 1) * F] = (h * c).astype(hb_scr.dtype)
    acc_scr[...] += jnp.dot(hb_scr[...], wd_ref[...], preferred_element_type=F32)

    @pl.when(grp == pl.num_programs(1) - 1)
    def _():
        o_ref[...] = _layer_norm(DN_ALPHA * x_ref[...] + acc_scr[...], g_ref[...], b_ref[...])


def _moe(x1, comb, lw, tm):
    n = x1.shape[0]
    tm = min(tm, n)
    E, F = EXPERTS_PER_GROUP, D_EXPERT
    return pl.pallas_call(
        _moe_kernel,
        grid=(n // tm, N_GROUPS),
        in_specs=[pl.BlockSpec((tm, D_MODEL), lambda i, g: (i, 0)),
                  pl.BlockSpec((tm, ROUTER_LANES), lambda i, g: (i, 0)),
                  pl.BlockSpec((None, E, D_MODEL, 2 * F), lambda i, g: (g, 0, 0, 0)),
                  pl.BlockSpec((None, E * F, D_MODEL), lambda i, g: (g, 0, 0)),
                  pl.BlockSpec((1, D_MODEL), lambda i, g: (0, 0)),
                  pl.BlockSpec((1, D_MODEL), lambda i, g: (0, 0))],
        out_specs=pl.BlockSpec((tm, D_MODEL), lambda i, g: (i, 0)),
        out_shape=jax.ShapeDtypeStruct((n, D_MODEL), F32),
        scratch_shapes=[pltpu.VMEM((tm, D_MODEL), MXU_DTYPE), pltpu.VMEM((tm, E * F), MXU_DTYPE),
                        pltpu.VMEM((tm, D_MODEL), F32)],
        compiler_params=_params("parallel", "arbitrary"),
        name="moe",
    )(x1, comb, lw["w_gu"], lw["w_d"], lw["ln2_g"], lw["ln2_b"])


def _layer_weights(l, lbs, w_in, hg_norm_w, w_br_hg, w_br_att, w_out, ln1_g, ln1_b, ln2_g, ln2_b,
                   w_rg, b_rg, w_re, b_re, w_gate, w_up, w_down):
    md = MXU_DTYPE
    offs = [0]
    for s in IN_SIZES:
        offs.append(offs[-1] + s)
    w = w_in[l]
    seg = lambda a, b: w[:, offs[a]:offs[b]]
    idx_cols = jnp.concatenate([seg(8, 9), jnp.zeros((D_MODEL, LANES - IDX_DIM), F32)], axis=1)
    iw_rows = jnp.concatenate([seg(9, 10).T, jnp.zeros((SUBLANES - IDX_HEADS, D_MODEL), F32)], axis=0)
    lb = lbs[l]
    lbp = jnp.concatenate([jnp.log(lb)[None], jnp.log1p(-lb)[None], (1.0 - lb)[None],
                           jnp.zeros((SUBLANES - 3, HG_WIDTH), F32)], axis=0)
    w_r = jnp.concatenate([w_rg[l], w_re[l], jnp.zeros((D_MODEL, ROUTER_LANES - N_GROUPS - N_EXPERTS), F32)], axis=1)
    b_r = jnp.concatenate([b_rg[l], b_re[l], jnp.zeros((ROUTER_LANES - N_GROUPS - N_EXPERTS,), F32)])[None]
    return {
        "w_h": seg(0, 4).astype(md), "w_a": seg(5, 7).astype(md), "w_i": idx_cols.astype(md),
        "w_g": seg(10, 12).astype(md),
        "w_aqT": seg(4, 5).T.astype(md), "w_avT": seg(6, 7).T.astype(md), "w_iqT": seg(7, 8).T.astype(md),
        "w_iwT": iw_rows.astype(md), "lbp": lbp,
        "norm_w": hg_norm_w[l],
        "w_br_hg": w_br_hg[l].astype(md), "w_br_att": w_br_att[l].astype(md), "w_out": w_out[l].astype(md),
        "ln1_g": ln1_g[l][None], "ln1_b": ln1_b[l][None], "ln2_g": ln2_g[l][None], "ln2_b": ln2_b[l][None],
        "w_r": w_r.astype(md), "b_r": b_r,
        "w_gu": jnp.concatenate([w_gate[l], w_up[l]], axis=-1).astype(md),
        "w_d": w_down[l].reshape(N_GROUPS, EXPERTS_PER_GROUP * D_EXPERT, D_MODEL).astype(md),
    }


def _lower_bounds(lb_logits):
    p = jax.nn.softmax(lb_logits.astype(F32), axis=0)
    return jnp.concatenate([jnp.zeros_like(p[:1]), jnp.cumsum(p[1:], axis=0)], axis=0)


TM_IN = 256
TM_OUT = 512
TM_MOE = 1024
HGRN_CHUNKS_PER_STEP = 16
DSA_TQ_PROMPT = 256
DSA_TQ_SAMPLE = 128
DSA_KT = 256


def _mixer_and_ffn(x, lw, batch, seq, s0, past, layer, bufs):
    tq, kt = (DSA_TQ_PROMPT if past is None else DSA_TQ_SAMPLE), DSA_KT
    (qh, lf, kk, vh, og, k_buf, v_buf, kb, ik_buf, kib, aqT, vT, iqT, iwT, sgh, sga) = _in_proj(
        x, lw, TM_IN, kt, layer, bufs)
    y_hg, s_new = _hgrn(qh, lf, kk, vh, og, lw["norm_w"], s0, batch, seq, HGRN_CHUNKS_PER_STEP)
    if past is None:
        l_tot = seq
        l_pad = -(-l_tot // kt) * kt
        assert l_pad == l_tot and seq % tq == 0
        o_att = _dsa(kib.reshape(batch, seq, IDX_DIM), iqT, iwT, kb.reshape(batch, seq, ATT_WIDTH), aqT, vT,
                     batch, seq, l_pad, 0, min(TOPK_MAX, l_tot // 4), tq, kt)
    else:
        k_past, v_past, ki_past = past
        p_len = k_past.shape[1]
        l_tot = p_len + seq
        n_q = -(-seq // tq) * tq
        l_pad = -(-(p_len + n_q) // kt) * kt
        padk = lambda a, new: jnp.concatenate(
            [a, new, jnp.zeros((batch, l_pad - l_tot, a.shape[2]), a.dtype)], axis=1)
        k_all = padk(k_past, kb.reshape(batch, seq, ATT_WIDTH))
        ki_all = padk(ki_past, kib.reshape(batch, seq, IDX_DIM))
        v_all = padk(v_past, v_buf[layer].astype(v_past.dtype).reshape(batch, seq, ATT_WIDTH))
        vT_all = jnp.transpose(v_all.reshape(batch * (l_pad // kt), kt, ATT_WIDTH), (0, 2, 1))
        padq = lambda a: jnp.pad(a.reshape(a.shape[0], batch, seq), ((0, 0), (0, 0), (0, n_q - seq))
                                 ).reshape(a.shape[0], batch * n_q)
        o_pad = _dsa(ki_all, padq(iqT), padq(iwT), k_all, padq(aqT), vT_all,
                     batch, n_q, l_pad, p_len, min(TOPK_MAX, l_tot // 4), tq, kt)
        o_att = o_pad.reshape(batch, n_q, ATT_WIDTH)[:, :seq].reshape(batch * seq, ATT_WIDTH)
    x1, comb = _out_proj(y_hg, o_att, sgh, sga, x, lw, TM_OUT)
    x2 = _moe(x1, comb, lw, TM_MOE)
    return x2, (k_buf, v_buf, ik_buf), s_new


def kernel(x_prompt, x_sample, cache_k, cache_v, cache_idx_k, state_hgrn, w_in, hg_lb_logits, hg_norm_w,
           w_br_hg, w_br_att, w_out, ln1_g, ln1_b, ln2_g, ln2_b, w_rg, b_rg, w_re, b_re, w_gate, w_up, w_down):
    bp, tp, d = x_prompt.shape
    bs, ts, _ = x_sample.shape
    p_len = cache_k.shape[2]
    lbs = _lower_bounds(hg_lb_logits)
    xp = x_prompt.reshape(bp * tp, d)
    xs = x_sample.reshape(bs * ts, d)
    zeros_state = jnp.zeros((bp, HG_HEADS, HG_DK, HG_DV), F32)
    bufs_p, bufs_s, st_p, st_s = None, None, [], []
    for l in range(DEPTH):
        lw = _layer_weights(l, lbs, w_in, hg_norm_w, w_br_hg, w_br_att, w_out, ln1_g, ln1_b, ln2_g, ln2_b,
                            w_rg, b_rg, w_re, b_re, w_gate, w_up, w_down)
        xp, bufs_p, sp = _mixer_and_ffn(xp, lw, bp, tp, zeros_state, None, l, bufs_p)
        past = (cache_k[l].reshape(bs, p_len, ATT_WIDTH).astype(MXU_DTYPE),
                cache_v[l].reshape(bs, p_len, ATT_WIDTH).astype(MXU_DTYPE),
                cache_idx_k[l].astype(MXU_DTYPE))
        xs, bufs_s, ss = _mixer_and_ffn(xs, lw, bs, ts, state_hgrn[l].astype(F32), past, l, bufs_s)
        st_p.append(sp)
        st_s.append(ss)

    def shaped(bufs, b, t):
        k, v, ik = bufs
        return (k.reshape(DEPTH, b, t, ATT_HEADS, ATT_DIM), v.reshape(DEPTH, b, t, ATT_HEADS, ATT_DIM),
                ik.reshape(DEPTH, b, t, IDX_DIM))

    kp, vp, ikp = shaped(bufs_p, bp, tp)
    ks, vs, iks = shaped(bufs_s, bs, ts)
    return (xp.reshape(bp, tp, d), xs.reshape(bs, ts, d), kp, vp, ikp, jnp.stack(st_p), ks, vs, iks,
            jnp.stack(st_s).astype(state_hgrn.dtype))
```

```python
import functools

import jax
import jax.numpy as jnp
from jax import lax
from jax.experimental import pallas as pl
from jax.experimental.pallas import tpu as pltpu

F32 = jnp.float32
I32 = jnp.int32
MXU_DTYPE = jnp.bfloat16

D_MODEL = 1024
DEPTH = 4
CHUNK = 64
HG_HEADS = 4
HG_DK = 128
HG_DV = 128
HG_WIDTH = HG_HEADS * HG_DK
ATT_HEADS = 8
ATT_DIM = 64
ATT_WIDTH = ATT_HEADS * ATT_DIM
IDX_HEADS = 4
IDX_DIM = 64
TOPK_MAX = 256
ATT_SCALE = ATT_DIM ** -0.5
IDX_SCALE = IDX_DIM ** -0.5
IDX_W_SCALE = IDX_HEADS ** -0.5
N_GROUPS = 4
EXPERTS_PER_GROUP = 4
N_EXPERTS = N_GROUPS * EXPERTS_PER_GROUP
D_EXPERT = 256
DN_ALPHA = (2 * DEPTH) ** 0.25
LN_EPS = 1e-5
RMS_EPS = 1e-6
IN_SIZES = (HG_WIDTH, HG_WIDTH, HG_HEADS * HG_DV, HG_HEADS * HG_DV,
            ATT_WIDTH, ATT_WIDTH, ATT_WIDTH, IDX_HEADS * IDX_DIM, IDX_DIM, IDX_HEADS,
            D_MODEL, D_MODEL)

LANES = 128
SUBLANES = 8
PACKED_ROWS = 16
SUBCHUNK = 16
VMEM_LIMIT = 56 * 1024 * 1024
INT_MIN = -2 ** 31
NEG_BIG = -1e30
ROUTER_LANES = LANES
EXPERT_LANE0 = N_GROUPS


def _params(*sem):
    return pltpu.CompilerParams(dimension_semantics=sem, vmem_limit_bytes=VMEM_LIMIT)


def _mm(a, b):
    return jnp.dot(a.astype(MXU_DTYPE), b.astype(MXU_DTYPE), preferred_element_type=F32)


def _mm_nt(a, b):
    return lax.dot_general(a.astype(MXU_DTYPE), b.astype(MXU_DTYPE), (((1,), (1,)), ((), ())),
                           preferred_element_type=F32)


def _sigmoid(x):
    return 1.0 / (1.0 + jnp.exp(-x))


def _silu(x):
    return x * _sigmoid(x)


def _layer_norm(r, g, b):
    mu = jnp.mean(r, axis=-1, keepdims=True)
    d = r - mu
    var = jnp.mean(d * d, axis=-1, keepdims=True)
    return d * lax.rsqrt(var + LN_EPS) * g + b


def _in_proj_kernel(x_ref, wh_ref, wa_ref, wi_ref, wg_ref, waqT_ref, wavT_ref, wiqT_ref, wiwT_ref,
                    lbp_ref, *refs):
    (qh_ref, lf_ref, kk_ref, vh_ref, og_ref, k32_ref, v32_ref, kb_ref, ik32_ref, kib_ref,
     aqT_ref, vT_ref, iqT_ref, iwT_ref, sgh_ref, sga_ref) = refs[-16:]
    xb = x_ref[...].astype(MXU_DTYPE)
    W = HG_WIDTH
    qh_ref[...] = _silu(_mm(xb, wh_ref[:, 0:W]))
    z = _mm(xb, wh_ref[:, W:2 * W])
    log_lb = lbp_ref[0:1, :]
    log_1mlb = lbp_ref[1:2, :]
    one_mlb = lbp_ref[2:3, :]
    log_sig = jnp.minimum(z, 0.0) - jnp.log(1.0 + jnp.exp(-jnp.abs(z)))
    b = log_1mlb + log_sig
    lf_ref[...] = jnp.maximum(log_lb, b) + jnp.log(1.0 + jnp.exp(-jnp.abs(log_lb - b)))
    kk_ref[...] = one_mlb * _sigmoid(-z)
    vh_ref[...] = _mm(xb, wh_ref[:, 2 * W:3 * W])
    og_ref[...] = _silu(_mm(xb, wh_ref[:, 3 * W:4 * W]))
    A = ATT_WIDTH
    k = _mm(xb, wa_ref[:, 0:A])
    v = _mm(xb, wa_ref[:, A:2 * A])
    kb_ref[...] = k.astype(kb_ref.dtype)
    n_tok = k.shape[0]
    for h in range(ATT_HEADS):
        k32_ref[pl.ds(h, n_tok, stride=ATT_HEADS), :] = k[:, h * ATT_DIM:(h + 1) * ATT_DIM]
        v32_ref[pl.ds(h, n_tok, stride=ATT_HEADS), :] = v[:, h * ATT_DIM:(h + 1) * ATT_DIM]
    aqT_ref[...] = (_mm_nt(waqT_ref[...], xb) * ATT_SCALE).astype(aqT_ref.dtype)
    kt = vT_ref.shape[-1]
    for t in range(vT_ref.shape[0]):
        vT_ref[t] = _mm_nt(wavT_ref[...], xb[t * kt:(t + 1) * kt, :]).astype(vT_ref.dtype)
    ik = _mm(xb, wi_ref[...])[:, 0:IDX_DIM]
    ik32_ref[...] = ik
    kib_ref[...] = ik.astype(kib_ref.dtype)
    iqT_ref[...] = _mm_nt(wiqT_ref[...], xb).astype(iqT_ref.dtype)
    iwT_ref[...] = _mm_nt(wiwT_ref[...], xb) * (IDX_SCALE * IDX_W_SCALE)
    D = D_MODEL
    sgh_ref[...] = _sigmoid(_mm(xb, wg_ref[:, 0:D]))
    sga_ref[...] = _sigmoid(_mm(xb, wg_ref[:, D:2 * D]))


def _in_proj(x, lw, tm, kt, layer, bufs):
    n = x.shape[0]
    tm = min(tm, n)
    assert tm % kt == 0 and n % tm == 0
    grid = (n // tm,)
    full = lambda a: pl.BlockSpec(a.shape, lambda i: (0,) * a.ndim)
    rows = lambda c: pl.BlockSpec((tm, c), lambda i: (i, 0))
    cols = lambda r: pl.BlockSpec((r, tm), lambda i: (0, i))
    lrows = lambda c: pl.BlockSpec((None, tm, c), lambda i: (layer, i, 0))
    weights = (lw["w_h"], lw["w_a"], lw["w_i"], lw["w_g"], lw["w_aqT"], lw["w_avT"], lw["w_iqT"], lw["w_iwT"],
               lw["lbp"])
    out_shape = (
        jax.ShapeDtypeStruct((n, HG_WIDTH), F32),
        jax.ShapeDtypeStruct((n, HG_WIDTH), F32),
        jax.ShapeDtypeStruct((n, HG_WIDTH), F32),
        jax.ShapeDtypeStruct((n, HG_WIDTH), F32),
        jax.ShapeDtypeStruct((n, HG_WIDTH), F32),
        jax.ShapeDtypeStruct((DEPTH, n * ATT_HEADS, ATT_DIM), F32),
        jax.ShapeDtypeStruct((DEPTH, n * ATT_HEADS, ATT_DIM), F32),
        jax.ShapeDtypeStruct((n, ATT_WIDTH), MXU_DTYPE),
        jax.ShapeDtypeStruct((DEPTH, n, IDX_DIM), F32),
        jax.ShapeDtypeStruct((n, IDX_DIM), MXU_DTYPE),
        jax.ShapeDtypeStruct((ATT_WIDTH, n), MXU_DTYPE),
        jax.ShapeDtypeStruct((n // kt, ATT_WIDTH, kt), MXU_DTYPE),
        jax.ShapeDtypeStruct((IDX_HEADS * IDX_DIM, n), MXU_DTYPE),
        jax.ShapeDtypeStruct((SUBLANES, n), F32),
        jax.ShapeDtypeStruct((n, D_MODEL), F32),
        jax.ShapeDtypeStruct((n, D_MODEL), F32),
    )
    hrows = pl.BlockSpec((None, tm * ATT_HEADS, ATT_DIM), lambda i: (layer, i, 0))
    out_specs = (rows(HG_WIDTH),) * 5 + (hrows, hrows, rows(ATT_WIDTH),
                                         lrows(IDX_DIM), rows(IDX_DIM)) + (
        cols(ATT_WIDTH), pl.BlockSpec((tm // kt, ATT_WIDTH, kt), lambda i: (i, 0, 0)),
        cols(IDX_HEADS * IDX_DIM), cols(SUBLANES), rows(D_MODEL), rows(D_MODEL))
    in_specs = [rows(D_MODEL)] + [full(w) for w in weights]
    aliases = {}
    if bufs is not None:
        first = len(in_specs)
        in_specs += [pl.BlockSpec(memory_space=pl.ANY)] * len(bufs)
        aliases = {first: 5, first + 1: 6, first + 2: 8}
    return pl.pallas_call(
        _in_proj_kernel,
        grid=grid,
        in_specs=in_specs,
        out_specs=out_specs,
        out_shape=out_shape,
        input_output_aliases=aliases,
        compiler_params=_params("parallel"),
        name="in_proj",
    )(x, *weights, *(bufs or ()))


def _hgrn_kernel(q_ref, f_ref, k_ref, v_ref, og_ref, nw_ref, s0_ref, y_ref, s_ref, st_scr, *, n_chunks):
    g = pl.program_id(1)

    @pl.when(g == 0)
    def _():
        for h in range(HG_HEADS):
            st_scr[h] = s0_ref[h].T

    C, SC, R8 = CHUNK, SUBCHUNK, SUBLANES
    row = lax.broadcasted_iota(I32, (C, C), 0)
    col = lax.broadcasted_iota(I32, (C, C), 1)
    tri = (row >= col).astype(F32)
    row_c = lax.broadcasted_iota(I32, (C, 1), 0)
    row_8 = lax.broadcasted_iota(I32, (R8, 1), 0)
    lane_c = lax.broadcasted_iota(I32, (1, C), 1)
    ones = jnp.ones((HG_DK, LANES), MXU_DTYPE)
    nw = nw_ref[...]

    heads = range(HG_HEADS)
    hcols = [slice(h * HG_DK, (h + 1) * HG_DK) for h in heads]

    def chunk(c, carry):
        sl = pl.ds(pl.multiple_of(c * C, C), C)
        q = [q_ref[sl, hc] for hc in hcols]
        k = [k_ref[sl, hc] for hc in hcols]
        v = [v_ref[sl, hc] for hc in hcols]
        bc = [jnp.dot(tri, f_ref[sl, hc], precision=lax.Precision.HIGHEST, preferred_element_type=F32)
              for hc in hcols]
        bt = [b[C - 1:C, :] for b in bc]
        o = [_mm_nt(q[h] * jnp.exp(bc[h]), st_scr[h]) for h in heads]
        upd = [_mm(v[h].T, k[h] * jnp.exp(bt[h] - bc[h])) for h in heads]
        for h in heads:
            st_scr[h] = st_scr[h] * jnp.exp(bt[h]) + upd[h]
        a_off = [[None] * HG_HEADS for _ in range(C // SC)]
        for i in range(1, C // SC):
            r0 = i * SC
            for h in heads:
                anchor = bc[h][r0:r0 + 1, :]
                qd = q[h][r0:r0 + SC, :] * jnp.exp(bc[h][r0:r0 + SC, :] - anchor)
                kd_i = jnp.where(row_c < r0, k[h] * jnp.exp(jnp.minimum(anchor - bc[h], 0.0)), 0.0)
                a_off[i][h] = _mm_nt(qd, kd_i)
        red = []
        for h in heads:
            prods = []
            for i in range(C // SC):
                r0 = i * SC
                q_i, k_i, bc_i = q[h][r0:r0 + SC, :], k[h][r0:r0 + SC, :], bc[h][r0:r0 + SC, :]
                for s in range(SC):
                    lo = (s // R8) * R8
                    k_s = k_i[s:s + 1, :]
                    b_s = bc_i[s:s + 1, :]
                    e = jnp.where(row_8 + lo >= s, jnp.exp(bc_i[lo:lo + R8, :] - b_s), 0.0)
                    prods.append(q_i[lo:lo + R8, :] * e * k_s)
                    for r in range(lo + R8, SC, R8):
                        prods.append(q_i[r:r + R8, :] * jnp.exp(bc_i[r:r + R8, :] - b_s) * k_s)
            red.append(_mm(jnp.concatenate(prods, axis=0), ones))
        for h in heads:
            blocks = []
            off = 0
            for i in range(C // SC):
                r0 = i * SC
                a_i = jnp.zeros((SC, C), F32) if i == 0 else a_off[i][h]
                groups = [a_i[r:r + R8, :] for r in range(0, SC, R8)]
                for s in range(SC):
                    for r in range((s // R8) * R8, SC, R8):
                        groups[r // R8] = jnp.where(lane_c == r0 + s, red[h][off:off + R8, 0:C], groups[r // R8])
                        off += R8
                blocks.extend(groups)
            o[h] = o[h] + _mm(jnp.concatenate(blocks, axis=0), v[h])
        for h in heads:
            on = o[h] * lax.rsqrt(jnp.mean(o[h] * o[h], axis=-1, keepdims=True) + RMS_EPS) * nw
            y_ref[sl, hcols[h]] = (on * og_ref[sl, hcols[h]]).astype(y_ref.dtype)
        return carry

    lax.fori_loop(0, n_chunks, chunk, 0)

    @pl.when(g == pl.num_programs(1) - 1)
    def _():
        for h in range(HG_HEADS):
            s_ref[h] = st_scr[h].T


def _hgrn(qh, lf, kk, vh, og, norm_w, s0, batch, seq, chunks_per_step):
    n_chunks = seq // CHUNK
    g_sz = min(chunks_per_step, n_chunks)
    steps = n_chunks // g_sz
    tb = g_sz * CHUNK
    r3 = lambda a: a.reshape(batch, seq, HG_WIDTH)
    tok = pl.BlockSpec((None, tb, HG_WIDTH), lambda b, g: (b, g, 0))
    st = pl.BlockSpec((None, HG_HEADS, HG_DK, HG_DV), lambda b, g: (b, 0, 0, 0))
    y, s = pl.pallas_call(
        functools.partial(_hgrn_kernel, n_chunks=g_sz),
        grid=(batch, steps),
        in_specs=[tok, tok, tok, tok, tok, pl.BlockSpec((1, HG_DV), lambda b, g: (0, 0)), st],
        out_specs=(tok, st),
        out_shape=(jax.ShapeDtypeStruct((batch, seq, HG_WIDTH), MXU_DTYPE),
                   jax.ShapeDtypeStruct((batch, HG_HEADS, HG_DK, HG_DV), F32)),
        scratch_shapes=[pltpu.VMEM((HG_HEADS, HG_DV, HG_DK), F32)],
        compiler_params=_params("parallel", "arbitrary"),
        name="hgrn",
    )(r3(qh), r3(lf), r3(kk), r3(vh), r3(og), norm_w.reshape(1, HG_DV), s0)
    return y.reshape(batch * seq, HG_WIDTH), s


def _dsa_kernel(kib_ref, iqT_ref, iwT_ref, kb_ref, aqT_ref, vT_ref, o_ref, keys_scr, acc_scr, qh_scr, p_scr,
                lg_scr, half_scr, p2_scr, lg2_scr,
                *, past, topk, tq, kt):
    i = pl.program_id(1)
    qpos0 = past + i * tq
    nk = (qpos0 + tq + kt - 1) // kt
    lane_q = lax.broadcasted_iota(I32, (1, tq), 1)
    qchunk = (qpos0 + lane_q) // CHUNK
    row_k = lax.broadcasted_iota(I32, (kt, 1), 0)
    tile = lambda j: pl.ds(pl.multiple_of(j * kt, kt), kt)
    fold = lambda a: a.reshape(kt // SUBLANES, SUBLANES, tq)

    def score_tile(j, masked):
        ki = kib_ref[tile(j), :]
        sc = jnp.zeros((kt, tq), F32)
        for h in range(IDX_HEADS):
            raw = jnp.dot(ki, iqT_ref[h * IDX_DIM:(h + 1) * IDX_DIM, :], preferred_element_type=F32)
            sc = sc + jnp.maximum(raw, 0.0) * iwT_ref[h:h + 1, :]
        bits = lax.bitcast_convert_type(sc, I32)
        key = jnp.where(bits < 0, INT_MIN - bits, bits)
        if masked:
            kchunk = (j * kt + row_k) // CHUNK
            key = jnp.where(kchunk <= qchunk, key, INT_MIN)
        keys_scr[tile(j), :] = key

    def score_full(j, carry):
        score_tile(j, False)
        return carry

    lax.fori_loop(0, nk - 1, score_full, 0)
    score_tile(nk - 1, True)

    def count(pred):
        def one(j, acc):
            m = pred(keys_scr[tile(j), :], j * kt + row_k).astype(I32)
            return acc + jnp.sum(fold(m), axis=0)

        def two(jj, acc):
            return one(2 * jj + 1, one(2 * jj, acc))

        acc = lax.fori_loop(0, nk // 2, two, jnp.zeros((SUBLANES, tq), I32))
        acc = lax.fori_loop(2 * (nk // 2), nk, one, acc)
        return jnp.sum(acc, axis=0, keepdims=True)

    I16 = jnp.int16
    HALF_BITS, HALF_MIN = 16, -2 ** 15
    fold16 = lambda a: a.reshape(kt // PACKED_ROWS, PACKED_ROWS, tq)

    def count16(cand):
        c16 = jnp.broadcast_to(cand, (PACKED_ROWS, tq)).astype(I16)

        def one(j, acc):
            m = (fold16(half_scr[tile(j), :]) >= c16).astype(I16)
            for r in range(kt // PACKED_ROWS):
                acc = acc + m[r]
            return acc

        def two(jj, acc):
            return one(2 * jj + 1, one(2 * jj, acc))

        acc = lax.fori_loop(0, nk // 2, two, jnp.zeros((PACKED_ROWS, tq), I16))
        acc = lax.fori_loop(2 * (nk // 2), nk, one, acc)
        return jnp.sum(acc.astype(I32), axis=0, keepdims=True)

    def select16(kth, n_init):
        def step(it, carry):
            lo, n_lo = carry
            cand = lo + jnp.left_shift(jnp.int32(1), HALF_BITS - 1 - it)
            cnt = count16(cand)
            take = cnt >= kth
            return jnp.where(take, cand, lo), jnp.where(take, cnt, n_lo)
        return lax.fori_loop(0, HALF_BITS, step, (jnp.full((1, tq), HALF_MIN, I32), n_init))

    def split_hi(j, carry):
        half_scr[tile(j), :] = (keys_scr[tile(j), :] >> HALF_BITS).astype(I16)
        return carry

    lax.fori_loop(0, nk, split_hi, 0)
    t_hi, n_ge_hi = select16(topk, jnp.zeros((1, tq), I32))

    def split_lo(j, acc):
        t = keys_scr[tile(j), :]
        hi = t >> HALF_BITS
        lo_s = (t & (2 ** HALF_BITS - 1)) + HALF_MIN
        half_scr[tile(j), :] = jnp.where(hi == t_hi, lo_s, HALF_MIN).astype(I16)
        return acc + jnp.sum(fold((hi > t_hi).astype(I32)), axis=0)

    n_above = jnp.sum(lax.fori_loop(0, nk, split_lo, jnp.zeros((SUBLANES, tq), I32)), axis=0, keepdims=True)
    t_lo, n_eq_hi = select16(topk - n_above, n_ge_hi - n_above)
    thr = t_hi * 2 ** HALF_BITS + (t_lo - HALF_MIN)
    n_ge = n_above + n_eq_hi
    has_k = thr > INT_MIN
    n_gt = count(lambda t, pos: t > thr)
    need = topk - n_gt
    surplus = jnp.logical_and(has_k, n_ge > topk)

    lmax_bits = max(1, (keys_scr.shape[0]).bit_length())

    def bis(it, jb):
        cand = jb + jnp.left_shift(jnp.int32(1), lmax_bits - 1 - it)
        cnt = count(lambda t, pos: jnp.logical_and(t == thr, pos < cand))
        return jnp.where(cnt <= need, cand, jb)

    any_surplus = jnp.max(surplus.astype(I32)) > 0
    jbound = lax.fori_loop(0, jnp.where(any_surplus, lmax_bits, 0), bis, jnp.zeros((1, tq), I32))

    def demote(j, carry):
        t = keys_scr[tile(j), :]
        drop = jnp.logical_and(surplus, jnp.logical_and(t == thr, j * kt + row_k >= jbound))
        keys_scr[tile(j), :] = jnp.where(drop, thr - 1, t)
        return carry

    lax.fori_loop(0, jnp.where(any_surplus, nk, 0), demote, 0)
    thr_sel = jnp.maximum(thr, INT_MIN + 1)

    half = lax.broadcasted_iota(I32, (LANES, 1), 0) // ATT_DIM
    for h in range(ATT_HEADS):
        pair = aqT_ref[(h // 2) * LANES:(h // 2 + 1) * LANES, :]
        qh_scr[h] = jnp.where(half == (h % 2), pair, jnp.zeros_like(pair))

    def logits(j, h):
        kh = kb_ref[tile(j), (h // 2) * LANES:(h // 2 + 1) * LANES]
        return jnp.dot(kh, qh_scr[h], preferred_element_type=F32)

    acc_scr[...] = jnp.zeros_like(acc_scr)
    heads = range(ATT_HEADS)
    ones_rows = jnp.ones((PACKED_ROWS, kt), MXU_DTYPE)

    def qk_and_numerators(j_next, lg_next, lg_cur, p_cur, m_new):
        tmax = []
        if lg_next is not None:
            jc = jnp.minimum(j_next, nk - 1)
            bias = jnp.where(jnp.logical_and(j_next < nk, keys_scr[tile(jc), :] >= thr_sel), 0.0, NEG_BIG)
        for h in heads:
            if lg_next is not None:
                x = logits(jc, h) + bias
                lg_next[h] = x
                tmax.append(jnp.max(jnp.max(fold(x), axis=0), axis=0, keepdims=True))
            if lg_cur is not None:
                p_cur[h] = jnp.exp(lg_cur[h] - m_new[h]).astype(p_cur.dtype)
        return tuple(tmax)

    def half_step(j, lg_cur, p_cur, lg_next, ms, ls, tmax):
        m_new = [jnp.maximum(ms[h], tmax[h]) for h in heads]
        alpha = [jnp.exp(ms[h] - m_new[h]) for h in heads]
        tmax_next = qk_and_numerators(j + 1, lg_next, lg_cur, p_cur, m_new)
        l_new = []
        for h in heads:
            rows = slice(h * ATT_DIM, (h + 1) * ATT_DIM)
            pv = jnp.dot(jnp.concatenate([vT_ref[j, rows, :], ones_rows], axis=0), p_cur[h],
                         preferred_element_type=F32)
            acc_scr[rows, :] = acc_scr[rows, :] * alpha[h] + pv[0:ATT_DIM, :]
            l_new.append(alpha[h] * ls[h] + pv[ATT_DIM:ATT_DIM + SUBLANES, :])
        return tuple(m_new), tuple(l_new), tmax_next

    def attend_pair(jj, carry):
        ms, ls, tmax = carry
        ms, ls, tmax = half_step(2 * jj, lg_scr, p_scr, lg2_scr, ms, ls, tmax)
        return half_step(2 * jj + 1, lg2_scr, p2_scr, lg_scr, ms, ls, tmax)

    def attend_last(j, carry):
        ms, ls, tmax = carry
        ms, ls, _ = half_step(nk - 1, lg_scr, p_scr, None, ms, ls, tmax)
        return ms, ls, tmax

    init = (tuple(jnp.full((1, tq), NEG_BIG, F32) for _ in heads),
            tuple(jnp.zeros((SUBLANES, tq), F32) for _ in heads),
            qk_and_numerators(0, lg_scr, None, None, None))
    carry = lax.fori_loop(0, nk // 2, attend_pair, init)
    _, ls, _ = lax.fori_loop(0, nk % 2, attend_last, carry)
    for h in heads:
        rows = slice(h * ATT_DIM, (h + 1) * ATT_DIM)
        acc_scr[rows, :] = acc_scr[rows, :] / ls[h][0:1, :]
    o_ref[...] = acc_scr[...].T.astype(o_ref.dtype)


def _dsa(kib, iqT, iwT, kb, aqT, vT, batch, n_q, l_pad, past, topk, tq, kt):
    assert kt % tq == 0 and past % tq == 0 and tq % CHUNK == 0 and n_q % tq == 0 and l_pad % kt == 0
    nq = n_q // tq
    qcol = lambda r: pl.BlockSpec((r, tq), lambda b, i: (0, b * nq + i))
    return pl.pallas_call(
        functools.partial(_dsa_kernel, past=past, topk=topk, tq=tq, kt=kt),
        grid=(batch, nq),
        in_specs=[pl.BlockSpec((None, l_pad, IDX_DIM), lambda b, i: (b, 0, 0)),
                  qcol(IDX_HEADS * IDX_DIM), qcol(SUBLANES),
                  pl.BlockSpec((None, l_pad, ATT_WIDTH), lambda b, i: (b, 0, 0)),
                  qcol(ATT_WIDTH),
                  pl.BlockSpec((l_pad // kt, ATT_WIDTH, kt), lambda b, i: (b, 0, 0))],
        out_specs=pl.BlockSpec((tq, ATT_WIDTH), lambda b, i: (b * nq + i, 0)),
        out_shape=jax.ShapeDtypeStruct((batch * n_q, ATT_WIDTH), MXU_DTYPE),
        scratch_shapes=[pltpu.VMEM((l_pad, tq), I32), pltpu.VMEM((ATT_WIDTH, tq), F32),
                        pltpu.VMEM((ATT_HEADS, LANES, tq), MXU_DTYPE),
                        pltpu.VMEM((ATT_HEADS, kt, tq), MXU_DTYPE),
                        pltpu.VMEM((ATT_HEADS, kt, tq), F32),
                        pltpu.VMEM((l_pad, tq), jnp.int16),
                        pltpu.VMEM((ATT_HEADS, kt, tq), MXU_DTYPE),
                        pltpu.VMEM((ATT_HEADS, kt, tq), F32)],
        compiler_params=_params("parallel", "arbitrary"),
        name="dsa",
    )(kib, iqT, iwT, kb, aqT, vT)


def _out_proj_kernel(yh_ref, oa_ref, sgh_ref, sga_ref, x_ref, wbh_ref, wba_ref, wo_ref, g_ref, b_ref,
                     wr_ref, br_ref, x1_ref, comb_ref):
    br_hg = jnp.dot(yh_ref[...], wbh_ref[...], preferred_element_type=F32)
    br_att = jnp.dot(oa_ref[...], wba_ref[...], preferred_element_type=F32)
    merged = sgh_ref[...] * br_hg + sga_ref[...] * br_att
    out = _mm(merged, wo_ref[...])
    x1 = _layer_norm(DN_ALPHA * x_ref[...] + out, g_ref[...], b_ref[...])
    x1_ref[...] = x1
    lg = _mm(x1, wr_ref[...]) + br_ref[...]
    lane = lax.broadcasted_iota(I32, lg.shape, 1).astype(F32)
    ninf = -jnp.inf
    gmask = lane < N_GROUPS
    gl = jnp.where(gmask, lg, ninf)
    gmax = jnp.max(gl, axis=1, keepdims=True)
    gsel = jnp.min(jnp.where(gl == gmax, lane, float(ROUTER_LANES)), axis=1, keepdims=True)
    g_w = 1.0 / jnp.sum(jnp.where(gmask, jnp.exp(gl - gmax), 0.0), axis=1, keepdims=True)
    e0 = EXPERT_LANE0 + EXPERTS_PER_GROUP * gsel
    emask = jnp.logical_and(lane >= e0, lane < e0 + EXPERTS_PER_GROUP)
    el = jnp.where(emask, lg, ninf)
    emax = jnp.max(el, axis=1, keepdims=True)
    ee = jnp.where(emask, jnp.exp(el - emax), 0.0)
    prob = ee / jnp.sum(ee, axis=1, keepdims=True)
    pm = jnp.where(emask, prob, -1.0)
    p1 = jnp.max(pm, axis=1, keepdims=True)
    i1 = jnp.min(jnp.where(pm == p1, lane, float(ROUTER_LANES)), axis=1, keepdims=True)
    pm2 = jnp.where(lane == i1, -1.0, pm)
    p2 = jnp.max(pm2, axis=1, keepdims=True)
    i2 = jnp.min(jnp.where(pm2 == p2, lane, float(ROUTER_LANES)), axis=1, keepdims=True)
    tot = p1 + p2
    comb_ref[...] = (jnp.where(lane == i1, g_w * (p1 / tot), 0.0)
                     + jnp.where(lane == i2, g_w * (p2 / tot), 0.0))


def _out_proj(yh, oa, sgh, sga, x, lw, tm):
    n = x.shape[0]
    tm = min(tm, n)
    full = lambda a: pl.BlockSpec(a.shape, lambda i: (0,) * a.ndim)
    rows = lambda c: pl.BlockSpec((tm, c), lambda i: (i, 0))
    weights = (lw["w_br_hg"], lw["w_br_att"], lw["w_out"], lw["ln1_g"], lw["ln1_b"], lw["w_r"], lw["b_r"])
    return pl.pallas_call(
        _out_proj_kernel,
        grid=(n // tm,),
        in_specs=[rows(HG_WIDTH), rows(ATT_WIDTH), rows(D_MODEL), rows(D_MODEL), rows(D_MODEL)]
        + [full(w) for w in weights],
        out_specs=(rows(D_MODEL), rows(ROUTER_LANES)),
        out_shape=(jax.ShapeDtypeStruct((n, D_MODEL), F32), jax.ShapeDtypeStruct((n, ROUTER_LANES), F32)),
        compiler_params=_params("parallel"),
        name="out_proj",
    )(yh, oa, sgh, sga, x, *weights)


def _moe_kernel(x_ref, comb_ref, wgu_ref, wd_ref, g_ref, b_ref, o_ref, xb_scr, hb_scr, acc_scr):
    grp = pl.program_id(1)

    @pl.when(grp == 0)
    def _():
        xb_scr[...] = x_ref[...].astype(xb_scr.dtype)
        acc_scr[...] = jnp.zeros_like(acc_scr)

    xb = xb_scr[...]
    comb = comb_ref[...]
    lane = lax.broadcasted_iota(I32, comb.shape, 1)
    F = D_EXPERT
    for e in range(EXPERTS_PER_GROUP):
        gu = jnp.dot(xb, wgu_ref[e], preferred_element_type=F32)
        h = _silu(gu[:, 0:F]) * gu[:, F:2 * F]
        c = jnp.sum(jnp.where(lane == EXPERT_LANE0 + grp * EXPERTS_PER_GROUP + e, comb, 0.0),
                    axis=1, keepdims=True)
        hb_scr[:, e * F:(e + 1) * F] = (h * c).astype(hb_scr.dtype)
    acc_scr[...] += jnp.dot(hb_scr[...], wd_ref[...], preferred_element_type=F32)

    @pl.when(grp == pl.num_programs(1) - 1)
    def _():
        o_ref[...] = _layer_norm(DN_ALPHA * x_ref[...] + acc_scr[...], g_ref[...], b_ref[...])


def _moe(x1, comb, lw, tm):
    n = x1.shape[0]
    tm = min(tm, n)
    E, F = EXPERTS_PER_GROUP, D_EXPERT
    return pl.pallas_call(
        _moe_kernel,
        grid=(n // tm, N_GROUPS),
        in_specs=[pl.BlockSpec((tm, D_MODEL), lambda i, g: (i, 0)),
                  pl.BlockSpec((tm, ROUTER_LANES), lambda i, g: (i, 0)),
                  pl.BlockSpec((None, E, D_MODEL, 2 * F), lambda i, g: (g, 0, 0, 0)),
                  pl.BlockSpec((None, E * F, D_MODEL), lambda i, g: (g, 0, 0)),
                  pl.BlockSpec((1, D_MODEL), lambda i, g: (0, 0)),
                  pl.BlockSpec((1, D_MODEL), lambda i, g: (0, 0))],
        out_specs=pl.BlockSpec((tm, D_MODEL), lambda i, g: (i, 0)),
        out_shape=jax.ShapeDtypeStruct((n, D_MODEL), F32),
        scratch_shapes=[pltpu.VMEM((tm, D_MODEL), MXU_DTYPE), pltpu.VMEM((tm, E * F), MXU_DTYPE),
                        pltpu.VMEM((tm, D_MODEL), F32)],
        compiler_params=_params("parallel", "arbitrary"),
        name="moe",
    )(x1, comb, lw["w_gu"], lw["w_d"], lw["ln2_g"], lw["ln2_b"])


def _layer_weights(l, lbs, w_in, hg_norm_w, w_br_hg, w_br_att, w_out, ln1_g, ln1_b, ln2_g, ln2_b,
                   w_rg, b_rg, w_re, b_re, w_gate, w_up, w_down):
    md = MXU_DTYPE
    offs = [0]
    for s in IN_SIZES:
        offs.append(offs[-1] + s)
    w = w_in[l]
    seg = lambda a, b: w[:, offs[a]:offs[b]]
    idx_cols = jnp.concatenate([seg(8, 9), jnp.zeros((D_MODEL, LANES - IDX_DIM), F32)], axis=1)
    iw_rows = jnp.concatenate([seg(9, 10).T, jnp.zeros((SUBLANES - IDX_HEADS, D_MODEL), F32)], axis=0)
    lb = lbs[l]
    lbp = jnp.concatenate([jnp.log(lb)[None], jnp.log1p(-lb)[None], (1.0 - lb)[None],
                           jnp.zeros((SUBLANES - 3, HG_WIDTH), F32)], axis=0)
    w_r = jnp.concatenate([w_rg[l], w_re[l], jnp.zeros((D_MODEL, ROUTER_LANES - N_GROUPS - N_EXPERTS), F32)], axis=1)
    b_r = jnp.concatenate([b_rg[l], b_re[l], jnp.zeros((ROUTER_LANES - N_GROUPS - N_EXPERTS,), F32)])[None]
    return {
        "w_h": seg(0, 4).astype(md), "w_a": seg(5, 7).astype(md), "w_i": idx_cols.astype(md),
        "w_g": seg(10, 12).astype(md),
        "w_aqT": seg(4, 5).T.astype(md), "w_avT": seg(6, 7).T.astype(md), "w_iqT": seg(7, 8).T.astype(md),
        "w_iwT": iw_rows.astype(md), "lbp": lbp,
        "norm_w": hg_norm_w[l],
        "w_br_hg": w_br_hg[l].astype(md), "w_br_att": w_br_att[l].astype(md), "w_out": w_out[l].astype(md),
        "ln1_g": ln1_g[l][None], "ln1_b": ln1_b[l][None], "ln2_g": ln2_g[l][None], "ln2_b": ln2_b[l][None],
        "w_r": w_r.astype(md), "b_r": b_r,
        "w_gu": jnp.concatenate([w_gate[l], w_up[l]], axis=-1).astype(md),
        "w_d": w_down[l].reshape(N_GROUPS, EXPERTS_PER_GROUP * D_EXPERT, D_MODEL).astype(md),
    }


def _lower_bounds(lb_logits):
    p = jax.nn.softmax(lb_logits.astype(F32), axis=0)
    return jnp.concatenate([jnp.zeros_like(p[:1]), jnp.cumsum(p[1:], axis=0)], axis=0)


TM_IN = 256
TM_OUT = 512
TM_MOE = 1024
HGRN_CHUNKS_PER_STEP = 16
DSA_TQ_PROMPT = 256
DSA_TQ_SAMPLE = 128
DSA_KT = 256


def _mixer_and_ffn(x, lw, batch, seq, s0, past, layer, bufs):
    tq, kt = (DSA_TQ_PROMPT if past is None else DSA_TQ_SAMPLE), DSA_KT
    (qh, lf, kk, vh, og, k_buf, v_buf, kb, ik_buf, kib, aqT, vT, iqT, iwT, sgh, sga) = _in_proj(
        x, lw, TM_IN, kt, layer, bufs)
    y_hg, s_new = _hgrn(qh, lf, kk, vh, og, lw["norm_w"], s0, batch, seq, HGRN_CHUNKS_PER_STEP)
    if past is None:
        l_tot = seq
        l_pad = -(-l_tot // kt) * kt
        assert l_pad == l_tot and seq % tq == 0
        o_att = _dsa(kib.reshape(batch, seq, IDX_DIM), iqT, iwT, kb.reshape(batch, seq, ATT_WIDTH), aqT, vT,
                     batch, seq, l_pad, 0, min(TOPK_MAX, l_tot // 4), tq, kt)
    else:
        k_past, v_past, ki_past = past
        p_len = k_past.shape[1]
        l_tot = p_len + seq
        n_q = -(-seq // tq) * tq
        l_pad = -(-(p_len + n_q) // kt) * kt
        padk = lambda a, new: jnp.concatenate(
            [a, new, jnp.zeros((batch, l_pad - l_tot, a.shape[2]), a.dtype)], axis=1)
        k_all = padk(k_past, kb.reshape(batch, seq, ATT_WIDTH))
        ki_all = padk(ki_past, kib.reshape(batch, seq, IDX_DIM))
        v_all = padk(v_past, v_buf[layer].astype(v_past.dtype).reshape(batch, seq, ATT_WIDTH))
        vT_all = jnp.transpose(v_all.reshape(batch * (l_pad // kt), kt, ATT_WIDTH), (0, 2, 1))
        padq = lambda a: jnp.pad(a.reshape(a.shape[0], batch, seq), ((0, 0), (0, 0), (0, n_q - seq))
                                 ).reshape(a.shape[0], batch * n_q)
        o_pad = _dsa(ki_all, padq(iqT), padq(iwT), k_all, padq(aqT), vT_all,
                     batch, n_q, l_pad, p_len, min(TOPK_MAX, l_tot // 4), tq, kt)
        o_att = o_pad.reshape(batch, n_q, ATT_WIDTH)[:, :seq].reshape(batch * seq, ATT_WIDTH)
    x1, comb = _out_proj(y_hg, o_att, sgh, sga, x, lw, TM_OUT)
    x2 = _moe(x1, comb, lw, TM_MOE)
    return x2, (k_buf, v_buf, ik_buf), s_new


def kernel(x_prompt, x_sample, cache_k, cache_v, cache_idx_k, state_hgrn, w_in, hg_lb_logits, hg_norm_w,
           w_br_hg, w_br_att, w_out, ln1_g, ln1_b, ln2_g, ln2_b, w_rg, b_rg, w_re, b_re, w_gate, w_up, w_down):
    bp, tp, d = x_prompt.shape
    bs, ts, _ = x_sample.shape
    p_len = cache_k.shape[2]
    lbs = _lower_bounds(hg_lb_logits)
    xp = x_prompt.reshape(bp * tp, d)
    xs = x_sample.reshape(bs * ts, d)
    zeros_state = jnp.zeros((bp, HG_HEADS, HG_DK, HG_DV), F32)
    bufs_p, bufs_s, st_p, st_s = None, None, [], []
    for l in range(DEPTH):
        lw = _layer_weights(l, lbs, w_in, hg_norm_w, w_br_hg, w_br_att, w_out, ln1_g, ln1_b, ln2_g, ln2_b,
                            w_rg, b_rg, w_re, b_re, w_gate, w_up, w_down)
        xp, bufs_p, sp = _mixer_and_ffn(xp, lw, bp, tp, zeros_state, None, l, bufs_p)
        past = (cache_k[l].reshape(bs, p_len, ATT_WIDTH).astype(MXU_DTYPE),
                cache_v[l].reshape(bs, p_len, ATT_WIDTH).astype(MXU_DTYPE),
                cache_idx_k[l].astype(MXU_DTYPE))
        xs, bufs_s, ss = _mixer_and_ffn(xs, lw, bs, ts, state_hgrn[l].astype(F32), past, l, bufs_s)
        st_p.append(sp)
        st_s.append(ss)

    def shaped(bufs, b, t):
        k, v, ik = bufs
        return (k.reshape(DEPTH, b, t, ATT_HEADS, ATT_DIM), v.reshape(DEPTH, b, t, ATT_HEADS, ATT_DIM),
                ik.reshape(DEPTH, b, t, IDX_DIM))

    kp, vp, ikp = shaped(bufs_p, bp, tp)
    ks, vs, iks = shaped(bufs_s, bs, ts)
    return (xp.reshape(bp, tp, d), xs.reshape(bs, ts, d), kp, vp, ikp, jnp.stack(st_p), ks, vs, iks,
            jnp.stack(st_s).astype(state_hgrn.dtype))
```

```python
import functools

import jax
import jax.numpy as jnp
from jax import lax
from jax.experimental import pallas as pl
from jax.experimental.pallas import tpu as pltpu

F32 = jnp.float32
I32 = jnp.int32
MXU_DTYPE = jnp.bfloat16

D_MODEL = 1024
DEPTH = 4
CHUNK = 64
HG_HEADS = 4
HG_DK = 128
HG_DV = 128
HG_WIDTH = HG_HEADS * HG_DK
ATT_HEADS = 8
ATT_DIM = 64
ATT_WIDTH = ATT_HEADS * ATT_DIM
IDX_HEADS = 4
IDX_DIM = 64
TOPK_MAX = 256
ATT_SCALE = ATT_DIM ** -0.5
IDX_SCALE = IDX_DIM ** -0.5
IDX_W_SCALE = IDX_HEADS ** -0.5
N_GROUPS = 4
EXPERTS_PER_GROUP = 4
N_EXPERTS = N_GROUPS * EXPERTS_PER_GROUP
D_EXPERT = 256
DN_ALPHA = (2 * DEPTH) ** 0.25
LN_EPS = 1e-5
RMS_EPS = 1e-6
IN_SIZES = (HG_WIDTH, HG_WIDTH, HG_HEADS * HG_DV, HG_HEADS * HG_DV,
            ATT_WIDTH, ATT_WIDTH, ATT_WIDTH, IDX_HEADS * IDX_DIM, IDX_DIM, IDX_HEADS,
            D_MODEL, D_MODEL)

LANES = 128
SUBLANES = 8
PACKED_ROWS = 16
I16 = jnp.int16
HALF_BITS, HALF_MIN = 16, -2 ** 15
SUBCHUNK = 16
SPAN_MAX = 60.0
VMEM_LIMIT = 56 * 1024 * 1024
INT_MIN = -2 ** 31
NEG_BIG = -1e30
ROUTER_LANES = LANES
EXPERT_LANE0 = N_GROUPS


def _params(*sem):
    return pltpu.CompilerParams(dimension_semantics=sem, vmem_limit_bytes=VMEM_LIMIT)


def _mm(a, b):
    return jnp.dot(a.astype(MXU_DTYPE), b.astype(MXU_DTYPE), preferred_element_type=F32)


def _mm_nt(a, b):
    return lax.dot_general(a.astype(MXU_DTYPE), b.astype(MXU_DTYPE), (((1,), (1,)), ((), ())),
                           preferred_element_type=F32)


def _sigmoid(x):
    return 1.0 / (1.0 + jnp.exp(-x))


def _silu(x):
    return x * _sigmoid(x)


def _layer_norm(r, g, b):
    mu = jnp.mean(r, axis=-1, keepdims=True)
    d = r - mu
    var = jnp.mean(d * d, axis=-1, keepdims=True)
    return d * lax.rsqrt(var + LN_EPS) * g + b


def _in_proj_kernel(x_ref, wh_ref, wa_ref, wi_ref, wg_ref, waqT_ref, wiqT_ref, wiwT_ref,
                    lbp_ref, *refs):
    (qh_ref, lf_ref, kk_ref, vh_ref, og_ref, k32_ref, v32_ref, kb_ref, ik32_ref, kib_ref,
     aqT_ref, vT_ref, iqT_ref, iwT_ref, sgh_ref, sga_ref) = refs[-16:]
    xb = x_ref[...].astype(MXU_DTYPE)
    W = HG_WIDTH
    qh_ref[...] = _silu(_mm(xb, wh_ref[:, 0:W]))
    z = _mm(xb, wh_ref[:, W:2 * W])
    log_lb = lbp_ref[0:1, :]
    log_1mlb = lbp_ref[1:2, :]
    one_mlb = lbp_ref[2:3, :]
    log_sig = jnp.minimum(z, 0.0) - jnp.log(1.0 + jnp.exp(-jnp.abs(z)))
    b = log_1mlb + log_sig
    lf_ref[...] = jnp.maximum(log_lb, b) + jnp.log(1.0 + jnp.exp(-jnp.abs(log_lb - b)))
    kk_ref[...] = one_mlb * _sigmoid(-z)
    vh_ref[...] = _mm(xb, wh_ref[:, 2 * W:3 * W])
    og_ref[...] = _silu(_mm(xb, wh_ref[:, 3 * W:4 * W]))
    A = ATT_WIDTH
    k = _mm(xb, wa_ref[:, 0:A])
    v = _mm(xb, wa_ref[:, A:2 * A])
    kb_ref[...] = k.astype(kb_ref.dtype)
    n_tok = k.shape[0]
    for h in range(ATT_HEADS):
        k32_ref[pl.ds(h, n_tok, stride=ATT_HEADS), :] = k[:, h * ATT_DIM:(h + 1) * ATT_DIM]
        v32_ref[pl.ds(h, n_tok, stride=ATT_HEADS), :] = v[:, h * ATT_DIM:(h + 1) * ATT_DIM]
    aqT_ref[...] = (_mm_nt(waqT_ref[...], xb) * ATT_SCALE).astype(aqT_ref.dtype)
    kt = vT_ref.shape[-1]
    for t in range(vT_ref.shape[0]):
        vT_ref[t] = v[t * kt:(t + 1) * kt, :].T.astype(vT_ref.dtype)
    ik = _mm(xb, wi_ref[...])[:, 0:IDX_DIM]
    ik32_ref[...] = ik
    kib_ref[...] = ik.astype(kib_ref.dtype)
    iqT_ref[...] = _mm_nt(wiqT_ref[...], xb).astype(iqT_ref.dtype)
    iwT_ref[...] = _mm_nt(wiwT_ref[...], xb) * (IDX_SCALE * IDX_W_SCALE)
    D = D_MODEL
    sgh_ref[...] = _sigmoid(_mm(xb, wg_ref[:, 0:D]))
    sga_ref[...] = _sigmoid(_mm(xb, wg_ref[:, D:2 * D]))


def _in_proj(x, lw, tm, kt, layer, bufs):
    n = x.shape[0]
    tm = min(tm, n)
    assert tm % kt == 0 and n % tm == 0
    grid = (n // tm,)
    full = lambda a: pl.BlockSpec(a.shape, lambda i: (0,) * a.ndim)
    rows = lambda c: pl.BlockSpec((tm, c), lambda i: (i, 0))
    cols = lambda r: pl.BlockSpec((r, tm), lambda i: (0, i))
    lrows = lambda c: pl.BlockSpec((None, tm, c), lambda i: (layer, i, 0))
    weights = (lw["w_h"], lw["w_a"], lw["w_i"], lw["w_g"], lw["w_aqT"], lw["w_iqT"], lw["w_iwT"],
               lw["lbp"])
    out_shape = (
        jax.ShapeDtypeStruct((n, HG_WIDTH), F32),
        jax.ShapeDtypeStruct((n, HG_WIDTH), F32),
        jax.ShapeDtypeStruct((n, HG_WIDTH), F32),
        jax.ShapeDtypeStruct((n, HG_WIDTH), F32),
        jax.ShapeDtypeStruct((n, HG_WIDTH), F32),
        jax.ShapeDtypeStruct((DEPTH, n * ATT_HEADS, ATT_DIM), F32),
        jax.ShapeDtypeStruct((DEPTH, n * ATT_HEADS, ATT_DIM), F32),
        jax.ShapeDtypeStruct((n, ATT_WIDTH), MXU_DTYPE),
        jax.ShapeDtypeStruct((DEPTH, n, IDX_DIM), F32),
        jax.ShapeDtypeStruct((n, IDX_DIM), MXU_DTYPE),
        jax.ShapeDtypeStruct((ATT_WIDTH, n), MXU_DTYPE),
        jax.ShapeDtypeStruct((n // kt, ATT_WIDTH, kt), MXU_DTYPE),
        jax.ShapeDtypeStruct((IDX_HEADS * IDX_DIM, n), MXU_DTYPE),
        jax.ShapeDtypeStruct((SUBLANES, n), F32),
        jax.ShapeDtypeStruct((n, D_MODEL), F32),
        jax.ShapeDtypeStruct((n, D_MODEL), F32),
    )
    hrows = pl.BlockSpec((None, tm * ATT_HEADS, ATT_DIM), lambda i: (layer, i, 0))
    out_specs = (rows(HG_WIDTH),) * 5 + (hrows, hrows, rows(ATT_WIDTH),
                                         lrows(IDX_DIM), rows(IDX_DIM)) + (
        cols(ATT_WIDTH), pl.BlockSpec((tm // kt, ATT_WIDTH, kt), lambda i: (i, 0, 0)),
        cols(IDX_HEADS * IDX_DIM), cols(SUBLANES), rows(D_MODEL), rows(D_MODEL))
    in_specs = [rows(D_MODEL)] + [full(w) for w in weights]
    aliases = {}
    if bufs is not None:
        first = len(in_specs)
        in_specs += [pl.BlockSpec(memory_space=pl.ANY)] * len(bufs)
        aliases = {first: 5, first + 1: 6, first + 2: 8}
    return pl.pallas_call(
        _in_proj_kernel,
        grid=grid,
        in_specs=in_specs,
        out_specs=out_specs,
        out_shape=out_shape,
        input_output_aliases=aliases,
        compiler_params=_params("parallel"),
        name="in_proj",
    )(x, *weights, *(bufs or ()))


def _hgrn_kernel(q_ref, f_ref, k_ref, v_ref, og_ref, nw_ref, s0_ref, y_ref, s_ref, st_scr, *, n_chunks,
                 unroll):
    g = pl.program_id(1)

    @pl.when(g == 0)
    def _():
        for h in range(HG_HEADS):
            st_scr[h] = s0_ref[h].T

    C, SC, R8 = CHUNK, SUBCHUNK, SUBLANES
    row = lax.broadcasted_iota(I32, (C, C), 0)
    col = lax.broadcasted_iota(I32, (C, C), 1)
    tri = (row >= col).astype(F32)
    row_c = lax.broadcasted_iota(I32, (C, 1), 0)
    row_8 = lax.broadcasted_iota(I32, (R8, 1), 0)
    row_sc = lax.broadcasted_iota(I32, (SC, 1), 0)
    lane_c = lax.broadcasted_iota(I32, (1, C), 1)
    ones = jnp.ones((HG_DK, LANES), MXU_DTYPE)
    nw = nw_ref[...]

    heads = range(HG_HEADS)
    hcols = [slice(h * HG_DK, (h + 1) * HG_DK) for h in heads]

    n_units = unroll * HG_HEADS

    def chunk(c, carry):
        units = range(n_units)
        sls = [pl.ds(pl.multiple_of((c * unroll + n // HG_HEADS) * C, C), C) for n in units]
        hcs = [hcols[n % HG_HEADS] for n in units]
        q = [q_ref[sls[n], hcs[n]] for n in units]
        k = [k_ref[sls[n], hcs[n]] for n in units]
        v = [v_ref[sls[n], hcs[n]] for n in units]
        bc = [jnp.dot(tri, f_ref[sls[n], hcs[n]], precision=lax.Precision.HIGHEST, preferred_element_type=F32)
              for n in units]
        bt = [b[C - 1:C, :] for b in bc]
        upd = [_mm(v[n].T, k[n] * jnp.exp(bt[n] - bc[n])) for n in units]
        o = [None] * n_units
        for n in units:
            h = n % HG_HEADS
            o[n] = _mm_nt(q[n] * jnp.exp(bc[n]), st_scr[h])
            st_scr[h] = st_scr[h] * jnp.exp(bt[n]) + upd[n]
        heads = units
        def anchored(h, i, own_rows):
            r0 = i * SC
            anchor = bc[h][r0:r0 + 1, :]
            qd = q[h][r0:r0 + SC, :] * jnp.exp(bc[h][r0:r0 + SC, :] - anchor)
            last = r0 + SC if own_rows else r0
            cap = SPAN_MAX if own_rows else 0.0
            kd_i = jnp.where(row_c < last, k[h] * jnp.exp(jnp.minimum(anchor - bc[h], cap)), 0.0)
            return _mm_nt(qd, kd_i)

        def intra_factored():
            out = []
            for h in heads:
                blocks = [jnp.where(lane_c <= i * SC + row_sc, anchored(h, i, True), 0.0) for i in range(C // SC)]
                out.append(jnp.concatenate(blocks, axis=0))
            return tuple(out)

        def intra_exact():
            a_off = [[None] * n_units for _ in range(C // SC)]
            for i in range(1, C // SC):
                for h in heads:
                    a_off[i][h] = anchored(h, i, False)
            red = []
            for h in heads:
                prods = []
                for i in range(C // SC):
                    r0 = i * SC
                    q_i, k_i, bc_i = q[h][r0:r0 + SC, :], k[h][r0:r0 + SC, :], bc[h][r0:r0 + SC, :]
                    for s in range(SC):
                        lo = (s // R8) * R8
                        k_s = k_i[s:s + 1, :]
                        b_s = bc_i[s:s + 1, :]
                        e = jnp.where(row_8 + lo >= s, jnp.exp(bc_i[lo:lo + R8, :] - b_s), 0.0)
                        prods.append(q_i[lo:lo + R8, :] * e * k_s)
                        for r in range(lo + R8, SC, R8):
                            prods.append(q_i[r:r + R8, :] * jnp.exp(bc_i[r:r + R8, :] - b_s) * k_s)
                red.append(_mm(jnp.concatenate(prods, axis=0), ones))
            out = []
            for h in heads:
                blocks = []
                off = 0
                for i in range(C // SC):
                    r0 = i * SC
                    a_i = jnp.zeros((SC, C), F32) if i == 0 else a_off[i][h]
                    groups = [a_i[r:r + R8, :] for r in range(0, SC, R8)]
                    for s in range(SC):
                        for r in range((s // R8) * R8, SC, R8):
                            groups[r // R8] = jnp.where(lane_c == r0 + s, red[h][off:off + R8, 0:C],
                                                        groups[r // R8])
                            off += R8
                    blocks.extend(groups)
                out.append(jnp.concatenate(blocks, axis=0))
            return tuple(out)

        span = None
        for h in heads:
            for i in range(C // SC):
                d = bc[h][i * SC:i * SC + 1, :] - bc[h][(i + 1) * SC - 1:(i + 1) * SC, :]
                span = d if span is None else jnp.maximum(span, d)
        a = lax.cond(jnp.max(span) <= SPAN_MAX, intra_factored, intra_exact)
        for h in heads:
            o[h] = o[h] + _mm(a[h], v[h])
        for h in heads:
            on = o[h] * lax.rsqrt(jnp.mean(o[h] * o[h], axis=-1, keepdims=True) + RMS_EPS) * nw
            y_ref[sls[h], hcs[h]] = (on * og_ref[sls[h], hcs[h]]).astype(y_ref.dtype)
        return carry

    lax.fori_loop(0, n_chunks // unroll, chunk, 0)

    @pl.when(g == pl.num_programs(1) - 1)
    def _():
        for h in range(HG_HEADS):
            s_ref[h] = st_scr[h].T


def _hgrn(qh, lf, kk, vh, og, norm_w, s0, batch, seq, chunks_per_step):
    n_chunks = seq // CHUNK
    g_sz = min(chunks_per_step, n_chunks)
    steps = n_chunks // g_sz
    tb = g_sz * CHUNK
    r3 = lambda a: a.reshape(batch, seq, HG_WIDTH)
    tok = pl.BlockSpec((None, tb, HG_WIDTH), lambda b, g: (b, g, 0))
    st = pl.BlockSpec((None, HG_HEADS, HG_DK, HG_DV), lambda b, g: (b, 0, 0, 0))
    y, s = pl.pallas_call(
        functools.partial(_hgrn_kernel, n_chunks=g_sz, unroll=HGRN_CHUNKS_PER_TRIP if g_sz % HGRN_CHUNKS_PER_TRIP == 0 else 1),
        grid=(batch, steps),
        in_specs=[tok, tok, tok, tok, tok, pl.BlockSpec((1, HG_DV), lambda b, g: (0, 0)), st],
        out_specs=(tok, st),
        out_shape=(jax.ShapeDtypeStruct((batch, seq, HG_WIDTH), MXU_DTYPE),
                   jax.ShapeDtypeStruct((batch, HG_HEADS, HG_DK, HG_DV), F32)),
        scratch_shapes=[pltpu.VMEM((HG_HEADS, HG_DV, HG_DK), F32)],
        compiler_params=_params("parallel", "arbitrary"),
        name="hgrn",
    )(r3(qh), r3(lf), r3(kk), r3(vh), r3(og), norm_w.reshape(1, HG_DV), s0)
    return y.reshape(batch * seq, HG_WIDTH), s


def _dsa_kernel(kib_ref, iqT_ref, iwT_ref, kb_ref, aqT_ref, vT_ref, o_ref, keys_scr, acc_scr, qh_scr, p_scr,
                lg_scr, half_scr, p2_scr, lg2_scr,
                *, past, topk, tq, kt):
    i = pl.program_id(1)
    qpos0 = past + i * tq
    nk = (qpos0 + tq + kt - 1) // kt
    lane_q = lax.broadcasted_iota(I32, (1, tq), 1)
    qchunk = (qpos0 + lane_q) // CHUNK
    row_k = lax.broadcasted_iota(I32, (kt, 1), 0)
    tile = lambda j: pl.ds(pl.multiple_of(j * kt, kt), kt)
    fold = lambda a: a.reshape(kt // SUBLANES, SUBLANES, tq)

    def score_tile(j, masked):
        ki = kib_ref[tile(j), :]
        sc = jnp.zeros((kt, tq), F32)
        for h in range(IDX_HEADS):
            raw = jnp.dot(ki, iqT_ref[h * IDX_DIM:(h + 1) * IDX_DIM, :], preferred_element_type=F32)
            sc = sc + jnp.maximum(raw, 0.0) * iwT_ref[h:h + 1, :]
        bits = lax.bitcast_convert_type(sc, I32)
        key = jnp.where(bits < 0, INT_MIN - bits, bits)
        if masked:
            kchunk = (j * kt + row_k) // CHUNK
            key = jnp.where(kchunk <= qchunk, key, INT_MIN)
        keys_scr[tile(j), :] = key
        half_scr[tile(j), :] = (key >> HALF_BITS).astype(jnp.int16)

    def score_full(j, carry):
        score_tile(j, False)
        return carry

    lax.fori_loop(0, nk - 1, score_full, 0)
    score_tile(nk - 1, True)

    def count(pred):
        def one(j, acc):
            m = pred(keys_scr[tile(j), :], j * kt + row_k).astype(I32)
            return acc + jnp.sum(fold(m), axis=0)

        def two(jj, acc):
            return one(2 * jj + 1, one(2 * jj, acc))

        acc = lax.fori_loop(0, nk // 2, two, jnp.zeros((SUBLANES, tq), I32))
        acc = lax.fori_loop(2 * (nk // 2), nk, one, acc)
        return jnp.sum(acc, axis=0, keepdims=True)

    fold16 = lambda a: a.reshape(kt // PACKED_ROWS, PACKED_ROWS, tq)

    def count16(cand):
        c16 = jnp.broadcast_to(cand, (PACKED_ROWS, tq)).astype(I16)

        def one(j, acc):
            m = (fold16(half_scr[tile(j), :]) >= c16).astype(I16)
            for r in range(kt // PACKED_ROWS):
                acc = acc + m[r]
            return acc

        def two(jj, acc):
            return one(2 * jj + 1, one(2 * jj, acc))

        acc = lax.fori_loop(0, nk // 2, two, jnp.zeros((PACKED_ROWS, tq), I16))
        acc = lax.fori_loop(2 * (nk // 2), nk, one, acc)
        return jnp.sum(acc.astype(I32), axis=0, keepdims=True)

    def select16(kth, n_init):
        def step(it, carry):
            lo, n_lo = carry
            cand = lo + jnp.left_shift(jnp.int32(1), HALF_BITS - 1 - it)
            cnt = count16(cand)
            take = cnt >= kth
            return jnp.where(take, cand, lo), jnp.where(take, cnt, n_lo)
        return lax.fori_loop(0, HALF_BITS, step, (jnp.full((1, tq), HALF_MIN, I32), n_init))

    t_hi, n_ge_hi = select16(topk, jnp.zeros((1, tq), I32))

    def split_lo(j, acc):
        t = keys_scr[tile(j), :]
        hi = t >> HALF_BITS
        lo_s = (t & (2 ** HALF_BITS - 1)) + HALF_MIN
        half_scr[tile(j), :] = jnp.where(hi == t_hi, lo_s, HALF_MIN).astype(I16)
        return acc + jnp.sum(fold((hi > t_hi).astype(I32)), axis=0)

    n_above = jnp.sum(lax.fori_loop(0, nk, split_lo, jnp.zeros((SUBLANES, tq), I32)), axis=0, keepdims=True)
    t_lo, n_eq_hi = select16(topk - n_above, n_ge_hi - n_above)
    thr = t_hi * 2 ** HALF_BITS + (t_lo - HALF_MIN)
    n_ge = n_above + n_eq_hi
    has_k = thr > INT_MIN
    n_gt = n_above + jnp.where(t_lo < -HALF_MIN - 1, count16(jnp.minimum(t_lo + 1, -HALF_MIN - 1)), 0)
    need = topk - n_gt
    surplus = jnp.logical_and(has_k, n_ge > topk)

    lmax_bits = max(1, (keys_scr.shape[0]).bit_length())

    def bis(it, jb):
        cand = jb + jnp.left_shift(jnp.int32(1), lmax_bits - 1 - it)
        cnt = count(lambda t, pos: jnp.logical_and(t == thr, pos < cand))
        return jnp.where(cnt <= need, cand, jb)

    any_surplus = jnp.max(surplus.astype(I32)) > 0
    jbound = lax.fori_loop(0, jnp.where(any_surplus, lmax_bits, 0), bis, jnp.zeros((1, tq), I32))

    def demote(j, carry):
        t = keys_scr[tile(j), :]
        drop = jnp.logical_and(surplus, jnp.logical_and(t == thr, j * kt + row_k >= jbound))
        keys_scr[tile(j), :] = jnp.where(drop, thr - 1, t)
        return carry

    lax.fori_loop(0, jnp.where(any_surplus, nk, 0), demote, 0)
    thr_sel = jnp.maximum(thr, INT_MIN + 1)

    half = lax.broadcasted_iota(I32, (LANES, 1), 0) // ATT_DIM
    for h in range(ATT_HEADS):
        pair = aqT_ref[(h // 2) * LANES:(h // 2 + 1) * LANES, :]
        qh_scr[h] = jnp.where(half == (h % 2), pair, jnp.zeros_like(pair))

    def logits(j, h):
        kh = kb_ref[tile(j), (h // 2) * LANES:(h // 2 + 1) * LANES]
        return jnp.dot(kh, qh_scr[h], preferred_element_type=F32)

    acc_scr[...] = jnp.zeros_like(acc_scr)
    heads = range(ATT_HEADS)
    ones_rows = jnp.ones((PACKED_ROWS, kt), MXU_DTYPE)

    def qk_and_numerators(j_next, lg_next, lg_cur, p_cur, m_new):
        tmax = []
        if lg_next is not None:
            jc = jnp.minimum(j_next, nk - 1)
            bias = jnp.where(jnp.logical_and(j_next < nk, keys_scr[tile(jc), :] >= thr_sel), 0.0, NEG_BIG)
        for h in heads:
            if lg_next is not None:
                x = logits(jc, h) + bias
                lg_next[h] = x
                tmax.append(jnp.max(jnp.max(fold(x), axis=0), axis=0, keepdims=True))
            if lg_cur is not None:
                p_cur[h] = jnp.exp(lg_cur[h] - m_new[h]).astype(p_cur.dtype)
        return tuple(tmax)

    def half_step(j, lg_cur, p_cur, lg_next, ms, ls, tmax):
        m_new = [jnp.maximum(ms[h], tmax[h]) for h in heads]
        alpha = [jnp.exp(ms[h] - m_new[h]) for h in heads]
        tmax_next = qk_and_numerators(j + 1, lg_next, lg_cur, p_cur, m_new)
        l_new = []
        for h in heads:
            rows = slice(h * ATT_DIM, (h + 1) * ATT_DIM)
            pv = jnp.dot(jnp.concatenate([vT_ref[j, rows, :], ones_rows], axis=0), p_cur[h],
                         preferred_element_type=F32)
            acc_scr[rows, :] = acc_scr[rows, :] * alpha[h] + pv[0:ATT_DIM, :]
            l_new.append(alpha[h] * ls[h] + pv[ATT_DIM:ATT_DIM + SUBLANES, :])
        return tuple(m_new), tuple(l_new), tmax_next

    def attend_pair(jj, carry):
        ms, ls, tmax = carry
        ms, ls, tmax = half_step(2 * jj, lg_scr, p_scr, lg2_scr, ms, ls, tmax)
        return half_step(2 * jj + 1, lg2_scr, p2_scr, lg_scr, ms, ls, tmax)

    def attend_last(j, carry):
        ms, ls, tmax = carry
        ms, ls, _ = half_step(nk - 1, lg_scr, p_scr, None, ms, ls, tmax)
        return ms, ls, tmax

    init = (tuple(jnp.full((1, tq), NEG_BIG, F32) for _ in heads),
            tuple(jnp.zeros((SUBLANES, tq), F32) for _ in heads),
            qk_and_numerators(0, lg_scr, None, None, None))
    carry = lax.fori_loop(0, nk // 2, attend_pair, init)
    _, ls, _ = lax.fori_loop(0, nk % 2, attend_last, carry)
    for h in heads:
        rows = slice(h * ATT_DIM, (h + 1) * ATT_DIM)
        acc_scr[rows, :] = acc_scr[rows, :] / ls[h][0:1, :]
    o_ref[...] = acc_scr[...].T.astype(o_ref.dtype)


def _dsa(kib, iqT, iwT, kb, aqT, vT, batch, n_q, l_pad, past, topk, tq, kt):
    assert kt % tq == 0 and past % tq == 0 and tq % CHUNK == 0 and n_q % tq == 0 and l_pad % kt == 0
    nq = n_q // tq
    qcol = lambda r: pl.BlockSpec((r, tq), lambda b, i: (0, b * nq + i))
    return pl.pallas_call(
        functools.partial(_dsa_kernel, past=past, topk=topk, tq=tq, kt=kt),
        grid=(batch, nq),
        in_specs=[pl.BlockSpec((None, l_pad, IDX_DIM), lambda b, i: (b, 0, 0)),
                  qcol(IDX_HEADS * IDX_DIM), qcol(SUBLANES),
                  pl.BlockSpec((None, l_pad, ATT_WIDTH), lambda b, i: (b, 0, 0)),
                  qcol(ATT_WIDTH),
                  pl.BlockSpec((l_pad // kt, ATT_WIDTH, kt), lambda b, i: (b, 0, 0))],
        out_specs=pl.BlockSpec((tq, ATT_WIDTH), lambda b, i: (b * nq + i, 0)),
        out_shape=jax.ShapeDtypeStruct((batch * n_q, ATT_WIDTH), MXU_DTYPE),
        scratch_shapes=[pltpu.VMEM((l_pad, tq), I32), pltpu.VMEM((ATT_WIDTH, tq), F32),
                        pltpu.VMEM((ATT_HEADS, LANES, tq), MXU_DTYPE),
                        pltpu.VMEM((ATT_HEADS, kt, tq), MXU_DTYPE),
                        pltpu.VMEM((ATT_HEADS, kt, tq), F32),
                        pltpu.VMEM((l_pad, tq), jnp.int16),
                        pltpu.VMEM((ATT_HEADS, kt, tq), MXU_DTYPE),
                        pltpu.VMEM((ATT_HEADS, kt, tq), F32)],
        compiler_params=_params("parallel", "arbitrary"),
        name="dsa",
    )(kib, iqT, iwT, kb, aqT, vT)


def _out_proj_kernel(yh_ref, oa_ref, sgh_ref, sga_ref, x_ref, wbh_ref, wba_ref, wo_ref, g_ref, b_ref,
                     wr_ref, br_ref, x1_ref, comb_ref):
    br_hg = jnp.dot(yh_ref[...], wbh_ref[...], preferred_element_type=F32)
    br_att = jnp.dot(oa_ref[...], wba_ref[...], preferred_element_type=F32)
    merged = sgh_ref[...] * br_hg + sga_ref[...] * br_att
    out = _mm(merged, wo_ref[...])
    x1 = _layer_norm(DN_ALPHA * x_ref[...] + out, g_ref[...], b_ref[...])
    x1_ref[...] = x1
    lg = _mm(x1, wr_ref[...]) + br_ref[...]
    lane = lax.broadcasted_iota(I32, lg.shape, 1).astype(F32)
    ninf = -jnp.inf
    gmask = lane < N_GROUPS
    gl = jnp.where(gmask, lg, ninf)
    gmax = jnp.max(gl, axis=1, keepdims=True)
    gsel = jnp.min(jnp.where(gl == gmax, lane, float(ROUTER_LANES)), axis=1, keepdims=True)
    g_w = 1.0 / jnp.sum(jnp.where(gmask, jnp.exp(gl - gmax), 0.0), axis=1, keepdims=True)
    e0 = EXPERT_LANE0 + EXPERTS_PER_GROUP * gsel
    emask = jnp.logical_and(lane >= e0, lane < e0 + EXPERTS_PER_GROUP)
    el = jnp.where(emask, lg, ninf)
    emax = jnp.max(el, axis=1, keepdims=True)
    ee = jnp.where(emask, jnp.exp(el - emax), 0.0)
    prob = ee / jnp.sum(ee, axis=1, keepdims=True)
    pm = jnp.where(emask, prob, -1.0)
    p1 = jnp.max(pm, axis=1, keepdims=True)
    i1 = jnp.min(jnp.where(pm == p1, lane, float(ROUTER_LANES)), axis=1, keepdims=True)
    pm2 = jnp.where(lane == i1, -1.0, pm)
    p2 = jnp.max(pm2, axis=1, keepdims=True)
    i2 = jnp.min(jnp.where(pm2 == p2, lane, float(ROUTER_LANES)), axis=1, keepdims=True)
    tot = p1 + p2
    comb_ref[...] = (jnp.where(lane == i1, g_w * (p1 / tot), 0.0)
                     + jnp.where(lane == i2, g_w * (p2 / tot), 0.0))


def _out_proj(yh, oa, sgh, sga, x, lw, tm):
    n = x.shape[0]
    tm = min(tm, n)
    full = lambda a: pl.BlockSpec(a.shape, lambda i: (0,) * a.ndim)
    rows = lambda c: pl.BlockSpec((tm, c), lambda i: (i, 0))
    weights = (lw["w_br_hg"], lw["w_br_att"], lw["w_out"], lw["ln1_g"], lw["ln1_b"], lw["w_r"], lw["b_r"])
    return pl.pallas_call(
        _out_proj_kernel,
        grid=(n // tm,),
        in_specs=[rows(HG_WIDTH), rows(ATT_WIDTH), rows(D_MODEL), rows(D_MODEL), rows(D_MODEL)]
        + [full(w) for w in weights],
        out_specs=(rows(D_MODEL), rows(ROUTER_LANES)),
        out_shape=(jax.ShapeDtypeStruct((n, D_MODEL), F32), jax.ShapeDtypeStruct((n, ROUTER_LANES), F32)),
        compiler_params=_params("parallel"),
        name="out_proj",
    )(yh, oa, sgh, sga, x, *weights)


def _moe_kernel(x_ref, comb_ref, wgu_ref, wd_ref, g_ref, b_ref, o_ref, xb_scr, hb_scr, acc_scr):
    grp = pl.program_id(1)

    @pl.when(grp == 0)
    def _():
        xb_scr[...] = x_ref[...].astype(xb_scr.dtype)
        acc_scr[...] = jnp.zeros_like(acc_scr)

    xb = xb_scr[...]
    comb = comb_ref[...]
    lane = lax.broadcasted_iota(I32, comb.shape, 1)
    F = D_EXPERT
    for e in range(EXPERTS_PER_GROUP):
        gu = jnp.dot(xb, wgu_ref[e], preferred_element_type=F32)
        h = _silu(gu[:, 0:F]) * gu[:, F:2 * F]
        c = jnp.sum(jnp.where(lane == EXPERT_LANE0 + grp * EXPERTS_PER_GROUP + e, comb, 0.0),
                    axis=1, keepdims=True)
        hb_scr[:, e * F:(e + 1) * F] = (h * c).astype(hb_scr.dtype)
    acc_scr[...] += jnp.dot(hb_scr[...], wd_ref[...], preferred_element_type=F32)

    @pl.when(grp == pl.num_programs(1) - 1)
    def _():
        o_ref[...] = _layer_norm(DN_ALPHA * x_ref[...] + acc_scr[...], g_ref[...], b_ref[...])


def _moe(x1, comb, lw, tm):
    n = x1.shape[0]
    tm = min(tm, n)
    E, F = EXPERTS_PER_GROUP, D_EXPERT
    return pl.pallas_call(
        _moe_kernel,
        grid=(n // tm, N_GROUPS),
        in_specs=[pl.BlockSpec((tm, D_MODEL), lambda i, g: (i, 0)),
                  pl.BlockSpec((tm, ROUTER_LANES), lambda i, g: (i, 0)),
                  pl.BlockSpec((None, E, D_MODEL, 2 * F), lambda i, g: (g, 0, 0, 0)),
                  pl.BlockSpec((None, E * F, D_MODEL), lambda i, g: (g, 0, 0)),
                  pl.BlockSpec((1, D_MODEL), lambda i, g: (0, 0)),
                  pl.BlockSpec((1, D_MODEL), lambda i, g: (0, 0))],
        out_specs=pl.BlockSpec((tm, D_MODEL), lambda i, g: (i, 0)),
        out_shape=jax.ShapeDtypeStruct((n, D_MODEL), F32),
        scratch_shapes=[pltpu.VMEM((tm, D_MODEL), MXU_DTYPE), pltpu.VMEM((tm, E * F), MXU_DTYPE),
                        pltpu.VMEM((tm, D_MODEL), F32)],
        compiler_params=_params("parallel", "arbitrary"),
        name="moe",
    )(x1, comb, lw["w_gu"], lw["w_d"], lw["ln2_g"], lw["ln2_b"])


def _layer_weights(l, lbs, w_in, hg_norm_w, w_br_hg, w_br_att, w_out, ln1_g, ln1_b, ln2_g, ln2_b,
                   w_rg, b_rg, w_re, b_re, w_gate, w_up, w_down):
    md = MXU_DTYPE
    offs = [0]
    for s in IN_SIZES:
        offs.append(offs[-1] + s)
    w = w_in[l]
    seg = lambda a, b: w[:, offs[a]:offs[b]]
    idx_cols = jnp.concatenate([seg(8, 9), jnp.zeros((D_MODEL, LANES - IDX_DIM), F32)], axis=1)
    iw_rows = jnp.concatenate([seg(9, 10).T, jnp.zeros((SUBLANES - IDX_HEADS, D_MODEL), F32)], axis=0)
    lb = lbs[l]
    lbp = jnp.concatenate([jnp.log(lb)[None], jnp.log1p(-lb)[None], (1.0 - lb)[None],
                           jnp.zeros((SUBLANES - 3, HG_WIDTH), F32)], axis=0)
    w_r = jnp.concatenate([w_rg[l], w_re[l], jnp.zeros((D_MODEL, ROUTER_LANES - N_GROUPS - N_EXPERTS), F32)], axis=1)
    b_r = jnp.concatenate([b_rg[l], b_re[l], jnp.zeros((ROUTER_LANES - N_GROUPS - N_EXPERTS,), F32)])[None]
    return {
        "w_h": seg(0, 4).astype(md), "w_a": seg(5, 7).astype(md), "w_i": idx_cols.astype(md),
        "w_g": seg(10, 12).astype(md),
        "w_aqT": seg(4, 5).T.astype(md), "w_iqT": seg(7, 8).T.astype(md),
        "w_iwT": iw_rows.astype(md), "lbp": lbp,
        "norm_w": hg_norm_w[l],
        "w_br_hg": w_br_hg[l].astype(md), "w_br_att": w_br_att[l].astype(md), "w_out": w_out[l].astype(md),
        "ln1_g": ln1_g[l][None], "ln1_b": ln1_b[l][None], "ln2_g": ln2_g[l][None], "ln2_b": ln2_b[l][None],
        "w_r": w_r.astype(md), "b_r": b_r,
        "w_gu": jnp.concatenate([w_gate[l], w_up[l]], axis=-1).astype(md),
        "w_d": w_down[l].reshape(N_GROUPS, EXPERTS_PER_GROUP * D_EXPERT, D_MODEL).astype(md),
    }


def _lower_bounds(lb_logits):
    p = jax.nn.softmax(lb_logits.astype(F32), axis=0)
    return jnp.concatenate([jnp.zeros_like(p[:1]), jnp.cumsum(p[1:], axis=0)], axis=0)


TM_IN = 256
TM_OUT = 512
TM_MOE = 1024
HGRN_CHUNKS_PER_STEP = 16
HGRN_CHUNKS_PER_TRIP = 4
DSA_TQ_PROMPT = 256
DSA_TQ_SAMPLE = 128
DSA_KT = 256


def _mixer_and_ffn(x, lw, batch, seq, s0, past, layer, bufs):
    tq, kt = (DSA_TQ_PROMPT if past is None else DSA_TQ_SAMPLE), DSA_KT
    (qh, lf, kk, vh, og, k_buf, v_buf, kb, ik_buf, kib, aqT, vT, iqT, iwT, sgh, sga) = _in_proj(
        x, lw, TM_IN, kt, layer, bufs)
    y_hg, s_new = _hgrn(qh, lf, kk, vh, og, lw["norm_w"], s0, batch, seq, HGRN_CHUNKS_PER_STEP)
    if past is None:
        l_tot = seq
        l_pad = -(-l_tot // kt) * kt
        assert l_pad == l_tot and seq % tq == 0
        o_att = _dsa(kib.reshape(batch, seq, IDX_DIM), iqT, iwT, kb.reshape(batch, seq, ATT_WIDTH), aqT, vT,
                     batch, seq, l_pad, 0, min(TOPK_MAX, l_tot // 4), tq, kt)
    else:
        k_past, v_past, ki_past = past
        p_len = k_past.shape[1]
        l_tot = p_len + seq
        n_q = -(-seq // tq) * tq
        l_pad = -(-(p_len + n_q) // kt) * kt
        padk = lambda a, new: jnp.concatenate(
            [a, new, jnp.zeros((batch, l_pad - l_tot, a.shape[2]), a.dtype)], axis=1)
        k_all = padk(k_past, kb.reshape(batch, seq, ATT_WIDTH))
        ki_all = padk(ki_past, kib.reshape(batch, seq, IDX_DIM))
        v_all = padk(v_past, v_buf[layer].astype(v_past.dtype).reshape(batch, seq, ATT_WIDTH))
        vT_all = jnp.transpose(v_all.reshape(batch * (l_pad // kt), kt, ATT_WIDTH), (0, 2, 1))
        padq = lambda a: jnp.pad(a.reshape(a.shape[0], batch, seq), ((0, 0), (0, 0), (0, n_q - seq))
                                 ).reshape(a.shape[0], batch * n_q)
        o_pad = _dsa(ki_all, padq(iqT), padq(iwT), k_all, padq(aqT), vT_all,
                     batch, n_q, l_pad, p_len, min(TOPK_MAX, l_tot // 4), tq, kt)
        o_att = o_pad.reshape(batch, n_q, ATT_WIDTH)[:, :seq].reshape(batch * seq, ATT_WIDTH)
    x1, comb = _out_proj(y_hg, o_att, sgh, sga, x, lw, TM_OUT)
    x2 = _moe(x1, comb, lw, TM_MOE)
    return x2, (k_buf, v_buf, ik_buf), s_new


def kernel(x_prompt, x_sample, cache_k, cache_v, cache_idx_k, state_hgrn, w_in, hg_lb_logits, hg_norm_w,
           w_br_hg, w_br_att, w_out, ln1_g, ln1_b, ln2_g, ln2_b, w_rg, b_rg, w_re, b_re, w_gate, w_up, w_down):
    bp, tp, d = x_prompt.shape
    bs, ts, _ = x_sample.shape
    p_len = cache_k.shape[2]
    lbs = _lower_bounds(hg_lb_logits)
    xp = x_prompt.reshape(bp * tp, d)
    xs = x_sample.reshape(bs * ts, d)
    zeros_state = jnp.zeros((bp, HG_HEADS, HG_DK, HG_DV), F32)
    bufs_p, bufs_s, st_p, st_s = None, None, [], []
    for l in range(DEPTH):
        lw = _layer_weights(l, lbs, w_in, hg_norm_w, w_br_hg, w_br_att, w_out, ln1_g, ln1_b, ln2_g, ln2_b,
                            w_rg, b_rg, w_re, b_re, w_gate, w_up, w_down)
        xp, bufs_p, sp = _mixer_and_ffn(xp, lw, bp, tp, zeros_state, None, l, bufs_p)
        past = (cache_k[l].reshape(bs, p_len, ATT_WIDTH).astype(MXU_DTYPE),
                cache_v[l].reshape(bs, p_len, ATT_WIDTH).astype(MXU_DTYPE),
                cache_idx_k[l].astype(MXU_DTYPE))
        xs, bufs_s, ss = _mixer_and_ffn(xs, lw, bs, ts, state_hgrn[l].astype(F32), past, l, bufs_s)
        st_p.append(sp)
        st_s.append(ss)

    def shaped(bufs, b, t):
        k, v, ik = bufs
        return (k.reshape(DEPTH, b, t, ATT_HEADS, ATT_DIM), v.reshape(DEPTH, b, t, ATT_HEADS, ATT_DIM),
                ik.reshape(DEPTH, b, t, IDX_DIM))

    kp, vp, ikp = shaped(bufs_p, bp, tp)
    ks, vs, iks = shaped(bufs_s, bs, ts)
    return (xp.reshape(bp, tp, d), xs.reshape(bs, ts, d), kp, vp, ikp, jnp.stack(st_p), ks, vs, iks,
            jnp.stack(st_s).astype(state_hgrn.dtype))
```

```python
import functools

import jax
import jax.numpy as jnp
from jax import lax
from jax.experimental import pallas as pl
from jax.experimental.pallas import tpu as pltpu

F32 = jnp.float32
I32 = jnp.int32
MXU_DTYPE = jnp.bfloat16

D_MODEL = 1024
DEPTH = 4
CHUNK = 64
HG_HEADS = 4
HG_DK = 128
HG_DV = 128
HG_WIDTH = HG_HEADS * HG_DK
ATT_HEADS = 8
ATT_DIM = 64
ATT_WIDTH = ATT_HEADS * ATT_DIM
IDX_HEADS = 4
IDX_DIM = 64
TOPK_MAX = 256
ATT_SCALE = ATT_DIM ** -0.5
IDX_SCALE = IDX_DIM ** -0.5
IDX_W_SCALE = IDX_HEADS ** -0.5
N_GROUPS = 4
EXPERTS_PER_GROUP = 4
N_EXPERTS = N_GROUPS * EXPERTS_PER_GROUP
D_EXPERT = 256
DN_ALPHA = (2 * DEPTH) ** 0.25
LN_EPS = 1e-5
RMS_EPS = 1e-6
IN_SIZES = (HG_WIDTH, HG_WIDTH, HG_HEADS * HG_DV, HG_HEADS * HG_DV,
            ATT_WIDTH, ATT_WIDTH, ATT_WIDTH, IDX_HEADS * IDX_DIM, IDX_DIM, IDX_HEADS,
            D_MODEL, D_MODEL)

LANES = 128
SUBLANES = 8
PACKED_ROWS = 16
I16 = jnp.int16
HALF_BITS, HALF_MIN = 16, -2 ** 15
SUBCHUNK = 16
SPAN_MAX = 60.0
VMEM_LIMIT = 56 * 1024 * 1024
INT_MIN = -2 ** 31
NEG_BIG = -1e30
ROUTER_LANES = LANES
EXPERT_LANE0 = N_GROUPS


def _params(*sem):
    return pltpu.CompilerParams(dimension_semantics=sem, vmem_limit_bytes=VMEM_LIMIT)


def _mm(a, b):
    return jnp.dot(a.astype(MXU_DTYPE), b.astype(MXU_DTYPE), preferred_element_type=F32)


def _mm_nt(a, b):
    return lax.dot_general(a.astype(MXU_DTYPE), b.astype(MXU_DTYPE), (((1,), (1,)), ((), ())),
                           preferred_element_type=F32)


def _sigmoid(x):
    return 1.0 / (1.0 + jnp.exp(-x))


def _silu(x):
    return x * _sigmoid(x)


def _layer_norm(r, g, b):
    mu = jnp.mean(r, axis=-1, keepdims=True)
    d = r - mu
    var = jnp.mean(d * d, axis=-1, keepdims=True)
    return d * lax.rsqrt(var + LN_EPS) * g + b


def _in_proj_kernel(x_ref, wh_ref, wa_ref, wi_ref, wg_ref, waqT_ref, wiqT_ref, wiwT_ref,
                    lbp_ref, *refs):
    (qh_ref, lf_ref, kk_ref, vh_ref, og_ref, k32_ref, v32_ref, kb_ref, ik32_ref, kib_ref,
     aqT_ref, vT_ref, iqT_ref, iwT_ref, sgh_ref, sga_ref) = refs[-16:]
    xb = x_ref[...].astype(MXU_DTYPE)
    W = HG_WIDTH
    qh_ref[...] = _silu(_mm(xb, wh_ref[:, 0:W]))
    z = _mm(xb, wh_ref[:, W:2 * W])
    log_lb = lbp_ref[0:1, :]
    log_1mlb = lbp_ref[1:2, :]
    one_mlb = lbp_ref[2:3, :]
    log_sig = jnp.minimum(z, 0.0) - jnp.log(1.0 + jnp.exp(-jnp.abs(z)))
    b = log_1mlb + log_sig
    lf_ref[...] = jnp.maximum(log_lb, b) + jnp.log(1.0 + jnp.exp(-jnp.abs(log_lb - b)))
    kk_ref[...] = one_mlb * _sigmoid(-z)
    vh_ref[...] = _mm(xb, wh_ref[:, 2 * W:3 * W])
    og_ref[...] = _silu(_mm(xb, wh_ref[:, 3 * W:4 * W]))
    A = ATT_WIDTH
    k = _mm(xb, wa_ref[:, 0:A])
    v = _mm(xb, wa_ref[:, A:2 * A])
    kb_ref[...] = k.astype(kb_ref.dtype)
    n_tok = k.shape[0]
    for h in range(ATT_HEADS):
        k32_ref[pl.ds(h, n_tok, stride=ATT_HEADS), :] = k[:, h * ATT_DIM:(h + 1) * ATT_DIM]
        v32_ref[pl.ds(h, n_tok, stride=ATT_HEADS), :] = v[:, h * ATT_DIM:(h + 1) * ATT_DIM]
    aqT_ref[...] = (_mm_nt(waqT_ref[...], xb) * ATT_SCALE).astype(aqT_ref.dtype)
    kt = vT_ref.shape[-1]
    for t in range(vT_ref.shape[0]):
        vT_ref[t] = v[t * kt:(t + 1) * kt, :].T.astype(vT_ref.dtype)
    ik = _mm(xb, wi_ref[...])[:, 0:IDX_DIM]
    ik32_ref[...] = ik
    kib_ref[...] = ik.astype(kib_ref.dtype)
    iqT_ref[...] = _mm_nt(wiqT_ref[...], xb).astype(iqT_ref.dtype)
    iwT_ref[...] = _mm_nt(wiwT_ref[...], xb) * (IDX_SCALE * IDX_W_SCALE)
    D = D_MODEL
    sgh_ref[...] = _sigmoid(_mm(xb, wg_ref[:, 0:D]))
    sga_ref[...] = _sigmoid(_mm(xb, wg_ref[:, D:2 * D]))


def _in_proj(x, lw, tm, kt, layer, bufs):
    n = x.shape[0]
    tm = min(tm, n)
    assert tm % kt == 0 and n % tm == 0
    grid = (n // tm,)
    full = lambda a: pl.BlockSpec(a.shape, lambda i: (0,) * a.ndim)
    rows = lambda c: pl.BlockSpec((tm, c), lambda i: (i, 0))
    cols = lambda r: pl.BlockSpec((r, tm), lambda i: (0, i))
    lrows = lambda c: pl.BlockSpec((None, tm, c), lambda i: (layer, i, 0))
    weights = (lw["w_h"], lw["w_a"], lw["w_i"], lw["w_g"], lw["w_aqT"], lw["w_iqT"], lw["w_iwT"],
               lw["lbp"])
    out_shape = (
        jax.ShapeDtypeStruct((n, HG_WIDTH), F32),
        jax.ShapeDtypeStruct((n, HG_WIDTH), F32),
        jax.ShapeDtypeStruct((n, HG_WIDTH), F32),
        jax.ShapeDtypeStruct((n, HG_WIDTH), F32),
        jax.ShapeDtypeStruct((n, HG_WIDTH), F32),
        jax.ShapeDtypeStruct((DEPTH, n * ATT_HEADS, ATT_DIM), F32),
        jax.ShapeDtypeStruct((DEPTH, n * ATT_HEADS, ATT_DIM), F32),
        jax.ShapeDtypeStruct((n, ATT_WIDTH), MXU_DTYPE),
        jax.ShapeDtypeStruct((DEPTH, n, IDX_DIM), F32),
        jax.ShapeDtypeStruct((n, IDX_DIM), MXU_DTYPE),
        jax.ShapeDtypeStruct((ATT_WIDTH, n), MXU_DTYPE),
        jax.ShapeDtypeStruct((n // kt, ATT_WIDTH, kt), MXU_DTYPE),
        jax.ShapeDtypeStruct((IDX_HEADS * IDX_DIM, n), MXU_DTYPE),
        jax.ShapeDtypeStruct((SUBLANES, n), F32),
        jax.ShapeDtypeStruct((n, D_MODEL), F32),
        jax.ShapeDtypeStruct((n, D_MODEL), F32),
    )
    hrows = pl.BlockSpec((None, tm * ATT_HEADS, ATT_DIM), lambda i: (layer, i, 0))
    out_specs = (rows(HG_WIDTH),) * 5 + (hrows, hrows, rows(ATT_WIDTH),
                                         lrows(IDX_DIM), rows(IDX_DIM)) + (
        cols(ATT_WIDTH), pl.BlockSpec((tm // kt, ATT_WIDTH, kt), lambda i: (i, 0, 0)),
        cols(IDX_HEADS * IDX_DIM), cols(SUBLANES), rows(D_MODEL), rows(D_MODEL))
    in_specs = [rows(D_MODEL)] + [full(w) for w in weights]
    aliases = {}
    if bufs is not None:
        first = len(in_specs)
        in_specs += [pl.BlockSpec(memory_space=pl.ANY)] * len(bufs)
        aliases = {first: 5, first + 1: 6, first + 2: 8}
    return pl.pallas_call(
        _in_proj_kernel,
        grid=grid,
        in_specs=in_specs,
        out_specs=out_specs,
        out_shape=out_shape,
        input_output_aliases=aliases,
        compiler_params=_params("parallel"),
        name="in_proj",
    )(x, *weights, *(bufs or ()))


def _hgrn_kernel(q_ref, f_ref, k_ref, v_ref, og_ref, nw_ref, s0_ref, y_ref, s_ref, st_scr, *, n_chunks,
                 unroll):
    g = pl.program_id(1)

    @pl.when(g == 0)
    def _():
        for h in range(HG_HEADS):
            st_scr[h] = s0_ref[h].T

    C, SC, R8 = CHUNK, SUBCHUNK, SUBLANES
    row = lax.broadcasted_iota(I32, (C, C), 0)
    col = lax.broadcasted_iota(I32, (C, C), 1)
    tri = (row >= col).astype(F32)
    row_c = lax.broadcasted_iota(I32, (C, 1), 0)
    row_8 = lax.broadcasted_iota(I32, (R8, 1), 0)
    row_sc = lax.broadcasted_iota(I32, (SC, 1), 0)
    lane_c = lax.broadcasted_iota(I32, (1, C), 1)
    ones = jnp.ones((HG_DK, LANES), MXU_DTYPE)
    nw = nw_ref[...]

    heads = range(HG_HEADS)
    hcols = [slice(h * HG_DK, (h + 1) * HG_DK) for h in heads]

    n_units = unroll * HG_HEADS

    def chunk(c, carry):
        units = range(n_units)
        sls = [pl.ds(pl.multiple_of((c * unroll + n // HG_HEADS) * C, C), C) for n in units]
        hcs = [hcols[n % HG_HEADS] for n in units]
        q = [q_ref[sls[n], hcs[n]] for n in units]
        k = [k_ref[sls[n], hcs[n]] for n in units]
        v = [v_ref[sls[n], hcs[n]] for n in units]
        bc = [jnp.dot(tri, f_ref[sls[n], hcs[n]], precision=lax.Precision.HIGHEST, preferred_element_type=F32)
              for n in units]
        bt = [b[C - 1:C, :] for b in bc]
        upd = [_mm(v[n].T, k[n] * jnp.exp(bt[n] - bc[n])) for n in units]
        o = [None] * n_units
        for n in units:
            h = n % HG_HEADS
            o[n] = _mm_nt(q[n] * jnp.exp(bc[n]), st_scr[h])
            st_scr[h] = st_scr[h] * jnp.exp(bt[n]) + upd[n]
        heads = units
        def anchored(h, i, own_rows):
            r0 = i * SC
            anchor = bc[h][r0:r0 + 1, :]
            qd = q[h][r0:r0 + SC, :] * jnp.exp(bc[h][r0:r0 + SC, :] - anchor)
            last = r0 + SC if own_rows else r0
            cap = SPAN_MAX if own_rows else 0.0
            kd_i = jnp.where(row_c < last, k[h] * jnp.exp(jnp.minimum(anchor - bc[h], cap)), 0.0)
            return _mm_nt(qd, kd_i)

        def intra_factored():
            out = []
            for h in heads:
                blocks = [jnp.where(lane_c <= i * SC + row_sc, anchored(h, i, True), 0.0) for i in range(C // SC)]
                out.append(jnp.concatenate(blocks, axis=0))
            return tuple(out)

        def intra_exact():
            a_off = [[None] * n_units for _ in range(C // SC)]
            for i in range(1, C // SC):
                for h in heads:
                    a_off[i][h] = anchored(h, i, False)
            red = []
            for h in heads:
                prods = []
                for i in range(C // SC):
                    r0 = i * SC
                    q_i, k_i, bc_i = q[h][r0:r0 + SC, :], k[h][r0:r0 + SC, :], bc[h][r0:r0 + SC, :]
                    for s in range(SC):
                        lo = (s // R8) * R8
                        k_s = k_i[s:s + 1, :]
                        b_s = bc_i[s:s + 1, :]
                        e = jnp.where(row_8 + lo >= s, jnp.exp(bc_i[lo:lo + R8, :] - b_s), 0.0)
                        prods.append(q_i[lo:lo + R8, :] * e * k_s)
                        for r in range(lo + R8, SC, R8):
                            prods.append(q_i[r:r + R8, :] * jnp.exp(bc_i[r:r + R8, :] - b_s) * k_s)
                red.append(_mm(jnp.concatenate(prods, axis=0), ones))
            out = []
            for h in heads:
                blocks = []
                off = 0
                for i in range(C // SC):
                    r0 = i * SC
                    a_i = jnp.zeros((SC, C), F32) if i == 0 else a_off[i][h]
                    groups = [a_i[r:r + R8, :] for r in range(0, SC, R8)]
                    for s in range(SC):
                        for r in range((s // R8) * R8, SC, R8):
                            groups[r // R8] = jnp.where(lane_c == r0 + s, red[h][off:off + R8, 0:C],
                                                        groups[r // R8])
                            off += R8
                    blocks.extend(groups)
                out.append(jnp.concatenate(blocks, axis=0))
            return tuple(out)

        span = None
        for h in heads:
            for i in range(C // SC):
                d = bc[h][i * SC:i * SC + 1, :] - bc[h][(i + 1) * SC - 1:(i + 1) * SC, :]
                span = d if span is None else jnp.maximum(span, d)
        a = lax.cond(jnp.max(span) <= SPAN_MAX, intra_factored, intra_exact)
        for h in heads:
            o[h] = o[h] + _mm(a[h], v[h])
        for h in heads:
            on = o[h] * lax.rsqrt(jnp.mean(o[h] * o[h], axis=-1, keepdims=True) + RMS_EPS) * nw
            y_ref[sls[h], hcs[h]] = (on * og_ref[sls[h], hcs[h]]).astype(y_ref.dtype)
        return carry

    lax.fori_loop(0, n_chunks // unroll, chunk, 0)

    @pl.when(g == pl.num_programs(1) - 1)
    def _():
        for h in range(HG_HEADS):
            s_ref[h] = st_scr[h].T


def _hgrn(qh, lf, kk, vh, og, norm_w, s0, batch, seq, chunks_per_step):
    n_chunks = seq // CHUNK
    g_sz = min(chunks_per_step, n_chunks)
    steps = n_chunks // g_sz
    tb = g_sz * CHUNK
    r3 = lambda a: a.reshape(batch, seq, HG_WIDTH)
    tok = pl.BlockSpec((None, tb, HG_WIDTH), lambda b, g: (b, g, 0))
    st = pl.BlockSpec((None, HG_HEADS, HG_DK, HG_DV), lambda b, g: (b, 0, 0, 0))
    y, s = pl.pallas_call(
        functools.partial(_hgrn_kernel, n_chunks=g_sz, unroll=HGRN_CHUNKS_PER_TRIP if g_sz % HGRN_CHUNKS_PER_TRIP == 0 else 1),
        grid=(batch, steps),
        in_specs=[tok, tok, tok, tok, tok, pl.BlockSpec((1, HG_DV), lambda b, g: (0, 0)), st],
        out_specs=(tok, st),
        out_shape=(jax.ShapeDtypeStruct((batch, seq, HG_WIDTH), MXU_DTYPE),
                   jax.ShapeDtypeStruct((batch, HG_HEADS, HG_DK, HG_DV), F32)),
        scratch_shapes=[pltpu.VMEM((HG_HEADS, HG_DV, HG_DK), F32)],
        compiler_params=_params("parallel", "arbitrary"),
        name="hgrn",
    )(r3(qh), r3(lf), r3(kk), r3(vh), r3(og), norm_w.reshape(1, HG_DV), s0)
    return y.reshape(batch * seq, HG_WIDTH), s


def _dsa_kernel(kib_ref, iqT_ref, iwT_ref, kb_ref, aqT_ref, vT_ref, o_ref, keys_scr, acc_scr, qh_scr, p_scr,
                lg_scr, half_scr, p2_scr, lg2_scr,
                *, past, topk, tq, kt):
    i = pl.program_id(1)
    qpos0 = past + i * tq
    nk = (qpos0 + tq + kt - 1) // kt
    lane_q = lax.broadcasted_iota(I32, (1, tq), 1)
    qchunk = (qpos0 + lane_q) // CHUNK
    row_k = lax.broadcasted_iota(I32, (kt, 1), 0)
    tile = lambda j: pl.ds(pl.multiple_of(j * kt, kt), kt)
    fold = lambda a: a.reshape(kt // SUBLANES, SUBLANES, tq)

    def score_tiles(js, masked):
        raw = [[jnp.dot(kib_ref[tile(j), :], iqT_ref[h * IDX_DIM:(h + 1) * IDX_DIM, :],
                        preferred_element_type=F32) for h in range(IDX_HEADS)] for j in js]
        for j, raw_j in zip(js, raw):
            sc = jnp.zeros((kt, tq), F32)
            for h in range(IDX_HEADS):
                sc = sc + jnp.maximum(raw_j[h], 0.0) * iwT_ref[h:h + 1, :]
            bits = lax.bitcast_convert_type(sc, I32)
            key = jnp.where(bits < 0, INT_MIN - bits, bits)
            if masked:
                kchunk = (j * kt + row_k) // CHUNK
                key = jnp.where(kchunk <= qchunk, key, INT_MIN)
            keys_scr[tile(j), :] = key
            half_scr[tile(j), :] = (key >> HALF_BITS).astype(I16)

    def score_pair(jj, carry):
        score_tiles([2 * jj, 2 * jj + 1], False)
        return carry

    def score_one(j, carry):
        score_tiles([j], False)
        return carry

    lax.fori_loop(0, (nk - 1) // 2, score_pair, 0)
    lax.fori_loop(2 * ((nk - 1) // 2), nk - 1, score_one, 0)
    score_tiles([nk - 1], True)

    def count(pred):
        def one(j, acc):
            m = pred(keys_scr[tile(j), :], j * kt + row_k).astype(I32)
            return acc + jnp.sum(fold(m), axis=0)

        def two(jj, acc):
            return one(2 * jj + 1, one(2 * jj, acc))

        acc = lax.fori_loop(0, nk // 2, two, jnp.zeros((SUBLANES, tq), I32))
        acc = lax.fori_loop(2 * (nk // 2), nk, one, acc)
        return jnp.sum(acc, axis=0, keepdims=True)

    fold16 = lambda a: a.reshape(kt // PACKED_ROWS, PACKED_ROWS, tq)

    def count16(cand):
        c16 = jnp.broadcast_to(cand, (PACKED_ROWS, tq)).astype(I16)

        def one(j, acc):
            m = (fold16(half_scr[tile(j), :]) >= c16).astype(I16)
            for r in range(kt // PACKED_ROWS):
                acc = acc + m[r]
            return acc

        def four(jj, acc):
            for u in range(4):
                acc = one(4 * jj + u, acc)
            return acc

        acc = lax.fori_loop(0, nk // 4, four, jnp.zeros((PACKED_ROWS, tq), I16))
        acc = lax.fori_loop(4 * (nk // 4), nk, one, acc)
        return jnp.sum(acc.astype(I32), axis=0, keepdims=True)

    def select16(kth, n_init):
        def step(it, carry):
            lo, n_lo = carry
            cand = lo + jnp.left_shift(jnp.int32(1), HALF_BITS - 1 - it)
            cnt = count16(cand)
            take = cnt >= kth
            return jnp.where(take, cand, lo), jnp.where(take, cnt, n_lo)
        return lax.fori_loop(0, HALF_BITS, step, (jnp.full((1, tq), HALF_MIN, I32), n_init))

    t_hi, n_ge_hi = select16(topk, jnp.zeros((1, tq), I32))

    def split_lo(j, acc):
        t = keys_scr[tile(j), :]
        hi = t >> HALF_BITS
        lo_s = (t & (2 ** HALF_BITS - 1)) + HALF_MIN
        half_scr[tile(j), :] = jnp.where(hi == t_hi, lo_s, HALF_MIN).astype(I16)
        return acc + jnp.sum(fold((hi > t_hi).astype(I32)), axis=0)

    n_above = jnp.sum(lax.fori_loop(0, nk, split_lo, jnp.zeros((SUBLANES, tq), I32)), axis=0, keepdims=True)
    t_lo, n_eq_hi = select16(topk - n_above, n_ge_hi - n_above)
    thr = t_hi * 2 ** HALF_BITS + (t_lo - HALF_MIN)
    n_ge = n_above + n_eq_hi
    has_k = thr > INT_MIN
    n_gt = n_above + jnp.where(t_lo < -HALF_MIN - 1, count16(jnp.minimum(t_lo + 1, -HALF_MIN - 1)), 0)
    need = topk - n_gt
    surplus = jnp.logical_and(has_k, n_ge > topk)

    lmax_bits = max(1, (keys_scr.shape[0]).bit_length())

    def bis(it, jb):
        cand = jb + jnp.left_shift(jnp.int32(1), lmax_bits - 1 - it)
        cnt = count(lambda t, pos: jnp.logical_and(t == thr, pos < cand))
        return jnp.where(cnt <= need, cand, jb)

    any_surplus = jnp.max(surplus.astype(I32)) > 0
    jbound = lax.fori_loop(0, jnp.where(any_surplus, lmax_bits, 0), bis, jnp.zeros((1, tq), I32))

    def demote(j, carry):
        t = keys_scr[tile(j), :]
        drop = jnp.logical_and(surplus, jnp.logical_and(t == thr, j * kt + row_k >= jbound))
        keys_scr[tile(j), :] = jnp.where(drop, thr - 1, t)
        return carry

    lax.fori_loop(0, jnp.where(any_surplus, nk, 0), demote, 0)
    thr_sel = jnp.maximum(thr, INT_MIN + 1)

    half = lax.broadcasted_iota(I32, (LANES, 1), 0) // ATT_DIM
    for h in range(ATT_HEADS):
        pair = aqT_ref[(h // 2) * LANES:(h // 2 + 1) * LANES, :]
        qh_scr[h] = jnp.where(half == (h % 2), pair, jnp.zeros_like(pair))

    def logits(j, h):
        kh = kb_ref[tile(j), (h // 2) * LANES:(h // 2 + 1) * LANES]
        return jnp.dot(kh, qh_scr[h], preferred_element_type=F32)

    acc_scr[...] = jnp.zeros_like(acc_scr)
    heads = range(ATT_HEADS)
    ones_rows = jnp.ones((PACKED_ROWS, kt), MXU_DTYPE)

    def qk_and_numerators(j_next, lg_next, lg_cur, p_cur, m_new):
        tmax = []
        if lg_next is not None:
            jc = jnp.minimum(j_next, nk - 1)
            bias = jnp.where(jnp.logical_and(j_next < nk, keys_scr[tile(jc), :] >= thr_sel), 0.0, NEG_BIG)
        for h in heads:
            if lg_next is not None:
                x = logits(jc, h) + bias
                lg_next[h] = x
                tmax.append(jnp.max(jnp.max(fold(x), axis=0), axis=0, keepdims=True))
            if lg_cur is not None:
                p_cur[h] = jnp.exp(lg_cur[h] - m_new[h]).astype(p_cur.dtype)
        return tuple(tmax)

    def half_step(j, lg_cur, p_cur, lg_next, ms, ls, tmax):
        m_new = [jnp.maximum(ms[h], tmax[h]) for h in heads]
        alpha = [jnp.exp(ms[h] - m_new[h]) for h in heads]
        tmax_next = qk_and_numerators(j + 1, lg_next, lg_cur, p_cur, m_new)
        l_new = []
        for h in heads:
            rows = slice(h * ATT_DIM, (h + 1) * ATT_DIM)
            pv = jnp.dot(jnp.concatenate([vT_ref[j, rows, :], ones_rows], axis=0), p_cur[h],
                         preferred_element_type=F32)
            acc_scr[rows, :] = acc_scr[rows, :] * alpha[h] + pv[0:ATT_DIM, :]
            l_new.append(alpha[h] * ls[h] + pv[ATT_DIM:ATT_DIM + SUBLANES, :])
        return tuple(m_new), tuple(l_new), tmax_next

    def attend_pair(jj, carry):
        ms, ls, tmax = carry
        ms, ls, tmax = half_step(2 * jj, lg_scr, p_scr, lg2_scr, ms, ls, tmax)
        return half_step(2 * jj + 1, lg2_scr, p2_scr, lg_scr, ms, ls, tmax)

    def attend_last(j, carry):
        ms, ls, tmax = carry
        ms, ls, _ = half_step(nk - 1, lg_scr, p_scr, None, ms, ls, tmax)
        return ms, ls, tmax

    init = (tuple(jnp.full((1, tq), NEG_BIG, F32) for _ in heads),
            tuple(jnp.zeros((SUBLANES, tq), F32) for _ in heads),
            qk_and_numerators(0, lg_scr, None, None, None))
    carry = lax.fori_loop(0, nk // 2, attend_pair, init)
    _, ls, _ = lax.fori_loop(0, nk % 2, attend_last, carry)
    for h in heads:
        rows = slice(h * ATT_DIM, (h + 1) * ATT_DIM)
        acc_scr[rows, :] = acc_scr[rows, :] / ls[h][0:1, :]
    o_ref[...] = acc_scr[...].T.astype(o_ref.dtype)


def _dsa(kib, iqT, iwT, kb, aqT, vT, batch, n_q, l_pad, past, topk, tq, kt):
    assert kt % tq == 0 and past % tq == 0 and tq % CHUNK == 0 and n_q % tq == 0 and l_pad % kt == 0
    nq = n_q // tq
    qcol = lambda r: pl.BlockSpec((r, tq), lambda b, i: (0, b * nq + i))
    return pl.pallas_call(
        functools.partial(_dsa_kernel, past=past, topk=topk, tq=tq, kt=kt),
        grid=(batch, nq),
        in_specs=[pl.BlockSpec((None, l_pad, IDX_DIM), lambda b, i: (b, 0, 0)),
                  qcol(IDX_HEADS * IDX_DIM), qcol(SUBLANES),
                  pl.BlockSpec((None, l_pad, ATT_WIDTH), lambda b, i: (b, 0, 0)),
                  qcol(ATT_WIDTH),
                  pl.BlockSpec((l_pad // kt, ATT_WIDTH, kt), lambda b, i: (b, 0, 0))],
        out_specs=pl.BlockSpec((tq, ATT_WIDTH), lambda b, i: (b * nq + i, 0)),
        out_shape=jax.ShapeDtypeStruct((batch * n_q, ATT_WIDTH), MXU_DTYPE),
        scratch_shapes=[pltpu.VMEM((l_pad, tq), I32), pltpu.VMEM((ATT_WIDTH, tq), F32),
                        pltpu.VMEM((ATT_HEADS, LANES, tq), MXU_DTYPE),
                        pltpu.VMEM((ATT_HEADS, kt, tq), MXU_DTYPE),
                        pltpu.VMEM((ATT_HEADS, kt, tq), F32),
                        pltpu.VMEM((l_pad, tq), jnp.int16),
                        pltpu.VMEM((ATT_HEADS, kt, tq), MXU_DTYPE),
                        pltpu.VMEM((ATT_HEADS, kt, tq), F32)],
        compiler_params=_params("parallel", "arbitrary"),
        name="dsa",
    )(kib, iqT, iwT, kb, aqT, vT)


def _out_proj_kernel(yh_ref, oa_ref, sgh_ref, sga_ref, x_ref, wbh_ref, wba_ref, wo_ref, g_ref, b_ref,
                     wr_ref, br_ref, x1_ref, comb_ref):
    tm = x_ref.shape[0]
    parts = [slice(p * (tm // OUT_PROJ_PARTS), (p + 1) * (tm // OUT_PROJ_PARTS)) for p in range(OUT_PROJ_PARTS)]
    br_hg = [jnp.dot(yh_ref[r, :], wbh_ref[...], preferred_element_type=F32) for r in parts]
    br_att = [jnp.dot(oa_ref[r, :], wba_ref[...], preferred_element_type=F32) for r in parts]
    merged = [sgh_ref[r, :] * bh + sga_ref[r, :] * ba for r, bh, ba in zip(parts, br_hg, br_att)]
    out = [_mm(m, wo_ref[...]) for m in merged]
    x1 = [_layer_norm(DN_ALPHA * x_ref[r, :] + o, g_ref[...], b_ref[...]) for r, o in zip(parts, out)]
    for r, v in zip(parts, x1):
        x1_ref[r, :] = v
    logits = [_mm(v, wr_ref[...]) + br_ref[...] for v in x1]
    for r, lg in zip(parts, logits):
        comb_ref[r, :] = _route(lg)


def _route(lg):
    lane = lax.broadcasted_iota(I32, lg.shape, 1).astype(F32)
    ninf = -jnp.inf
    gmask = lane < N_GROUPS
    gl = jnp.where(gmask, lg, ninf)
    gmax = jnp.max(gl, axis=1, keepdims=True)
    gsel = jnp.min(jnp.where(gl == gmax, lane, float(ROUTER_LANES)), axis=1, keepdims=True)
    g_w = 1.0 / jnp.sum(jnp.where(gmask, jnp.exp(gl - gmax), 0.0), axis=1, keepdims=True)
    e0 = EXPERT_LANE0 + EXPERTS_PER_GROUP * gsel
    emask = jnp.logical_and(lane >= e0, lane < e0 + EXPERTS_PER_GROUP)
    el = jnp.where(emask, lg, ninf)
    emax = jnp.max(el, axis=1, keepdims=True)
    ee = jnp.where(emask, jnp.exp(el - emax), 0.0)
    prob = ee / jnp.sum(ee, axis=1, keepdims=True)
    pm = jnp.where(emask, prob, -1.0)
    p1 = jnp.max(pm, axis=1, keepdims=True)
    i1 = jnp.min(jnp.where(pm == p1, lane, float(ROUTER_LANES)), axis=1, keepdims=True)
    pm2 = jnp.where(lane == i1, -1.0, pm)
    p2 = jnp.max(pm2, axis=1, keepdims=True)
    i2 = jnp.min(jnp.where(pm2 == p2, lane, float(ROUTER_LANES)), axis=1, keepdims=True)
    tot = p1 + p2
    return (jnp.where(lane == i1, g_w * (p1 / tot), 0.0)
            + jnp.where(lane == i2, g_w * (p2 / tot), 0.0))


def _out_proj(yh, oa, sgh, sga, x, lw, tm):
    n = x.shape[0]
    tm = min(tm, n)
    full = lambda a: pl.BlockSpec(a.shape, lambda i: (0,) * a.ndim)
    rows = lambda c: pl.BlockSpec((tm, c), lambda i: (i, 0))
    weights = (lw["w_br_hg"], lw["w_br_att"], lw["w_out"], lw["ln1_g"], lw["ln1_b"], lw["w_r"], lw["b_r"])
    return pl.pallas_call(
        _out_proj_kernel,
        grid=(n // tm,),
        in_specs=[rows(HG_WIDTH), rows(ATT_WIDTH), rows(D_MODEL), rows(D_MODEL), rows(D_MODEL)]
        + [full(w) for w in weights],
        out_specs=(rows(D_MODEL), rows(ROUTER_LANES)),
        out_shape=(jax.ShapeDtypeStruct((n, D_MODEL), F32), jax.ShapeDtypeStruct((n, ROUTER_LANES), F32)),
        compiler_params=_params("parallel"),
        name="out_proj",
    )(yh, oa, sgh, sga, x, *weights)


def _moe_kernel(x_ref, comb_ref, wgu_ref, wd_ref, g_ref, b_ref, o_ref, xb_scr, hb_scr, acc_scr):
    grp = pl.program_id(1)

    @pl.when(grp == 0)
    def _():
        xb_scr[...] = x_ref[...].astype(xb_scr.dtype)
        acc_scr[...] = jnp.zeros_like(acc_scr)

    xb = xb_scr[...]
    comb = comb_ref[...]
    lane = lax.broadcasted_iota(I32, comb.shape, 1)
    F = D_EXPERT
    for e in range(EXPERTS_PER_GROUP):
        gu = jnp.dot(xb, wgu_ref[e], preferred_element_type=F32)
        h = _silu(gu[:, 0:F]) * gu[:, F:2 * F]
        c = jnp.sum(jnp.where(lane == EXPERT_LANE0 + grp * EXPERTS_PER_GROUP + e, comb, 0.0),
                    axis=1, keepdims=True)
        hb_scr[:, e * F:(e + 1) * F] = (h * c).astype(hb_scr.dtype)
    acc_scr[...] += jnp.dot(hb_scr[...], wd_ref[...], preferred_element_type=F32)

    @pl.when(grp == pl.num_programs(1) - 1)
    def _():
        o_ref[...] = _layer_norm(DN_ALPHA * x_ref[...] + acc_scr[...], g_ref[...], b_ref[...])


def _moe(x1, comb, lw, tm):
    n = x1.shape[0]
    tm = min(tm, n)
    E, F = EXPERTS_PER_GROUP, D_EXPERT
    return pl.pallas_call(
        _moe_kernel,
        grid=(n // tm, N_GROUPS),
        in_specs=[pl.BlockSpec((tm, D_MODEL), lambda i, g: (i, 0)),
                  pl.BlockSpec((tm, ROUTER_LANES), lambda i, g: (i, 0)),
                  pl.BlockSpec((None, E, D_MODEL, 2 * F), lambda i, g: (g, 0, 0, 0)),
                  pl.BlockSpec((None, E * F, D_MODEL), lambda i, g: (g, 0, 0)),
                  pl.BlockSpec((1, D_MODEL), lambda i, g: (0, 0)),
                  pl.BlockSpec((1, D_MODEL), lambda i, g: (0, 0))],
        out_specs=pl.BlockSpec((tm, D_MODEL), lambda i, g: (i, 0)),
        out_shape=jax.ShapeDtypeStruct((n, D_MODEL), F32),
        scratch_shapes=[pltpu.VMEM((tm, D_MODEL), MXU_DTYPE), pltpu.VMEM((tm, E * F), MXU_DTYPE),
                        pltpu.VMEM((tm, D_MODEL), F32)],
        compiler_params=_params("parallel", "arbitrary"),
        name="moe",
    )(x1, comb, lw["w_gu"], lw["w_d"], lw["ln2_g"], lw["ln2_b"])


def _layer_weights(l, lbs, w_in, hg_norm_w, w_br_hg, w_br_att, w_out, ln1_g, ln1_b, ln2_g, ln2_b,
                   w_rg, b_rg, w_re, b_re, w_gate, w_up, w_down):
    md = MXU_DTYPE
    offs = [0]
    for s in IN_SIZES:
        offs.append(offs[-1] + s)
    w = w_in[l]
    seg = lambda a, b: w[:, offs[a]:offs[b]]
    idx_cols = jnp.concatenate([seg(8, 9), jnp.zeros((D_MODEL, LANES - IDX_DIM), F32)], axis=1)
    iw_rows = jnp.concatenate([seg(9, 10).T, jnp.zeros((SUBLANES - IDX_HEADS, D_MODEL), F32)], axis=0)
    lb = lbs[l]
    lbp = jnp.concatenate([jnp.log(lb)[None], jnp.log1p(-lb)[None], (1.0 - lb)[None],
                           jnp.zeros((SUBLANES - 3, HG_WIDTH), F32)], axis=0)
    w_r = jnp.concatenate([w_rg[l], w_re[l], jnp.zeros((D_MODEL, ROUTER_LANES - N_GROUPS - N_EXPERTS), F32)], axis=1)
    b_r = jnp.concatenate([b_rg[l], b_re[l], jnp.zeros((ROUTER_LANES - N_GROUPS - N_EXPERTS,), F32)])[None]
    return {
        "w_h": seg(0, 4).astype(md), "w_a": seg(5, 7).astype(md), "w_i": idx_cols.astype(md),
        "w_g": seg(10, 12).astype(md),
        "w_aqT": seg(4, 5).T.astype(md), "w_iqT": seg(7, 8).T.astype(md),
        "w_iwT": iw_rows.astype(md), "lbp": lbp,
        "norm_w": hg_norm_w[l],
        "w_br_hg": w_br_hg[l].astype(md), "w_br_att": w_br_att[l].astype(md), "w_out": w_out[l].astype(md),
        "ln1_g": ln1_g[l][None], "ln1_b": ln1_b[l][None], "ln2_g": ln2_g[l][None], "ln2_b": ln2_b[l][None],
        "w_r": w_r.astype(md), "b_r": b_r,
        "w_gu": jnp.concatenate([w_gate[l], w_up[l]], axis=-1).astype(md),
        "w_d": w_down[l].reshape(N_GROUPS, EXPERTS_PER_GROUP * D_EXPERT, D_MODEL).astype(md),
    }


def _lower_bounds(lb_logits):
    p = jax.nn.softmax(lb_logits.astype(F32), axis=0)
    return jnp.concatenate([jnp.zeros_like(p[:1]), jnp.cumsum(p[1:], axis=0)], axis=0)


TM_IN = 256
TM_OUT = 512
OUT_PROJ_PARTS = 2
TM_MOE = 1024
HGRN_CHUNKS_PER_STEP = 16
HGRN_CHUNKS_PER_TRIP = 4
DSA_TQ_PROMPT = 256
DSA_TQ_SAMPLE = 128
DSA_KT = 256


def _mixer_and_ffn(x, lw, batch, seq, s0, past, layer, bufs):
    tq, kt = (DSA_TQ_PROMPT if past is None else DSA_TQ_SAMPLE), DSA_KT
    (qh, lf, kk, vh, og, k_buf, v_buf, kb, ik_buf, kib, aqT, vT, iqT, iwT, sgh, sga) = _in_proj(
        x, lw, TM_IN, kt, layer, bufs)
    y_hg, s_new = _hgrn(qh, lf, kk, vh, og, lw["norm_w"], s0, batch, seq, HGRN_CHUNKS_PER_STEP)
    if past is None:
        l_tot = seq
        l_pad = -(-l_tot // kt) * kt
        assert l_pad == l_tot and seq % tq == 0
        o_att = _dsa(kib.reshape(batch, seq, IDX_DIM), iqT, iwT, kb.reshape(batch, seq, ATT_WIDTH), aqT, vT,
                     batch, seq, l_pad, 0, min(TOPK_MAX, l_tot // 4), tq, kt)
    else:
        k_past, v_past, ki_past = past
        p_len = k_past.shape[1]
        l_tot = p_len + seq
        n_q = -(-seq // tq) * tq
        l_pad = -(-(p_len + n_q) // kt) * kt
        padk = lambda a, new: jnp.concatenate(
            [a, new, jnp.zeros((batch, l_pad - l_tot, a.shape[2]), a.dtype)], axis=1)
        k_all = padk(k_past, kb.reshape(batch, seq, ATT_WIDTH))
        ki_all = padk(ki_past, kib.reshape(batch, seq, IDX_DIM))
        v_all = padk(v_past, v_buf[layer].astype(v_past.dtype).reshape(batch, seq, ATT_WIDTH))
        vT_all = jnp.transpose(v_all.reshape(batch * (l_pad // kt), kt, ATT_WIDTH), (0, 2, 1))
        padq = lambda a: jnp.pad(a.reshape(a.shape[0], batch, seq), ((0, 0), (0, 0), (0, n_q - seq))
                                 ).reshape(a.shape[0], batch * n_q)
        o_pad = _dsa(ki_all, padq(iqT), padq(iwT), k_all, padq(aqT), vT_all,
                     batch, n_q, l_pad, p_len, min(TOPK_MAX, l_tot // 4), tq, kt)
        o_att = o_pad.reshape(batch, n_q, ATT_WIDTH)[:, :seq].reshape(batch * seq, ATT_WIDTH)
    x1, comb = _out_proj(y_hg, o_att, sgh, sga, x, lw, TM_OUT)
    x2 = _moe(x1, comb, lw, TM_MOE)
    return x2, (k_buf, v_buf, ik_buf), s_new


def kernel(x_prompt, x_sample, cache_k, cache_v, cache_idx_k, state_hgrn, w_in, hg_lb_logits, hg_norm_w,
           w_br_hg, w_br_att, w_out, ln1_g, ln1_b, ln2_g, ln2_b, w_rg, b_rg, w_re, b_re, w_gate, w_up, w_down):
    bp, tp, d = x_prompt.shape
    bs, ts, _ = x_sample.shape
    p_len = cache_k.shape[2]
    lbs = _lower_bounds(hg_lb_logits)
    xp = x_prompt.reshape(bp * tp, d)
    xs = x_sample.reshape(bs * ts, d)
    zeros_state = jnp.zeros((bp, HG_HEADS, HG_DK, HG_DV), F32)
    bufs_p, bufs_s, st_p, st_s = None, None, [], []
    for l in range(DEPTH):
        lw = _layer_weights(l, lbs, w_in, hg_norm_w, w_br_hg, w_br_att, w_out, ln1_g, ln1_b, ln2_g, ln2_b,
                            w_rg, b_rg, w_re, b_re, w_gate, w_up, w_down)
        xp, bufs_p, sp = _mixer_and_ffn(xp, lw, bp, tp, zeros_state, None, l, bufs_p)
        past = (cache_k[l].reshape(bs, p_len, ATT_WIDTH).astype(MXU_DTYPE),
                cache_v[l].reshape(bs, p_len, ATT_WIDTH).astype(MXU_DTYPE),
                cache_idx_k[l].astype(MXU_DTYPE))
        xs, bufs_s, ss = _mixer_and_ffn(xs, lw, bs, ts, state_hgrn[l].astype(F32), past, l, bufs_s)
        st_p.append(sp)
        st_s.append(ss)

    def shaped(bufs, b, t):
        k, v, ik = bufs
        return (k.reshape(DEPTH, b, t, ATT_HEADS, ATT_DIM), v.reshape(DEPTH, b, t, ATT_HEADS, ATT_DIM),
                ik.reshape(DEPTH, b, t, IDX_DIM))

    kp, vp, ikp = shaped(bufs_p, bp, tp)
    ks, vs, iks = shaped(bufs_s, bs, ts)
    return (xp.reshape(bp, tp, d), xs.reshape(bs, ts, d), kp, vp, ikp, jnp.stack(st_p), ks, vs, iks,
            jnp.stack(st_s).astype(state_hgrn.dtype))
```

```python
import functools

import jax
import jax.numpy as jnp
from jax import lax
from jax.experimental import pallas as pl
from jax.experimental.pallas import tpu as pltpu

F32 = jnp.float32
I32 = jnp.int32
MXU_DTYPE = jnp.bfloat16

D_MODEL = 1024
DEPTH = 4
CHUNK = 64
HG_HEADS = 4
HG_DK = 128
HG_DV = 128
HG_WIDTH = HG_HEADS * HG_DK
ATT_HEADS = 8
ATT_DIM = 64
ATT_WIDTH = ATT_HEADS * ATT_DIM
IDX_HEADS = 4
IDX_DIM = 64
TOPK_MAX = 256
ATT_SCALE = ATT_DIM ** -0.5
LOG2_E = 1.4426950408889634
IDX_SCALE = IDX_DIM ** -0.5
IDX_W_SCALE = IDX_HEADS ** -0.5
N_GROUPS = 4
EXPERTS_PER_GROUP = 4
N_EXPERTS = N_GROUPS * EXPERTS_PER_GROUP
D_EXPERT = 256
DN_ALPHA = (2 * DEPTH) ** 0.25
LN_EPS = 1e-5
RMS_EPS = 1e-6
IN_SIZES = (HG_WIDTH, HG_WIDTH, HG_HEADS * HG_DV, HG_HEADS * HG_DV,
            ATT_WIDTH, ATT_WIDTH, ATT_WIDTH, IDX_HEADS * IDX_DIM, IDX_DIM, IDX_HEADS,
            D_MODEL, D_MODEL)

LANES = 128
SUBLANES = 8
PACKED_ROWS = 16
I16 = jnp.int16
HALF_BITS, HALF_MIN = 16, -2 ** 15
SUBCHUNK = 16
SPAN_MAX = 60.0
VMEM_LIMIT = 56 * 1024 * 1024
INT_MIN = -2 ** 31
NEG_BIG = -1e30
ROUTER_LANES = LANES
EXPERT_LANE0 = N_GROUPS


def _params(*sem):
    return pltpu.CompilerParams(dimension_semantics=sem, vmem_limit_bytes=VMEM_LIMIT)


def _mm(a, b):
    return jnp.dot(a.astype(MXU_DTYPE), b.astype(MXU_DTYPE), preferred_element_type=F32)


def _mm_nt(a, b):
    return lax.dot_general(a.astype(MXU_DTYPE), b.astype(MXU_DTYPE), (((1,), (1,)), ((), ())),
                           preferred_element_type=F32)


def _sigmoid(x):
    return 1.0 / (1.0 + jnp.exp(-x))


def _silu(x):
    return x * _sigmoid(x)


def _layer_norm(r, g, b):
    mu = jnp.mean(r, axis=-1, keepdims=True)
    d = r - mu
    var = jnp.mean(d * d, axis=-1, keepdims=True)
    return d * lax.rsqrt(var + LN_EPS) * g + b


def _in_proj_kernel(x_ref, wh_ref, wa_ref, wi_ref, wg_ref, waqT_ref, wiqT_ref, wiwT_ref,
                    lbp_ref, *refs):
    (qh_ref, lf_ref, kk_ref, vh_ref, og_ref, k32_ref, v32_ref, kb_ref, ik32_ref, kib_ref,
     aqT_ref, vT_ref, iqT_ref, iwT_ref, sgh_ref, sga_ref) = refs[-16:]
    xb = x_ref[...].astype(MXU_DTYPE)
    W = HG_WIDTH
    qh_ref[...] = _silu(_mm(xb, wh_ref[:, 0:W]))
    z = _mm(xb, wh_ref[:, W:2 * W])
    log_lb = lbp_ref[0:1, :]
    log_1mlb = lbp_ref[1:2, :]
    one_mlb = lbp_ref[2:3, :]
    log_sig = jnp.minimum(z, 0.0) - jnp.log(1.0 + jnp.exp(-jnp.abs(z)))
    b = log_1mlb + log_sig
    lf_ref[...] = jnp.maximum(log_lb, b) + jnp.log(1.0 + jnp.exp(-jnp.abs(log_lb - b)))
    kk_ref[...] = one_mlb * _sigmoid(-z)
    vh_ref[...] = _mm(xb, wh_ref[:, 2 * W:3 * W])
    og_ref[...] = _silu(_mm(xb, wh_ref[:, 3 * W:4 * W]))
    A = ATT_WIDTH
    k = _mm(xb, wa_ref[:, 0:A])
    v = _mm(xb, wa_ref[:, A:2 * A])
    kb_ref[...] = k.astype(kb_ref.dtype)
    n_tok = k.shape[0]
    for h in range(ATT_HEADS):
        k32_ref[pl.ds(h, n_tok, stride=ATT_HEADS), :] = k[:, h * ATT_DIM:(h + 1) * ATT_DIM]
        v32_ref[pl.ds(h, n_tok, stride=ATT_HEADS), :] = v[:, h * ATT_DIM:(h + 1) * ATT_DIM]
    aqT_ref[...] = (_mm_nt(waqT_ref[...], xb) * (ATT_SCALE * LOG2_E)).astype(aqT_ref.dtype)
    kt = vT_ref.shape[-1]
    for t in range(vT_ref.shape[0]):
        vT_ref[t] = v[t * kt:(t + 1) * kt, :].T.astype(vT_ref.dtype)
    ik = _mm(xb, wi_ref[...])[:, 0:IDX_DIM]
    ik32_ref[...] = ik
    kib_ref[...] = ik.astype(kib_ref.dtype)
    iqT_ref[...] = _mm_nt(wiqT_ref[...], xb).astype(iqT_ref.dtype)
    iwT_ref[...] = _mm_nt(wiwT_ref[...], xb) * (IDX_SCALE * IDX_W_SCALE)
    D = D_MODEL
    sgh_ref[...] = _sigmoid(_mm(xb, wg_ref[:, 0:D]))
    sga_ref[...] = _sigmoid(_mm(xb, wg_ref[:, D:2 * D]))


def _in_proj(x, lw, tm, kt, layer, bufs):
    n = x.shape[0]
    tm = min(tm, n)
    assert tm % kt == 0 and n % tm == 0
    grid = (n // tm,)
    full = lambda a: pl.BlockSpec(a.shape, lambda i: (0,) * a.ndim)
    rows = lambda c: pl.BlockSpec((tm, c), lambda i: (i, 0))
    cols = lambda r: pl.BlockSpec((r, tm), lambda i: (0, i))
    lrows = lambda c: pl.BlockSpec((None, tm, c), lambda i: (layer, i, 0))
    weights = (lw["w_h"], lw["w_a"], lw["w_i"], lw["w_g"], lw["w_aqT"], lw["w_iqT"], lw["w_iwT"],
               lw["lbp"])
    out_shape = (
        jax.ShapeDtypeStruct((n, HG_WIDTH), F32),
        jax.ShapeDtypeStruct((n, HG_WIDTH), F32),
        jax.ShapeDtypeStruct((n, HG_WIDTH), F32),
        jax.ShapeDtypeStruct((n, HG_WIDTH), F32),
        jax.ShapeDtypeStruct((n, HG_WIDTH), F32),
        jax.ShapeDtypeStruct((DEPTH, n * ATT_HEADS, ATT_DIM), F32),
        jax.ShapeDtypeStruct((DEPTH, n * ATT_HEADS, ATT_DIM), F32),
        jax.ShapeDtypeStruct((n, ATT_WIDTH), MXU_DTYPE),
        jax.ShapeDtypeStruct((DEPTH, n, IDX_DIM), F32),
        jax.ShapeDtypeStruct((n, IDX_DIM), MXU_DTYPE),
        jax.ShapeDtypeStruct((ATT_WIDTH, n), MXU_DTYPE),
        jax.ShapeDtypeStruct((n // kt, ATT_WIDTH, kt), MXU_DTYPE),
        jax.ShapeDtypeStruct((IDX_HEADS * IDX_DIM, n), MXU_DTYPE),
        jax.ShapeDtypeStruct((SUBLANES, n), F32),
        jax.ShapeDtypeStruct((n, D_MODEL), F32),
        jax.ShapeDtypeStruct((n, D_MODEL), F32),
    )
    hrows = pl.BlockSpec((None, tm * ATT_HEADS, ATT_DIM), lambda i: (layer, i, 0))
    out_specs = (rows(HG_WIDTH),) * 5 + (hrows, hrows, rows(ATT_WIDTH),
                                         lrows(IDX_DIM), rows(IDX_DIM)) + (
        cols(ATT_WIDTH), pl.BlockSpec((tm // kt, ATT_WIDTH, kt), lambda i: (i, 0, 0)),
        cols(IDX_HEADS * IDX_DIM), cols(SUBLANES), rows(D_MODEL), rows(D_MODEL))
    in_specs = [rows(D_MODEL)] + [full(w) for w in weights]
    aliases = {}
    if bufs is not None:
        first = len(in_specs)
        in_specs += [pl.BlockSpec(memory_space=pl.ANY)] * len(bufs)
        aliases = {first: 5, first + 1: 6, first + 2: 8}
    return pl.pallas_call(
        _in_proj_kernel,
        grid=grid,
        in_specs=in_specs,
        out_specs=out_specs,
        out_shape=out_shape,
        input_output_aliases=aliases,
        compiler_params=_params("parallel"),
        name="in_proj",
    )(x, *weights, *(bufs or ()))


def _hgrn_kernel(q_ref, f_ref, k_ref, v_ref, og_ref, nw_ref, s0_ref, y_ref, s_ref, st_scr, *, n_chunks,
                 unroll):
    g = pl.program_id(1)

    @pl.when(g == 0)
    def _():
        for h in range(HG_HEADS):
            st_scr[h] = s0_ref[h].T

    C, SC, R8 = CHUNK, SUBCHUNK, SUBLANES
    row = lax.broadcasted_iota(I32, (C, C), 0)
    col = lax.broadcasted_iota(I32, (C, C), 1)
    tri = (row >= col).astype(F32)
    row_c = lax.broadcasted_iota(I32, (C, 1), 0)
    row_8 = lax.broadcasted_iota(I32, (R8, 1), 0)
    row_sc = lax.broadcasted_iota(I32, (SC, 1), 0)
    lane_c = lax.broadcasted_iota(I32, (1, C), 1)
    ones = jnp.ones((HG_DK, LANES), MXU_DTYPE)
    nw = nw_ref[...]

    heads = range(HG_HEADS)
    hcols = [slice(h * HG_DK, (h + 1) * HG_DK) for h in heads]

    n_units = unroll * HG_HEADS

    def chunk(c, carry):
        units = range(n_units)
        sls = [pl.ds(pl.multiple_of((c * unroll + n // HG_HEADS) * C, C), C) for n in units]
        hcs = [hcols[n % HG_HEADS] for n in units]
        q = [q_ref[sls[n], hcs[n]] for n in units]
        k = [k_ref[sls[n], hcs[n]] for n in units]
        v = [v_ref[sls[n], hcs[n]] for n in units]
        bc = [jnp.dot(tri, f_ref[sls[n], hcs[n]], precision=lax.Precision.HIGHEST, preferred_element_type=F32)
              for n in units]
        bt = [b[C - 1:C, :] for b in bc]
        upd = [_mm(v[n].T, k[n] * jnp.exp(bt[n] - bc[n])) for n in units]
        o = [None] * n_units
        for n in units:
            h = n % HG_HEADS
            o[n] = _mm_nt(q[n] * jnp.exp(bc[n]), st_scr[h])
            st_scr[h] = st_scr[h] * jnp.exp(bt[n]) + upd[n]
        heads = units
        def anchored(h, i, own_rows):
            r0 = i * SC
            anchor = bc[h][r0:r0 + 1, :]
            qd = q[h][r0:r0 + SC, :] * jnp.exp(bc[h][r0:r0 + SC, :] - anchor)
            last = r0 + SC if own_rows else r0
            cap = SPAN_MAX if own_rows else 0.0
            kd_i = jnp.where(row_c < last, k[h] * jnp.exp(jnp.minimum(anchor - bc[h], cap)), 0.0)
            return _mm_nt(qd, kd_i)

        def intra_factored():
            out = []
            for h in heads:
                blocks = [jnp.where(lane_c <= i * SC + row_sc, anchored(h, i, True), 0.0) for i in range(C // SC)]
                out.append(jnp.concatenate(blocks, axis=0))
            return tuple(out)

        def intra_exact():
            a_off = [[None] * n_units for _ in range(C // SC)]
            for i in range(1, C // SC):
                for h in heads:
                    a_off[i][h] = anchored(h, i, False)
            red = []
            for h in heads:
                prods = []
                for i in range(C // SC):
                    r0 = i * SC
                    q_i, k_i, bc_i = q[h][r0:r0 + SC, :], k[h][r0:r0 + SC, :], bc[h][r0:r0 + SC, :]
                    for s in range(SC):
                        lo = (s // R8) * R8
                        k_s = k_i[s:s + 1, :]
                        b_s = bc_i[s:s + 1, :]
                        e = jnp.where(row_8 + lo >= s, jnp.exp(bc_i[lo:lo + R8, :] - b_s), 0.0)
                        prods.append(q_i[lo:lo + R8, :] * e * k_s)
                        for r in range(lo + R8, SC, R8):
                            prods.append(q_i[r:r + R8, :] * jnp.exp(bc_i[r:r + R8, :] - b_s) * k_s)
                red.append(_mm(jnp.concatenate(prods, axis=0), ones))
            out = []
            for h in heads:
                blocks = []
                off = 0
                for i in range(C // SC):
                    r0 = i * SC
                    a_i = jnp.zeros((SC, C), F32) if i == 0 else a_off[i][h]
                    groups = [a_i[r:r + R8, :] for r in range(0, SC, R8)]
                    for s in range(SC):
                        for r in range((s // R8) * R8, SC, R8):
                            groups[r // R8] = jnp.where(lane_c == r0 + s, red[h][off:off + R8, 0:C],
                                                        groups[r // R8])
                            off += R8
                    blocks.extend(groups)
                out.append(jnp.concatenate(blocks, axis=0))
            return tuple(out)

        span = None
        for h in heads:
            for i in range(C // SC):
                d = bc[h][i * SC:i * SC + 1, :] - bc[h][(i + 1) * SC - 1:(i + 1) * SC, :]
                span = d if span is None else jnp.maximum(span, d)
        a = lax.cond(jnp.max(span) <= SPAN_MAX, intra_factored, intra_exact)
        for h in heads:
            o[h] = o[h] + _mm(a[h], v[h])
        for h in heads:
            on = o[h] * lax.rsqrt(jnp.mean(o[h] * o[h], axis=-1, keepdims=True) + RMS_EPS) * nw
            y_ref[sls[h], hcs[h]] = (on * og_ref[sls[h], hcs[h]]).astype(y_ref.dtype)
        return carry

    lax.fori_loop(0, n_chunks // unroll, chunk, 0)

    @pl.when(g == pl.num_programs(1) - 1)
    def _():
        for h in range(HG_HEADS):
            s_ref[h] = st_scr[h].T


def _hgrn(qh, lf, kk, vh, og, norm_w, s0, batch, seq, chunks_per_step):
    n_chunks = seq // CHUNK
    g_sz = min(chunks_per_step, n_chunks)
    steps = n_chunks // g_sz
    tb = g_sz * CHUNK
    r3 = lambda a: a.reshape(batch, seq, HG_WIDTH)
    tok = pl.BlockSpec((None, tb, HG_WIDTH), lambda b, g: (b, g, 0))
    st = pl.BlockSpec((None, HG_HEADS, HG_DK, HG_DV), lambda b, g: (b, 0, 0, 0))
    y, s = pl.pallas_call(
        functools.partial(_hgrn_kernel, n_chunks=g_sz, unroll=HGRN_CHUNKS_PER_TRIP if g_sz % HGRN_CHUNKS_PER_TRIP == 0 else 1),
        grid=(batch, steps),
        in_specs=[tok, tok, tok, tok, tok, pl.BlockSpec((1, HG_DV), lambda b, g: (0, 0)), st],
        out_specs=(tok, st),
        out_shape=(jax.ShapeDtypeStruct((batch, seq, HG_WIDTH), MXU_DTYPE),
                   jax.ShapeDtypeStruct((batch, HG_HEADS, HG_DK, HG_DV), F32)),
        scratch_shapes=[pltpu.VMEM((HG_HEADS, HG_DV, HG_DK), F32)],
        compiler_params=_params("parallel", "arbitrary"),
        name="hgrn",
    )(r3(qh), r3(lf), r3(kk), r3(vh), r3(og), norm_w.reshape(1, HG_DV), s0)
    return y.reshape(batch * seq, HG_WIDTH), s


def _dsa_kernel(kib_ref, iqT_ref, iwT_ref, kb_ref, aqT_ref, vT_ref, o_ref, keys_scr, acc_scr, qh_scr, p_scr,
                lg_scr, half_scr, p2_scr, lg2_scr,
                *, past, topk, tq, kt):
    i = pl.program_id(1)
    qpos0 = past + i * tq
    nk = (qpos0 + tq + kt - 1) // kt
    lane_q = lax.broadcasted_iota(I32, (1, tq), 1)
    qchunk = (qpos0 + lane_q) // CHUNK
    row_k = lax.broadcasted_iota(I32, (kt, 1), 0)
    tile = lambda j: pl.ds(pl.multiple_of(j * kt, kt), kt)
    fold = lambda a: a.reshape(kt // SUBLANES, SUBLANES, tq)

    def score_tiles(js, masked):
        raw = [[jnp.dot(kib_ref[tile(j), :], iqT_ref[h * IDX_DIM:(h + 1) * IDX_DIM, :],
                        preferred_element_type=F32) for h in range(IDX_HEADS)] for j in js]
        for j, raw_j in zip(js, raw):
            sc = jnp.zeros((kt, tq), F32)
            for h in range(IDX_HEADS):
                sc = sc + jnp.maximum(raw_j[h], 0.0) * iwT_ref[h:h + 1, :]
            bits = lax.bitcast_convert_type(sc, I32)
            key = jnp.where(bits < 0, INT_MIN - bits, bits)
            if masked:
                kchunk = (j * kt + row_k) // CHUNK
                key = jnp.where(kchunk <= qchunk, key, INT_MIN)
            keys_scr[tile(j), :] = key
            half_scr[tile(j), :] = (key >> HALF_BITS).astype(I16)

    def score_pair(jj, carry):
        score_tiles([2 * jj, 2 * jj + 1], False)
        return carry

    def score_one(j, carry):
        score_tiles([j], False)
        return carry

    lax.fori_loop(0, (nk - 1) // 2, score_pair, 0)
    lax.fori_loop(2 * ((nk - 1) // 2), nk - 1, score_one, 0)
    score_tiles([nk - 1], True)

    def count(pred):
        def one(j, acc):
            m = pred(keys_scr[tile(j), :], j * kt + row_k).astype(I32)
            return acc + jnp.sum(fold(m), axis=0)

        def two(jj, acc):
            return one(2 * jj + 1, one(2 * jj, acc))

        acc = lax.fori_loop(0, nk // 2, two, jnp.zeros((SUBLANES, tq), I32))
        acc = lax.fori_loop(2 * (nk // 2), nk, one, acc)
        return jnp.sum(acc, axis=0, keepdims=True)

    fold16 = lambda a: a.reshape(kt // PACKED_ROWS, PACKED_ROWS, tq)

    def count16(cand):
        c16 = jnp.broadcast_to(cand, (PACKED_ROWS, tq)).astype(I16)

        def one(j, acc):
            m = (fold16(half_scr[tile(j), :]) >= c16).astype(I16)
            for r in range(kt // PACKED_ROWS):
                acc = acc + m[r]
            return acc

        def four(jj, acc):
            for u in range(4):
                acc = one(4 * jj + u, acc)
            return acc

        acc = lax.fori_loop(0, nk // 4, four, jnp.zeros((PACKED_ROWS, tq), I16))
        acc = lax.fori_loop(4 * (nk // 4), nk, one, acc)
        return jnp.sum(acc.astype(I32), axis=0, keepdims=True)

    def select16(kth, n_init):
        def step(it, carry):
            lo, n_lo = carry
            cand = lo + jnp.left_shift(jnp.int32(1), HALF_BITS - 1 - it)
            cnt = count16(cand)
            take = cnt >= kth
            return jnp.where(take, cand, lo), jnp.where(take, cnt, n_lo)
        return lax.fori_loop(0, HALF_BITS, step, (jnp.full((1, tq), HALF_MIN, I32), n_init))

    t_hi, n_ge_hi = select16(topk, jnp.zeros((1, tq), I32))

    def split_lo(j, acc):
        t = keys_scr[tile(j), :]
        hi = t >> HALF_BITS
        lo_s = (t & (2 ** HALF_BITS - 1)) + HALF_MIN
        half_scr[tile(j), :] = jnp.where(hi == t_hi, lo_s, HALF_MIN).astype(I16)
        return acc + jnp.sum(fold((hi > t_hi).astype(I32)), axis=0)

    n_above = jnp.sum(lax.fori_loop(0, nk, split_lo, jnp.zeros((SUBLANES, tq), I32)), axis=0, keepdims=True)
    t_lo, n_eq_hi = select16(topk - n_above, n_ge_hi - n_above)
    thr = t_hi * 2 ** HALF_BITS + (t_lo - HALF_MIN)
    n_ge = n_above + n_eq_hi
    has_k = thr > INT_MIN
    n_gt = n_above + jnp.where(t_lo < -HALF_MIN - 1, count16(jnp.minimum(t_lo + 1, -HALF_MIN - 1)), 0)
    need = topk - n_gt
    surplus = jnp.logical_and(has_k, n_ge > topk)

    lmax_bits = max(1, (keys_scr.shape[0]).bit_length())

    def bis(it, jb):
        cand = jb + jnp.left_shift(jnp.int32(1), lmax_bits - 1 - it)
        cnt = count(lambda t, pos: jnp.logical_and(t == thr, pos < cand))
        return jnp.where(cnt <= need, cand, jb)

    any_surplus = jnp.max(surplus.astype(I32)) > 0
    jbound = lax.fori_loop(0, jnp.where(any_surplus, lmax_bits, 0), bis, jnp.zeros((1, tq), I32))

    def demote(j, carry):
        t = keys_scr[tile(j), :]
        drop = jnp.logical_and(surplus, jnp.logical_and(t == thr, j * kt + row_k >= jbound))
        keys_scr[tile(j), :] = jnp.where(drop, thr - 1, t)
        return carry

    lax.fori_loop(0, jnp.where(any_surplus, nk, 0), demote, 0)
    thr_sel = jnp.maximum(thr, INT_MIN + 1)

    half = lax.broadcasted_iota(I32, (LANES, 1), 0) // ATT_DIM
    for h in range(ATT_HEADS):
        pair = aqT_ref[(h // 2) * LANES:(h // 2 + 1) * LANES, :]
        qh_scr[h] = jnp.where(half == (h % 2), pair, jnp.zeros_like(pair))

    def logits(j, h):
        kh = kb_ref[tile(j), (h // 2) * LANES:(h // 2 + 1) * LANES]
        return jnp.dot(kh, qh_scr[h], preferred_element_type=F32)

    acc_scr[...] = jnp.zeros_like(acc_scr)
    heads = range(ATT_HEADS)
    ones_rows = jnp.ones((PACKED_ROWS, kt), MXU_DTYPE)

    def qk_and_numerators(j_next, lg_next, lg_cur, p_cur, m_new):
        tmax = []
        if lg_next is not None:
            jc = jnp.minimum(j_next, nk - 1)
            bias = jnp.where(jnp.logical_and(j_next < nk, keys_scr[tile(jc), :] >= thr_sel), 0.0, NEG_BIG)
        for h in heads:
            if lg_next is not None:
                x = logits(jc, h) + bias
                lg_next[h] = x
                tmax.append(jnp.max(jnp.max(fold(x), axis=0), axis=0, keepdims=True))
            if lg_cur is not None:
                p_cur[h] = jnp.exp2(lg_cur[h] - m_new[h]).astype(p_cur.dtype)
        return tuple(tmax)

    def half_step(j, lg_cur, p_cur, lg_next, ms, ls, tmax):
        m_new = [jnp.maximum(ms[h], tmax[h]) for h in heads]
        alpha = [jnp.exp2(ms[h] - m_new[h]) for h in heads]
        tmax_next = qk_and_numerators(j + 1, lg_next, lg_cur, p_cur, m_new)
        l_new = []
        for h in heads:
            rows = slice(h * ATT_DIM, (h + 1) * ATT_DIM)
            pv = jnp.dot(jnp.concatenate([vT_ref[j, rows, :], ones_rows], axis=0), p_cur[h],
                         preferred_element_type=F32)
            acc_scr[rows, :] = acc_scr[rows, :] * alpha[h] + pv[0:ATT_DIM, :]
            l_new.append(alpha[h] * ls[h] + pv[ATT_DIM:ATT_DIM + SUBLANES, :])
        return tuple(m_new), tuple(l_new), tmax_next

    def attend_pair(jj, carry):
        ms, ls, tmax = carry
        ms, ls, tmax = half_step(2 * jj, lg_scr, p_scr, lg2_scr, ms, ls, tmax)
        return half_step(2 * jj + 1, lg2_scr, p2_scr, lg_scr, ms, ls, tmax)

    def attend_last(j, carry):
        ms, ls, tmax = carry
        ms, ls, _ = half_step(nk - 1, lg_scr, p_scr, None, ms, ls, tmax)
        return ms, ls, tmax

    init = (tuple(jnp.full((1, tq), NEG_BIG, F32) for _ in heads),
            tuple(jnp.zeros((SUBLANES, tq), F32) for _ in heads),
            qk_and_numerators(0, lg_scr, None, None, None))
    carry = lax.fori_loop(0, nk // 2, attend_pair, init)
    _, ls, _ = lax.fori_loop(0, nk % 2, attend_last, carry)
    for h in heads:
        rows = slice(h * ATT_DIM, (h + 1) * ATT_DIM)
        acc_scr[rows, :] = acc_scr[rows, :] / ls[h][0:1, :]
    o_ref[...] = acc_scr[...].T.astype(o_ref.dtype)


def _dsa(kib, iqT, iwT, kb, aqT, vT, batch, n_q, l_pad, past, topk, tq, kt):
    assert kt % tq == 0 and past % tq == 0 and tq % CHUNK == 0 and n_q % tq == 0 and l_pad % kt == 0
    nq = n_q // tq
    qcol = lambda r: pl.BlockSpec((r, tq), lambda b, i: (0, b * nq + i))
    return pl.pallas_call(
        functools.partial(_dsa_kernel, past=past, topk=topk, tq=tq, kt=kt),
        grid=(batch, nq),
        in_specs=[pl.BlockSpec((None, l_pad, IDX_DIM), lambda b, i: (b, 0, 0)),
                  qcol(IDX_HEADS * IDX_DIM), qcol(SUBLANES),
                  pl.BlockSpec((None, l_pad, ATT_WIDTH), lambda b, i: (b, 0, 0)),
                  qcol(ATT_WIDTH),
                  pl.BlockSpec((l_pad // kt, ATT_WIDTH, kt), lambda b, i: (b, 0, 0))],
        out_specs=pl.BlockSpec((tq, ATT_WIDTH), lambda b, i: (b * nq + i, 0)),
        out_shape=jax.ShapeDtypeStruct((batch * n_q, ATT_WIDTH), MXU_DTYPE),
        scratch_shapes=[pltpu.VMEM((l_pad, tq), I32), pltpu.VMEM((ATT_WIDTH, tq), F32),
                        pltpu.VMEM((ATT_HEADS, LANES, tq), MXU_DTYPE),
                        pltpu.VMEM((ATT_HEADS, kt, tq), MXU_DTYPE),
                        pltpu.VMEM((ATT_HEADS, kt, tq), F32),
                        pltpu.VMEM((l_pad, tq), jnp.int16),
                        pltpu.VMEM((ATT_HEADS, kt, tq), MXU_DTYPE),
                        pltpu.VMEM((ATT_HEADS, kt, tq), F32)],
        compiler_params=_params("parallel", "arbitrary"),
        name="dsa",
    )(kib, iqT, iwT, kb, aqT, vT)


def _out_proj_kernel(yh_ref, oa_ref, sgh_ref, sga_ref, x_ref, wbh_ref, wba_ref, wo_ref, g_ref, b_ref,
                     wr_ref, br_ref, x1_ref, comb_ref):
    tm = x_ref.shape[0]
    parts = [slice(p * (tm // OUT_PROJ_PARTS), (p + 1) * (tm // OUT_PROJ_PARTS)) for p in range(OUT_PROJ_PARTS)]
    br_hg = [jnp.dot(yh_ref[r, :], wbh_ref[...], preferred_element_type=F32) for r in parts]
    br_att = [jnp.dot(oa_ref[r, :], wba_ref[...], preferred_element_type=F32) for r in parts]
    merged = [sgh_ref[r, :] * bh + sga_ref[r, :] * ba for r, bh, ba in zip(parts, br_hg, br_att)]
    out = [_mm(m, wo_ref[...]) for m in merged]
    x1 = [_layer_norm(DN_ALPHA * x_ref[r, :] + o, g_ref[...], b_ref[...]) for r, o in zip(parts, out)]
    for r, v in zip(parts, x1):
        x1_ref[r, :] = v
    logits = [_mm(v, wr_ref[...]) + br_ref[...] for v in x1]
    for r, lg in zip(parts, logits):
        comb_ref[r, :] = _route(lg)


def _route(lg):
    lane = lax.broadcasted_iota(I32, lg.shape, 1).astype(F32)
    ninf = -jnp.inf
    gmask = lane < N_GROUPS
    gl = jnp.where(gmask, lg, ninf)
    gmax = jnp.max(gl, axis=1, keepdims=True)
    gsel = jnp.min(jnp.where(gl == gmax, lane, float(ROUTER_LANES)), axis=1, keepdims=True)
    g_w = 1.0 / jnp.sum(jnp.where(gmask, jnp.exp(gl - gmax), 0.0), axis=1, keepdims=True)
    e0 = EXPERT_LANE0 + EXPERTS_PER_GROUP * gsel
    emask = jnp.logical_and(lane >= e0, lane < e0 + EXPERTS_PER_GROUP)
    el = jnp.where(emask, lg, ninf)
    emax = jnp.max(el, axis=1, keepdims=True)
    ee = jnp.where(emask, jnp.exp(el - emax), 0.0)
    prob = ee / jnp.sum(ee, axis=1, keepdims=True)
    pm = jnp.where(emask, prob, -1.0)
    p1 = jnp.max(pm, axis=1, keepdims=True)
    i1 = jnp.min(jnp.where(pm == p1, lane, float(ROUTER_LANES)), axis=1, keepdims=True)
    pm2 = jnp.where(lane == i1, -1.0, pm)
    p2 = jnp.max(pm2, axis=1, keepdims=True)
    i2 = jnp.min(jnp.where(pm2 == p2, lane, float(ROUTER_LANES)), axis=1, keepdims=True)
    tot = p1 + p2
    return (jnp.where(lane == i1, g_w * (p1 / tot), 0.0)
            + jnp.where(lane == i2, g_w * (p2 / tot), 0.0))


def _out_proj(yh, oa, sgh, sga, x, lw, tm):
    n = x.shape[0]
    tm = min(tm, n)
    full = lambda a: pl.BlockSpec(a.shape, lambda i: (0,) * a.ndim)
    rows = lambda c: pl.BlockSpec((tm, c), lambda i: (i, 0))
    weights = (lw["w_br_hg"], lw["w_br_att"], lw["w_out"], lw["ln1_g"], lw["ln1_b"], lw["w_r"], lw["b_r"])
    return pl.pallas_call(
        _out_proj_kernel,
        grid=(n // tm,),
        in_specs=[rows(HG_WIDTH), rows(ATT_WIDTH), rows(D_MODEL), rows(D_MODEL), rows(D_MODEL)]
        + [full(w) for w in weights],
        out_specs=(rows(D_MODEL), rows(ROUTER_LANES)),
        out_shape=(jax.ShapeDtypeStruct((n, D_MODEL), F32), jax.ShapeDtypeStruct((n, ROUTER_LANES), F32)),
        compiler_params=_params("parallel"),
        name="out_proj",
    )(yh, oa, sgh, sga, x, *weights)


def _moe_kernel(x_ref, comb_ref, wgu_ref, wd_ref, g_ref, b_ref, o_ref, xb_scr, hb_scr, acc_scr):
    grp = pl.program_id(1)

    @pl.when(grp == 0)
    def _():
        xb_scr[...] = x_ref[...].astype(xb_scr.dtype)
        acc_scr[...] = jnp.zeros_like(acc_scr)

    xb = xb_scr[...]
    comb = comb_ref[...]
    lane = lax.broadcasted_iota(I32, comb.shape, 1)
    F = D_EXPERT
    for e in range(EXPERTS_PER_GROUP):
        gu = jnp.dot(xb, wgu_ref[e], preferred_element_type=F32)
        h = _silu(gu[:, 0:F]) * gu[:, F:2 * F]
        c = jnp.sum(jnp.where(lane == EXPERT_LANE0 + grp * EXPERTS_PER_GROUP + e, comb, 0.0),
                    axis=1, keepdims=True)
        hb_scr[:, e * F:(e + 1) * F] = (h * c).astype(hb_scr.dtype)
    acc_scr[...] += jnp.dot(hb_scr[...], wd_ref[...], preferred_element_type=F32)

    @pl.when(grp == pl.num_programs(1) - 1)
    def _():
        o_ref[...] = _layer_norm(DN_ALPHA * x_ref[...] + acc_scr[...], g_ref[...], b_ref[...])


def _moe(x1, comb, lw, tm):
    n = x1.shape[0]
    tm = min(tm, n)
    E, F = EXPERTS_PER_GROUP, D_EXPERT
    return pl.pallas_call(
        _moe_kernel,
        grid=(n // tm, N_GROUPS),
        in_specs=[pl.BlockSpec((tm, D_MODEL), lambda i, g: (i, 0)),
                  pl.BlockSpec((tm, ROUTER_LANES), lambda i, g: (i, 0)),
                  pl.BlockSpec((None, E, D_MODEL, 2 * F), lambda i, g: (g, 0, 0, 0)),
                  pl.BlockSpec((None, E * F, D_MODEL), lambda i, g: (g, 0, 0)),
                  pl.BlockSpec((1, D_MODEL), lambda i, g: (0, 0)),
                  pl.BlockSpec((1, D_MODEL), lambda i, g: (0, 0))],
        out_specs=pl.BlockSpec((tm, D_MODEL), lambda i, g: (i, 0)),
        out_shape=jax.ShapeDtypeStruct((n, D_MODEL), F32),
        scratch_shapes=[pltpu.VMEM((tm, D_MODEL), MXU_DTYPE), pltpu.VMEM((tm, E * F), MXU_DTYPE),
                        pltpu.VMEM((tm, D_MODEL), F32)],
        compiler_params=_params("parallel", "arbitrary"),
        name="moe",
    )(x1, comb, lw["w_gu"], lw["w_d"], lw["ln2_g"], lw["ln2_b"])


def _layer_weights(l, lbs, w_in, hg_norm_w, w_br_hg, w_br_att, w_out, ln1_g, ln1_b, ln2_g, ln2_b,
                   w_rg, b_rg, w_re, b_re, w_gate, w_up, w_down):
    md = MXU_DTYPE
    offs = [0]
    for s in IN_SIZES:
        offs.append(offs[-1] + s)
    w = w_in[l]
    seg = lambda a, b: w[:, offs[a]:offs[b]]
    idx_cols = jnp.concatenate([seg(8, 9), jnp.zeros((D_MODEL, LANES - IDX_DIM), F32)], axis=1)
    iw_rows = jnp.concatenate([seg(9, 10).T, jnp.zeros((SUBLANES - IDX_HEADS, D_MODEL), F32)], axis=0)
    lb = lbs[l]
    lbp = jnp.concatenate([jnp.log(lb)[None], jnp.log1p(-lb)[None], (1.0 - lb)[None],
                           jnp.zeros((SUBLANES - 3, HG_WIDTH), F32)], axis=0)
    w_r = jnp.concatenate([w_rg[l], w_re[l], jnp.zeros((D_MODEL, ROUTER_LANES - N_GROUPS - N_EXPERTS), F32)], axis=1)
    b_r = jnp.concatenate([b_rg[l], b_re[l], jnp.zeros((ROUTER_LANES - N_GROUPS - N_EXPERTS,), F32)])[None]
    return {
        "w_h": seg(0, 4).astype(md), "w_a": seg(5, 7).astype(md), "w_i": idx_cols.astype(md),
        "w_g": seg(10, 12).astype(md),
        "w_aqT": seg(4, 5).T.astype(md), "w_iqT": seg(7, 8).T.astype(md),
        "w_iwT": iw_rows.astype(md), "lbp": lbp,
        "norm_w": hg_norm_w[l],
        "w_br_hg": w_br_hg[l].astype(md), "w_br_att": w_br_att[l].astype(md), "w_out": w_out[l].astype(md),
        "ln1_g": ln1_g[l][None], "ln1_b": ln1_b[l][None], "ln2_g": ln2_g[l][None], "ln2_b": ln2_b[l][None],
        "w_r": w_r.astype(md), "b_r": b_r,
        "w_gu": jnp.concatenate([w_gate[l], w_up[l]], axis=-1).astype(md),
        "w_d": w_down[l].reshape(N_GROUPS, EXPERTS_PER_GROUP * D_EXPERT, D_MODEL).astype(md),
    }


def _lower_bounds(lb_logits):
    p = jax.nn.softmax(lb_logits.astype(F32), axis=0)
    return jnp.concatenate([jnp.zeros_like(p[:1]), jnp.cumsum(p[1:], axis=0)], axis=0)


TM_IN = 256
TM_OUT = 512
OUT_PROJ_PARTS = 2
TM_MOE = 1024
HGRN_CHUNKS_PER_STEP = 16
HGRN_CHUNKS_PER_TRIP = 4
DSA_TQ_PROMPT = 256
DSA_TQ_SAMPLE = 128
DSA_KT = 256


def _mixer_and_ffn(x, lw, batch, seq, s0, past, layer, bufs):
    tq, kt = (DSA_TQ_PROMPT if past is None else DSA_TQ_SAMPLE), DSA_KT
    (qh, lf, kk, vh, og, k_buf, v_buf, kb, ik_buf, kib, aqT, vT, iqT, iwT, sgh, sga) = _in_proj(
        x, lw, TM_IN, kt, layer, bufs)
    y_hg, s_new = _hgrn(qh, lf, kk, vh, og, lw["norm_w"], s0, batch, seq, HGRN_CHUNKS_PER_STEP)
    if past is None:
        l_tot = seq
        l_pad = -(-l_tot // kt) * kt
        assert l_pad == l_tot and seq % tq == 0
        o_att = _dsa(kib.reshape(batch, seq, IDX_DIM), iqT, iwT, kb.reshape(batch, seq, ATT_WIDTH), aqT, vT,
                     batch, seq, l_pad, 0, min(TOPK_MAX, l_tot // 4), tq, kt)
    else:
        k_past, v_past, ki_past = past
        p_len = k_past.shape[1]
        l_tot = p_len + seq
        n_q = -(-seq // tq) * tq
        l_pad = -(-(p_len + n_q) // kt) * kt
        padk = lambda a, new: jnp.concatenate(
            [a, new, jnp.zeros((batch, l_pad - l_tot, a.shape[2]), a.dtype)], axis=1)
        k_all = padk(k_past, kb.reshape(batch, seq, ATT_WIDTH))
        ki_all = padk(ki_past, kib.reshape(batch, seq, IDX_DIM))
        v_all = padk(v_past, v_buf[layer].astype(v_past.dtype).reshape(batch, seq, ATT_WIDTH))
        vT_all = jnp.transpose(v_all.reshape(batch * (l_pad // kt), kt, ATT_WIDTH), (0, 2, 1))
        padq = lambda a: jnp.pad(a.reshape(a.shape[0], batch, seq), ((0, 0), (0, 0), (0, n_q - seq))
                                 ).reshape(a.shape[0], batch * n_q)
        o_pad = _dsa(ki_all, padq(iqT), padq(iwT), k_all, padq(aqT), vT_all,
                     batch, n_q, l_pad, p_len, min(TOPK_MAX, l_tot // 4), tq, kt)
        o_att = o_pad.reshape(batch, n_q, ATT_WIDTH)[:, :seq].reshape(batch * seq, ATT_WIDTH)
    x1, comb = _out_proj(y_hg, o_att, sgh, sga, x, lw, TM_OUT)
    x2 = _moe(x1, comb, lw, TM_MOE)
    return x2, (k_buf, v_buf, ik_buf), s_new


def kernel(x_prompt, x_sample, cache_k, cache_v, cache_idx_k, state_hgrn, w_in, hg_lb_logits, hg_norm_w,
           w_br_hg, w_br_att, w_out, ln1_g, ln1_b, ln2_g, ln2_b, w_rg, b_rg, w_re, b_re, w_gate, w_up, w_down):
    bp, tp, d = x_prompt.shape
    bs, ts, _ = x_sample.shape
    p_len = cache_k.shape[2]
    lbs = _lower_bounds(hg_lb_logits)
    xp = x_prompt.reshape(bp * tp, d)
    xs = x_sample.reshape(bs * ts, d)
    zeros_state = jnp.zeros((bp, HG_HEADS, HG_DK, HG_DV), F32)
    bufs_p, bufs_s, st_p, st_s = None, None, [], []
    for l in range(DEPTH):
        lw = _layer_weights(l, lbs, w_in, hg_norm_w, w_br_hg, w_br_att, w_out, ln1_g, ln1_b, ln2_g, ln2_b,
                            w_rg, b_rg, w_re, b_re, w_gate, w_up, w_down)
        xp, bufs_p, sp = _mixer_and_ffn(xp, lw, bp, tp, zeros_state, None, l, bufs_p)
        past = (cache_k[l].reshape(bs, p_len, ATT_WIDTH).astype(MXU_DTYPE),
                cache_v[l].reshape(bs, p_len, ATT_WIDTH).astype(MXU_DTYPE),
                cache_idx_k[l].astype(MXU_DTYPE))
        xs, bufs_s, ss = _mixer_and_ffn(xs, lw, bs, ts, state_hgrn[l].astype(F32), past, l, bufs_s)
        st_p.append(sp)
        st_s.append(ss)

    def shaped(bufs, b, t):
        k, v, ik = bufs
        return (k.reshape(DEPTH, b, t, ATT_HEADS, ATT_DIM), v.reshape(DEPTH, b, t, ATT_HEADS, ATT_DIM),
                ik.reshape(DEPTH, b, t, IDX_DIM))

    kp, vp, ikp = shaped(bufs_p, bp, tp)
    ks, vs, iks = shaped(bufs_s, bs, ts)
    return (xp.reshape(bp, tp, d), xs.reshape(bs, ts, d), kp, vp, ikp, jnp.stack(st_p), ks, vs, iks,
            jnp.stack(st_s).astype(state_hgrn.dtype))
```

```python
import functools

import jax
import jax.numpy as jnp
from jax import lax
from jax.experimental import pallas as pl
from jax.experimental.pallas import tpu as pltpu

F32 = jnp.float32
I32 = jnp.int32
MXU_DTYPE = jnp.bfloat16

D_MODEL = 1024
DEPTH = 4
CHUNK = 64
HG_HEADS = 4
HG_DK = 128
HG_DV = 128
HG_WIDTH = HG_HEADS * HG_DK
ATT_HEADS = 8
ATT_DIM = 64
ATT_WIDTH = ATT_HEADS * ATT_DIM
IDX_HEADS = 4
IDX_DIM = 64
TOPK_MAX = 256
ATT_SCALE = ATT_DIM ** -0.5
LOG2_E = 1.4426950408889634
IDX_SCALE = IDX_DIM ** -0.5
IDX_W_SCALE = IDX_HEADS ** -0.5
N_GROUPS = 4
EXPERTS_PER_GROUP = 4
N_EXPERTS = N_GROUPS * EXPERTS_PER_GROUP
D_EXPERT = 256
DN_ALPHA = (2 * DEPTH) ** 0.25
LN_EPS = 1e-5
RMS_EPS = 1e-6
IN_SIZES = (HG_WIDTH, HG_WIDTH, HG_HEADS * HG_DV, HG_HEADS * HG_DV,
            ATT_WIDTH, ATT_WIDTH, ATT_WIDTH, IDX_HEADS * IDX_DIM, IDX_DIM, IDX_HEADS,
            D_MODEL, D_MODEL)

LANES = 128
SUBLANES = 8
PACKED_ROWS = 16
I16 = jnp.int16
HALF_BITS, HALF_MIN = 16, -2 ** 15
SUBCHUNK = 16
SPAN_MAX = 60.0
VMEM_LIMIT = 56 * 1024 * 1024
INT_MIN = -2 ** 31
NEG_BIG = -1e30
ROUTER_LANES = LANES
EXPERT_LANE0 = N_GROUPS


def _params(*sem):
    return pltpu.CompilerParams(dimension_semantics=sem, vmem_limit_bytes=VMEM_LIMIT)


def _mm(a, b):
    return jnp.dot(a.astype(MXU_DTYPE), b.astype(MXU_DTYPE), preferred_element_type=F32)


def _mm_nt(a, b):
    return lax.dot_general(a.astype(MXU_DTYPE), b.astype(MXU_DTYPE), (((1,), (1,)), ((), ())),
                           preferred_element_type=F32)


def _sigmoid(x):
    return 1.0 / (1.0 + jnp.exp(-x))


def _silu(x):
    return x * _sigmoid(x)


def _layer_norm(r, g, b):
    mu = jnp.mean(r, axis=-1, keepdims=True)
    d = r - mu
    var = jnp.mean(d * d, axis=-1, keepdims=True)
    return d * lax.rsqrt(var + LN_EPS) * g + b


def _in_proj_kernel(x_ref, wh_ref, wa_ref, wi_ref, wg_ref, waqT_ref, wiqT_ref, wiwT_ref,
                    lbp_ref, *refs):
    (qh_ref, lf_ref, kk_ref, vh_ref, og_ref, k32_ref, v32_ref, kb_ref, ik32_ref, kib_ref,
     aqT_ref, vT_ref, iqT_ref, iwT_ref, sgh_ref, sga_ref) = refs[-16:]
    xb = x_ref[...].astype(MXU_DTYPE)
    W = HG_WIDTH
    qh_ref[...] = _silu(_mm(xb, wh_ref[:, 0:W]))
    z = _mm(xb, wh_ref[:, W:2 * W])
    log_lb = lbp_ref[0:1, :]
    log_1mlb = lbp_ref[1:2, :]
    one_mlb = lbp_ref[2:3, :]
    log_sig = jnp.minimum(z, 0.0) - jnp.log(1.0 + jnp.exp(-jnp.abs(z)))
    b = log_1mlb + log_sig
    lf_ref[...] = jnp.maximum(log_lb, b) + jnp.log(1.0 + jnp.exp(-jnp.abs(log_lb - b)))
    kk_ref[...] = one_mlb * _sigmoid(-z)
    vh_ref[...] = _mm(xb, wh_ref[:, 2 * W:3 * W])
    og_ref[...] = _silu(_mm(xb, wh_ref[:, 3 * W:4 * W]))
    A = ATT_WIDTH
    k = _mm(xb, wa_ref[:, 0:A])
    v = _mm(xb, wa_ref[:, A:2 * A])
    kb_ref[...] = k.astype(kb_ref.dtype)
    n_tok = k.shape[0]
    for h in range(ATT_HEADS):
        k32_ref[pl.ds(h, n_tok, stride=ATT_HEADS), :] = k[:, h * ATT_DIM:(h + 1) * ATT_DIM]
        v32_ref[pl.ds(h, n_tok, stride=ATT_HEADS), :] = v[:, h * ATT_DIM:(h + 1) * ATT_DIM]
    aqT_ref[...] = (_mm_nt(waqT_ref[...], xb) * (ATT_SCALE * LOG2_E)).astype(aqT_ref.dtype)
    kt = vT_ref.shape[-1]
    for t in range(vT_ref.shape[0]):
        vT_ref[t] = v[t * kt:(t + 1) * kt, :].T.astype(vT_ref.dtype)
    ik = _mm(xb, wi_ref[...])[:, 0:IDX_DIM]
    ik32_ref[...] = ik
    kib_ref[...] = ik.astype(kib_ref.dtype)
    iqT_ref[...] = _mm_nt(wiqT_ref[...], xb).astype(iqT_ref.dtype)
    iwT_ref[...] = _mm_nt(wiwT_ref[...], xb) * (IDX_SCALE * IDX_W_SCALE)
    D = D_MODEL
    sgh_ref[...] = _sigmoid(_mm(xb, wg_ref[:, 0:D]))
    sga_ref[...] = _sigmoid(_mm(xb, wg_ref[:, D:2 * D]))


def _in_proj(x, lw, tm, kt, layer, bufs):
    n = x.shape[0]
    tm = min(tm, n)
    assert tm % kt == 0 and n % tm == 0
    grid = (n // tm,)
    full = lambda a: pl.BlockSpec(a.shape, lambda i: (0,) * a.ndim)
    rows = lambda c: pl.BlockSpec((tm, c), lambda i: (i, 0))
    cols = lambda r: pl.BlockSpec((r, tm), lambda i: (0, i))
    lrows = lambda c: pl.BlockSpec((None, tm, c), lambda i: (layer, i, 0))
    weights = (lw["w_h"], lw["w_a"], lw["w_i"], lw["w_g"], lw["w_aqT"], lw["w_iqT"], lw["w_iwT"],
               lw["lbp"])
    out_shape = (
        jax.ShapeDtypeStruct((n, HG_WIDTH), F32),
        jax.ShapeDtypeStruct((n, HG_WIDTH), F32),
        jax.ShapeDtypeStruct((n, HG_WIDTH), F32),
        jax.ShapeDtypeStruct((n, HG_WIDTH), F32),
        jax.ShapeDtypeStruct((n, HG_WIDTH), F32),
        jax.ShapeDtypeStruct((DEPTH, n * ATT_HEADS, ATT_DIM), F32),
        jax.ShapeDtypeStruct((DEPTH, n * ATT_HEADS, ATT_DIM), F32),
        jax.ShapeDtypeStruct((n, ATT_WIDTH), MXU_DTYPE),
        jax.ShapeDtypeStruct((DEPTH, n, IDX_DIM), F32),
        jax.ShapeDtypeStruct((n, IDX_DIM), MXU_DTYPE),
        jax.ShapeDtypeStruct((ATT_WIDTH, n), MXU_DTYPE),
        jax.ShapeDtypeStruct((n // kt, ATT_WIDTH, kt), MXU_DTYPE),
        jax.ShapeDtypeStruct((IDX_HEADS * IDX_DIM, n), MXU_DTYPE),
        jax.ShapeDtypeStruct((SUBLANES, n), F32),
        jax.ShapeDtypeStruct((n, D_MODEL), F32),
        jax.ShapeDtypeStruct((n, D_MODEL), F32),
    )
    hrows = pl.BlockSpec((None, tm * ATT_HEADS, ATT_DIM), lambda i: (layer, i, 0))
    out_specs = (rows(HG_WIDTH),) * 5 + (hrows, hrows, rows(ATT_WIDTH),
                                         lrows(IDX_DIM), rows(IDX_DIM)) + (
        cols(ATT_WIDTH), pl.BlockSpec((tm // kt, ATT_WIDTH, kt), lambda i: (i, 0, 0)),
        cols(IDX_HEADS * IDX_DIM), cols(SUBLANES), rows(D_MODEL), rows(D_MODEL))
    in_specs = [rows(D_MODEL)] + [full(w) for w in weights]
    aliases = {}
    if bufs is not None:
        first = len(in_specs)
        in_specs += [pl.BlockSpec(memory_space=pl.ANY)] * len(bufs)
        aliases = {first: 5, first + 1: 6, first + 2: 8}
    return pl.pallas_call(
        _in_proj_kernel,
        grid=grid,
        in_specs=in_specs,
        out_specs=out_specs,
        out_shape=out_shape,
        input_output_aliases=aliases,
        compiler_params=_params("parallel"),
        name="in_proj",
    )(x, *weights, *(bufs or ()))


def _hgrn_kernel(q_ref, f_ref, k_ref, v_ref, og_ref, nw_ref, s0_ref, y_ref, s_ref, st_scr, *, n_chunks,
                 unroll):
    g = pl.program_id(1)

    @pl.when(g == 0)
    def _():
        for h in range(HG_HEADS):
            st_scr[h] = s0_ref[h].T

    C, SC, R8 = CHUNK, SUBCHUNK, SUBLANES
    row = lax.broadcasted_iota(I32, (C, C), 0)
    col = lax.broadcasted_iota(I32, (C, C), 1)
    tri = (row >= col).astype(F32)
    row_c = lax.broadcasted_iota(I32, (C, 1), 0)
    row_8 = lax.broadcasted_iota(I32, (R8, 1), 0)
    row_sc = lax.broadcasted_iota(I32, (SC, 1), 0)
    lane_c = lax.broadcasted_iota(I32, (1, C), 1)
    ones = jnp.ones((HG_DK, LANES), MXU_DTYPE)
    nw = nw_ref[...]

    heads = range(HG_HEADS)
    hcols = [slice(h * HG_DK, (h + 1) * HG_DK) for h in heads]

    n_units = unroll * HG_HEADS

    def chunk(c, carry):
        units = range(n_units)
        sls = [pl.ds(pl.multiple_of((c * unroll + n // HG_HEADS) * C, C), C) for n in units]
        hcs = [hcols[n % HG_HEADS] for n in units]
        q = [q_ref[sls[n], hcs[n]] for n in units]
        k = [k_ref[sls[n], hcs[n]] for n in units]
        v = [v_ref[sls[n], hcs[n]] for n in units]
        bc = [jnp.dot(tri, f_ref[sls[n], hcs[n]], precision=lax.Precision.HIGHEST, preferred_element_type=F32)
              for n in units]
        bt = [b[C - 1:C, :] for b in bc]
        upd = [_mm(v[n].T, k[n] * jnp.exp(bt[n] - bc[n])) for n in units]
        o = [None] * n_units
        for n in units:
            h = n % HG_HEADS
            o[n] = _mm_nt(q[n] * jnp.exp(bc[n]), st_scr[h])
            st_scr[h] = st_scr[h] * jnp.exp(bt[n]) + upd[n]
        heads = units
        def anchored(h, i, own_rows):
            r0 = i * SC
            anchor = bc[h][r0:r0 + 1, :]
            qd = q[h][r0:r0 + SC, :] * jnp.exp(bc[h][r0:r0 + SC, :] - anchor)
            last = r0 + SC if own_rows else r0
            cap = SPAN_MAX if own_rows else 0.0
            kd_i = jnp.where(row_c < last, k[h] * jnp.exp(jnp.minimum(anchor - bc[h], cap)), 0.0)
            return _mm_nt(qd, kd_i)

        def intra_factored():
            out = []
            for h in heads:
                blocks = [jnp.where(lane_c <= i * SC + row_sc, anchored(h, i, True), 0.0) for i in range(C // SC)]
                out.append(jnp.concatenate(blocks, axis=0))
            return tuple(out)

        def intra_exact():
            a_off = [[None] * n_units for _ in range(C // SC)]
            for i in range(1, C // SC):
                for h in heads:
                    a_off[i][h] = anchored(h, i, False)
            red = []
            for h in heads:
                prods = []
                for i in range(C // SC):
                    r0 = i * SC
                    q_i, k_i, bc_i = q[h][r0:r0 + SC, :], k[h][r0:r0 + SC, :], bc[h][r0:r0 + SC, :]
                    for s in range(SC):
                        lo = (s // R8) * R8
                        k_s = k_i[s:s + 1, :]
                        b_s = bc_i[s:s + 1, :]
                        e = jnp.where(row_8 + lo >= s, jnp.exp(bc_i[lo:lo + R8, :] - b_s), 0.0)
                        prods.append(q_i[lo:lo + R8, :] * e * k_s)
                        for r in range(lo + R8, SC, R8):
                            prods.append(q_i[r:r + R8, :] * jnp.exp(bc_i[r:r + R8, :] - b_s) * k_s)
                red.append(_mm(jnp.concatenate(prods, axis=0), ones))
            out = []
            for h in heads:
                blocks = []
                off = 0
                for i in range(C // SC):
                    r0 = i * SC
                    a_i = jnp.zeros((SC, C), F32) if i == 0 else a_off[i][h]
                    groups = [a_i[r:r + R8, :] for r in range(0, SC, R8)]
                    for s in range(SC):
                        for r in range((s // R8) * R8, SC, R8):
                            groups[r // R8] = jnp.where(lane_c == r0 + s, red[h][off:off + R8, 0:C],
                                                        groups[r // R8])
                            off += R8
                    blocks.extend(groups)
                out.append(jnp.concatenate(blocks, axis=0))
            return tuple(out)

        span = None
        for h in heads:
            for i in range(C // SC):
                d = bc[h][i * SC:i * SC + 1, :] - bc[h][(i + 1) * SC - 1:(i + 1) * SC, :]
                span = d if span is None else jnp.maximum(span, d)
        a = lax.cond(jnp.max(span) <= SPAN_MAX, intra_factored, intra_exact)
        for h in heads:
            o[h] = o[h] + _mm(a[h], v[h])
        for h in heads:
            on = o[h] * lax.rsqrt(jnp.mean(o[h] * o[h], axis=-1, keepdims=True) + RMS_EPS) * nw
            y_ref[sls[h], hcs[h]] = (on * og_ref[sls[h], hcs[h]]).astype(y_ref.dtype)
        return carry

    lax.fori_loop(0, n_chunks // unroll, chunk, 0)

    @pl.when(g == pl.num_programs(1) - 1)
    def _():
        for h in range(HG_HEADS):
            s_ref[h] = st_scr[h].T


def _hgrn(qh, lf, kk, vh, og, norm_w, s0, batch, seq, chunks_per_step):
    n_chunks = seq // CHUNK
    g_sz = min(chunks_per_step, n_chunks)
    steps = n_chunks // g_sz
    tb = g_sz * CHUNK
    r3 = lambda a: a.reshape(batch, seq, HG_WIDTH)
    tok = pl.BlockSpec((None, tb, HG_WIDTH), lambda b, g: (b, g, 0))
    st = pl.BlockSpec((None, HG_HEADS, HG_DK, HG_DV), lambda b, g: (b, 0, 0, 0))
    y, s = pl.pallas_call(
        functools.partial(_hgrn_kernel, n_chunks=g_sz, unroll=HGRN_CHUNKS_PER_TRIP if g_sz % HGRN_CHUNKS_PER_TRIP == 0 else 1),
        grid=(batch, steps),
        in_specs=[tok, tok, tok, tok, tok, pl.BlockSpec((1, HG_DV), lambda b, g: (0, 0)), st],
        out_specs=(tok, st),
        out_shape=(jax.ShapeDtypeStruct((batch, seq, HG_WIDTH), MXU_DTYPE),
                   jax.ShapeDtypeStruct((batch, HG_HEADS, HG_DK, HG_DV), F32)),
        scratch_shapes=[pltpu.VMEM((HG_HEADS, HG_DV, HG_DK), F32)],
        compiler_params=_params("parallel", "arbitrary"),
        name="hgrn",
    )(r3(qh), r3(lf), r3(kk), r3(vh), r3(og), norm_w.reshape(1, HG_DV), s0)
    return y.reshape(batch * seq, HG_WIDTH), s


def _dsa_kernel(kib_ref, iqT_ref, iwT_ref, kb_ref, aqT_ref, vT_ref, o_ref, keys_scr, acc_scr, qh_scr, p_scr,
                lg_scr, half_scr, p2_scr, lg2_scr,
                *, past, topk, tq, kt):
    i = pl.program_id(1)
    qpos0 = past + i * tq
    nk = (qpos0 + tq + kt - 1) // kt
    lane_q = lax.broadcasted_iota(I32, (1, tq), 1)
    qchunk = (qpos0 + lane_q) // CHUNK
    row_k = lax.broadcasted_iota(I32, (kt, 1), 0)
    tile = lambda j: pl.ds(pl.multiple_of(j * kt, kt), kt)
    fold = lambda a: a.reshape(kt // SUBLANES, SUBLANES, tq)

    def score_tiles(js, masked):
        raw = [[jnp.dot(kib_ref[tile(j), :], iqT_ref[h * IDX_DIM:(h + 1) * IDX_DIM, :],
                        preferred_element_type=F32) for h in range(IDX_HEADS)] for j in js]
        for j, raw_j in zip(js, raw):
            sc = jnp.zeros((kt, tq), F32)
            for h in range(IDX_HEADS):
                sc = sc + jnp.maximum(raw_j[h], 0.0) * iwT_ref[h:h + 1, :]
            bits = lax.bitcast_convert_type(sc, I32)
            key = jnp.where(bits < 0, INT_MIN - bits, bits)
            if masked:
                kchunk = (j * kt + row_k) // CHUNK
                key = jnp.where(kchunk <= qchunk, key, INT_MIN)
            keys_scr[tile(j), :] = key
            half_scr[tile(j), :] = (key >> HALF_BITS).astype(I16)

    def score_pair(jj, carry):
        score_tiles([SCORE_TILES_PER_TRIP * jj + u for u in range(SCORE_TILES_PER_TRIP)], False)
        return carry

    def score_one(j, carry):
        score_tiles([j], False)
        return carry

    lax.fori_loop(0, (nk - 1) // SCORE_TILES_PER_TRIP, score_pair, 0)
    lax.fori_loop(SCORE_TILES_PER_TRIP * ((nk - 1) // SCORE_TILES_PER_TRIP), nk - 1, score_one, 0)
    score_tiles([nk - 1], True)

    def count(pred):
        def one(j, acc):
            m = pred(keys_scr[tile(j), :], j * kt + row_k).astype(I32)
            return acc + jnp.sum(fold(m), axis=0)

        def two(jj, acc):
            return one(2 * jj + 1, one(2 * jj, acc))

        acc = lax.fori_loop(0, nk // 2, two, jnp.zeros((SUBLANES, tq), I32))
        acc = lax.fori_loop(2 * (nk // 2), nk, one, acc)
        return jnp.sum(acc, axis=0, keepdims=True)

    fold16 = lambda a: a.reshape(kt // PACKED_ROWS, PACKED_ROWS, tq)

    def count16(cand):
        c16 = jnp.broadcast_to(cand, (PACKED_ROWS, tq)).astype(I16)

        def one(j, acc):
            m = (fold16(half_scr[tile(j), :]) >= c16).astype(I16)
            for r in range(kt // PACKED_ROWS):
                acc = acc + m[r]
            return acc

        def four(jj, acc):
            for u in range(4):
                acc = one(4 * jj + u, acc)
            return acc

        acc = lax.fori_loop(0, nk // 4, four, jnp.zeros((PACKED_ROWS, tq), I16))
        acc = lax.fori_loop(4 * (nk // 4), nk, one, acc)
        return jnp.sum(acc.astype(I32), axis=0, keepdims=True)

    def select16(kth, n_init):
        def step(it, carry):
            lo, n_lo = carry
            cand = lo + jnp.left_shift(jnp.int32(1), HALF_BITS - 1 - it)
            cnt = count16(cand)
            take = cnt >= kth
            return jnp.where(take, cand, lo), jnp.where(take, cnt, n_lo)
        return lax.fori_loop(0, HALF_BITS, step, (jnp.full((1, tq), HALF_MIN, I32), n_init))

    t_hi, n_ge_hi = select16(topk, jnp.zeros((1, tq), I32))

    def split_lo(j, acc):
        t = keys_scr[tile(j), :]
        hi = t >> HALF_BITS
        lo_s = (t & (2 ** HALF_BITS - 1)) + HALF_MIN
        half_scr[tile(j), :] = jnp.where(hi == t_hi, lo_s, HALF_MIN).astype(I16)
        return acc + jnp.sum(fold((hi > t_hi).astype(I32)), axis=0)

    n_above = jnp.sum(lax.fori_loop(0, nk, split_lo, jnp.zeros((SUBLANES, tq), I32)), axis=0, keepdims=True)
    t_lo, n_eq_hi = select16(topk - n_above, n_ge_hi - n_above)
    thr = t_hi * 2 ** HALF_BITS + (t_lo - HALF_MIN)
    n_ge = n_above + n_eq_hi
    has_k = thr > INT_MIN
    n_gt = n_above + jnp.where(t_lo < -HALF_MIN - 1, count16(jnp.minimum(t_lo + 1, -HALF_MIN - 1)), 0)
    need = topk - n_gt
    surplus = jnp.logical_and(has_k, n_ge > topk)

    lmax_bits = max(1, (keys_scr.shape[0]).bit_length())

    def bis(it, jb):
        cand = jb + jnp.left_shift(jnp.int32(1), lmax_bits - 1 - it)
        cnt = count(lambda t, pos: jnp.logical_and(t == thr, pos < cand))
        return jnp.where(cnt <= need, cand, jb)

    any_surplus = jnp.max(surplus.astype(I32)) > 0
    jbound = lax.fori_loop(0, jnp.where(any_surplus, lmax_bits, 0), bis, jnp.zeros((1, tq), I32))

    def demote(j, carry):
        t = keys_scr[tile(j), :]
        drop = jnp.logical_and(surplus, jnp.logical_and(t == thr, j * kt + row_k >= jbound))
        keys_scr[tile(j), :] = jnp.where(drop, thr - 1, t)
        return carry

    lax.fori_loop(0, jnp.where(any_surplus, nk, 0), demote, 0)
    thr_sel = jnp.maximum(thr, INT_MIN + 1)

    half = lax.broadcasted_iota(I32, (LANES, 1), 0) // ATT_DIM
    for h in range(ATT_HEADS):
        pair = aqT_ref[(h // 2) * LANES:(h // 2 + 1) * LANES, :]
        qh_scr[h] = jnp.where(half == (h % 2), pair, jnp.zeros_like(pair))

    def logits(j, h):
        kh = kb_ref[tile(j), (h // 2) * LANES:(h // 2 + 1) * LANES]
        return jnp.dot(kh, qh_scr[h], preferred_element_type=F32)

    acc_scr[...] = jnp.zeros_like(acc_scr)
    heads = range(ATT_HEADS)
    ones_rows = jnp.ones((PACKED_ROWS, kt), MXU_DTYPE)

    def qk_and_numerators(j_next, lg_next, lg_cur, p_cur, m_new):
        tmax = []
        if lg_next is not None:
            jc = jnp.minimum(j_next, nk - 1)
            bias = jnp.where(jnp.logical_and(j_next < nk, keys_scr[tile(jc), :] >= thr_sel), 0.0, NEG_BIG)
        for h in heads:
            if lg_next is not None:
                x = logits(jc, h) + bias
                lg_next[h] = x
                tmax.append(jnp.max(jnp.max(fold(x), axis=0), axis=0, keepdims=True))
            if lg_cur is not None:
                p_cur[h] = jnp.exp2(lg_cur[h] - m_new[h]).astype(p_cur.dtype)
        return tuple(tmax)

    def half_step(j, lg_cur, p_cur, lg_next, ms, ls, tmax):
        m_new = [jnp.maximum(ms[h], tmax[h]) for h in heads]
        alpha = [jnp.exp2(ms[h] - m_new[h]) for h in heads]
        tmax_next = qk_and_numerators(j + 1, lg_next, lg_cur, p_cur, m_new)
        l_new = []
        for h in heads:
            rows = slice(h * ATT_DIM, (h + 1) * ATT_DIM)
            pv = jnp.dot(jnp.concatenate([vT_ref[j, rows, :], ones_rows], axis=0), p_cur[h],
                         preferred_element_type=F32)
            acc_scr[rows, :] = acc_scr[rows, :] * alpha[h] + pv[0:ATT_DIM, :]
            l_new.append(alpha[h] * ls[h] + pv[ATT_DIM:ATT_DIM + SUBLANES, :])
        return tuple(m_new), tuple(l_new), tmax_next

    def attend_pair(jj, carry):
        ms, ls, tmax = carry
        ms, ls, tmax = half_step(2 * jj, lg_scr, p_scr, lg2_scr, ms, ls, tmax)
        return half_step(2 * jj + 1, lg2_scr, p2_scr, lg_scr, ms, ls, tmax)

    def attend_last(j, carry):
        ms, ls, tmax = carry
        ms, ls, _ = half_step(nk - 1, lg_scr, p_scr, None, ms, ls, tmax)
        return ms, ls, tmax

    init = (tuple(jnp.full((1, tq), NEG_BIG, F32) for _ in heads),
            tuple(jnp.zeros((SUBLANES, tq), F32) for _ in heads),
            qk_and_numerators(0, lg_scr, None, None, None))
    carry = lax.fori_loop(0, nk // 2, attend_pair, init)
    _, ls, _ = lax.fori_loop(0, nk % 2, attend_last, carry)
    for h in heads:
        rows = slice(h * ATT_DIM, (h + 1) * ATT_DIM)
        acc_scr[rows, :] = acc_scr[rows, :] / ls[h][0:1, :]
    o_ref[...] = acc_scr[...].T.astype(o_ref.dtype)


def _dsa(kib, iqT, iwT, kb, aqT, vT, batch, n_q, l_pad, past, topk, tq, kt):
    assert kt % tq == 0 and past % tq == 0 and tq % CHUNK == 0 and n_q % tq == 0 and l_pad % kt == 0
    nq = n_q // tq
    qcol = lambda r: pl.BlockSpec((r, tq), lambda b, i: (0, b * nq + i))
    return pl.pallas_call(
        functools.partial(_dsa_kernel, past=past, topk=topk, tq=tq, kt=kt),
        grid=(batch, nq),
        in_specs=[pl.BlockSpec((None, l_pad, IDX_DIM), lambda b, i: (b, 0, 0)),
                  qcol(IDX_HEADS * IDX_DIM), qcol(SUBLANES),
                  pl.BlockSpec((None, l_pad, ATT_WIDTH), lambda b, i: (b, 0, 0)),
                  qcol(ATT_WIDTH),
                  pl.BlockSpec((l_pad // kt, ATT_WIDTH, kt), lambda b, i: (b, 0, 0))],
        out_specs=pl.BlockSpec((tq, ATT_WIDTH), lambda b, i: (b * nq + i, 0)),
        out_shape=jax.ShapeDtypeStruct((batch * n_q, ATT_WIDTH), MXU_DTYPE),
        scratch_shapes=[pltpu.VMEM((l_pad, tq), I32), pltpu.VMEM((ATT_WIDTH, tq), F32),
                        pltpu.VMEM((ATT_HEADS, LANES, tq), MXU_DTYPE),
                        pltpu.VMEM((ATT_HEADS, kt, tq), MXU_DTYPE),
                        pltpu.VMEM((ATT_HEADS, kt, tq), F32),
                        pltpu.VMEM((l_pad, tq), jnp.int16),
                        pltpu.VMEM((ATT_HEADS, kt, tq), MXU_DTYPE),
                        pltpu.VMEM((ATT_HEADS, kt, tq), F32)],
        compiler_params=_params("parallel", "arbitrary"),
        name="dsa",
    )(kib, iqT, iwT, kb, aqT, vT)


def _out_proj_kernel(yh_ref, oa_ref, sgh_ref, sga_ref, x_ref, wbh_ref, wba_ref, wo_ref, g_ref, b_ref,
                     wr_ref, br_ref, x1_ref, comb_ref):
    tm = x_ref.shape[0]
    parts = [slice(p * (tm // OUT_PROJ_PARTS), (p + 1) * (tm // OUT_PROJ_PARTS)) for p in range(OUT_PROJ_PARTS)]
    br_hg = [jnp.dot(yh_ref[r, :], wbh_ref[...], preferred_element_type=F32) for r in parts]
    br_att = [jnp.dot(oa_ref[r, :], wba_ref[...], preferred_element_type=F32) for r in parts]
    merged = [sgh_ref[r, :] * bh + sga_ref[r, :] * ba for r, bh, ba in zip(parts, br_hg, br_att)]
    out = [_mm(m, wo_ref[...]) for m in merged]
    x1 = [_layer_norm(DN_ALPHA * x_ref[r, :] + o, g_ref[...], b_ref[...]) for r, o in zip(parts, out)]
    for r, v in zip(parts, x1):
        x1_ref[r, :] = v
    logits = [_mm(v, wr_ref[...]) + br_ref[...] for v in x1]
    for r, lg in zip(parts, logits):
        comb_ref[r, :] = _route(lg)


def _route(lg):
    lane = lax.broadcasted_iota(I32, lg.shape, 1).astype(F32)
    ninf = -jnp.inf
    gmask = lane < N_GROUPS
    gl = jnp.where(gmask, lg, ninf)
    gmax = jnp.max(gl, axis=1, keepdims=True)
    gsel = jnp.min(jnp.where(gl == gmax, lane, float(ROUTER_LANES)), axis=1, keepdims=True)
    g_w = 1.0 / jnp.sum(jnp.where(gmask, jnp.exp(gl - gmax), 0.0), axis=1, keepdims=True)
    e0 = EXPERT_LANE0 + EXPERTS_PER_GROUP * gsel
    emask = jnp.logical_and(lane >= e0, lane < e0 + EXPERTS_PER_GROUP)
    el = jnp.where(emask, lg, ninf)
    emax = jnp.max(el, axis=1, keepdims=True)
    ee = jnp.where(emask, jnp.exp(el - emax), 0.0)
    prob = ee / jnp.sum(ee, axis=1, keepdims=True)
    pm = jnp.where(emask, prob, -1.0)
    p1 = jnp.max(pm, axis=1, keepdims=True)
    i1 = jnp.min(jnp.where(pm == p1, lane, float(ROUTER_LANES)), axis=1, keepdims=True)
    pm2 = jnp.where(lane == i1, -1.0, pm)
    p2 = jnp.max(pm2, axis=1, keepdims=True)
    i2 = jnp.min(jnp.where(pm2 == p2, lane, float(ROUTER_LANES)), axis=1, keepdims=True)
    tot = p1 + p2
    return (jnp.where(lane == i1, g_w * (p1 / tot), 0.0)
            + jnp.where(lane == i2, g_w * (p2 / tot), 0.0))


def _out_proj(yh, oa, sgh, sga, x, lw, tm):
    n = x.shape[0]
    tm = min(tm, n)
    full = lambda a: pl.BlockSpec(a.shape, lambda i: (0,) * a.ndim)
    rows = lambda c: pl.BlockSpec((tm, c), lambda i: (i, 0))
    weights = (lw["w_br_hg"], lw["w_br_att"], lw["w_out"], lw["ln1_g"], lw["ln1_b"], lw["w_r"], lw["b_r"])
    return pl.pallas_call(
        _out_proj_kernel,
        grid=(n // tm,),
        in_specs=[rows(HG_WIDTH), rows(ATT_WIDTH), rows(D_MODEL), rows(D_MODEL), rows(D_MODEL)]
        + [full(w) for w in weights],
        out_specs=(rows(D_MODEL), rows(ROUTER_LANES)),
        out_shape=(jax.ShapeDtypeStruct((n, D_MODEL), F32), jax.ShapeDtypeStruct((n, ROUTER_LANES), F32)),
        compiler_params=_params("parallel"),
        name="out_proj",
    )(yh, oa, sgh, sga, x, *weights)


def _moe_kernel(x_ref, comb_ref, wgu_ref, wd_ref, g_ref, b_ref, o_ref, xb_scr, hb_scr, acc_scr):
    grp = pl.program_id(1)

    @pl.when(grp == 0)
    def _():
        xb_scr[...] = x_ref[...].astype(xb_scr.dtype)
        acc_scr[...] = jnp.zeros_like(acc_scr)

    xb = xb_scr[...]
    comb = comb_ref[...]
    lane = lax.broadcasted_iota(I32, comb.shape, 1)
    F = D_EXPERT
    for e in range(EXPERTS_PER_GROUP):
        gu = jnp.dot(xb, wgu_ref[e], preferred_element_type=F32)
        h = _silu(gu[:, 0:F]) * gu[:, F:2 * F]
        c = jnp.sum(jnp.where(lane == EXPERT_LANE0 + grp * EXPERTS_PER_GROUP + e, comb, 0.0),
                    axis=1, keepdims=True)
        hb_scr[:, e * F:(e + 1) * F] = (h * c).astype(hb_scr.dtype)
    acc_scr[...] += jnp.dot(hb_scr[...], wd_ref[...], preferred_element_type=F32)

    @pl.when(grp == pl.num_programs(1) - 1)
    def _():
        o_ref[...] = _layer_norm(DN_ALPHA * x_ref[...] + acc_scr[...], g_ref[...], b_ref[...])


def _moe(x1, comb, lw, tm):
    n = x1.shape[0]
    tm = min(tm, n)
    E, F = EXPERTS_PER_GROUP, D_EXPERT
    return pl.pallas_call(
        _moe_kernel,
        grid=(n // tm, N_GROUPS),
        in_specs=[pl.BlockSpec((tm, D_MODEL), lambda i, g: (i, 0)),
                  pl.BlockSpec((tm, ROUTER_LANES), lambda i, g: (i, 0)),
                  pl.BlockSpec((None, E, D_MODEL, 2 * F), lambda i, g: (g, 0, 0, 0)),
                  pl.BlockSpec((None, E * F, D_MODEL), lambda i, g: (g, 0, 0)),
                  pl.BlockSpec((1, D_MODEL), lambda i, g: (0, 0)),
                  pl.BlockSpec((1, D_MODEL), lambda i, g: (0, 0))],
        out_specs=pl.BlockSpec((tm, D_MODEL), lambda i, g: (i, 0)),
        out_shape=jax.ShapeDtypeStruct((n, D_MODEL), F32),
        scratch_shapes=[pltpu.VMEM((tm, D_MODEL), MXU_DTYPE), pltpu.VMEM((tm, E * F), MXU_DTYPE),
                        pltpu.VMEM((tm, D_MODEL), F32)],
        compiler_params=_params("parallel", "arbitrary"),
        name="moe",
    )(x1, comb, lw["w_gu"], lw["w_d"], lw["ln2_g"], lw["ln2_b"])


def _layer_weights(l, lbs, w_in, hg_norm_w, w_br_hg, w_br_att, w_out, ln1_g, ln1_b, ln2_g, ln2_b,
                   w_rg, b_rg, w_re, b_re, w_gate, w_up, w_down):
    md = MXU_DTYPE
    offs = [0]
    for s in IN_SIZES:
        offs.append(offs[-1] + s)
    w = w_in[l]
    seg = lambda a, b: w[:, offs[a]:offs[b]]
    idx_cols = jnp.concatenate([seg(8, 9), jnp.zeros((D_MODEL, LANES - IDX_DIM), F32)], axis=1)
    iw_rows = jnp.concatenate([seg(9, 10).T, jnp.zeros((SUBLANES - IDX_HEADS, D_MODEL), F32)], axis=0)
    lb = lbs[l]
    lbp = jnp.concatenate([jnp.log(lb)[None], jnp.log1p(-lb)[None], (1.0 - lb)[None],
                           jnp.zeros((SUBLANES - 3, HG_WIDTH), F32)], axis=0)
    w_r = jnp.concatenate([w_rg[l], w_re[l], jnp.zeros((D_MODEL, ROUTER_LANES - N_GROUPS - N_EXPERTS), F32)], axis=1)
    b_r = jnp.concatenate([b_rg[l], b_re[l], jnp.zeros((ROUTER_LANES - N_GROUPS - N_EXPERTS,), F32)])[None]
    return {
        "w_h": seg(0, 4).astype(md), "w_a": seg(5, 7).astype(md), "w_i": idx_cols.astype(md),
        "w_g": seg(10, 12).astype(md),
        "w_aqT": seg(4, 5).T.astype(md), "w_iqT": seg(7, 8).T.astype(md),
        "w_iwT": iw_rows.astype(md), "lbp": lbp,
        "norm_w": hg_norm_w[l],
        "w_br_hg": w_br_hg[l].astype(md), "w_br_att": w_br_att[l].astype(md), "w_out": w_out[l].astype(md),
        "ln1_g": ln1_g[l][None], "ln1_b": ln1_b[l][None], "ln2_g": ln2_g[l][None], "ln2_b": ln2_b[l][None],
        "w_r": w_r.astype(md), "b_r": b_r,
        "w_gu": jnp.concatenate([w_gate[l], w_up[l]], axis=-1).astype(md),
        "w_d": w_down[l].reshape(N_GROUPS, EXPERTS_PER_GROUP * D_EXPERT, D_MODEL).astype(md),
    }


def _lower_bounds(lb_logits):
    p = jax.nn.softmax(lb_logits.astype(F32), axis=0)
    return jnp.concatenate([jnp.zeros_like(p[:1]), jnp.cumsum(p[1:], axis=0)], axis=0)


TM_IN = 256
TM_OUT = 512
OUT_PROJ_PARTS = 2
TM_MOE = 1024
HGRN_CHUNKS_PER_STEP = 16
HGRN_CHUNKS_PER_TRIP = 4
DSA_TQ_PROMPT = 256
DSA_TQ_SAMPLE = 128
DSA_KT = 256
SCORE_TILES_PER_TRIP = 3


def _mixer_and_ffn(x, lw, batch, seq, s0, past, layer, bufs):
    tq, kt = (DSA_TQ_PROMPT if past is None else DSA_TQ_SAMPLE), DSA_KT
    (qh, lf, kk, vh, og, k_buf, v_buf, kb, ik_buf, kib, aqT, vT, iqT, iwT, sgh, sga) = _in_proj(
        x, lw, TM_IN, kt, layer, bufs)
    y_hg, s_new = _hgrn(qh, lf, kk, vh, og, lw["norm_w"], s0, batch, seq, HGRN_CHUNKS_PER_STEP)
    if past is None:
        l_tot = seq
        l_pad = -(-l_tot // kt) * kt
        assert l_pad == l_tot and seq % tq == 0
        o_att = _dsa(kib.reshape(batch, seq, IDX_DIM), iqT, iwT, kb.reshape(batch, seq, ATT_WIDTH), aqT, vT,
                     batch, seq, l_pad, 0, min(TOPK_MAX, l_tot // 4), tq, kt)
    else:
        k_past, v_past, ki_past = past
        p_len = k_past.shape[1]
        l_tot = p_len + seq
        n_q = -(-seq // tq) * tq
        l_pad = -(-(p_len + n_q) // kt) * kt
        padk = lambda a, new: jnp.concatenate(
            [a, new, jnp.zeros((batch, l_pad - l_tot, a.shape[2]), a.dtype)], axis=1)
        k_all = padk(k_past, kb.reshape(batch, seq, ATT_WIDTH))
        ki_all = padk(ki_past, kib.reshape(batch, seq, IDX_DIM))
        v_all = padk(v_past, v_buf[layer].astype(v_past.dtype).reshape(batch, seq, ATT_WIDTH))
        vT_all = jnp.transpose(v_all.reshape(batch * (l_pad // kt), kt, ATT_WIDTH), (0, 2, 1))
        padq = lambda a: jnp.pad(a.reshape(a.shape[0], batch, seq), ((0, 0), (0, 0), (0, n_q - seq))
                                 ).reshape(a.shape[0], batch * n_q)
        o_pad = _dsa(ki_all, padq(iqT), padq(iwT), k_all, padq(aqT), vT_all,
                     batch, n_q, l_pad, p_len, min(TOPK_MAX, l_tot // 4), tq, kt)
        o_att = o_pad.reshape(batch, n_q, ATT_WIDTH)[:, :seq].reshape(batch * seq, ATT_WIDTH)
    x1, comb = _out_proj(y_hg, o_att, sgh, sga, x, lw, TM_OUT)
    x2 = _moe(x1, comb, lw, TM_MOE)
    return x2, (k_buf, v_buf, ik_buf), s_new


def kernel(x_prompt, x_sample, cache_k, cache_v, cache_idx_k, state_hgrn, w_in, hg_lb_logits, hg_norm_w,
           w_br_hg, w_br_att, w_out, ln1_g, ln1_b, ln2_g, ln2_b, w_rg, b_rg, w_re, b_re, w_gate, w_up, w_down):
    bp, tp, d = x_prompt.shape
    bs, ts, _ = x_sample.shape
    p_len = cache_k.shape[2]
    lbs = _lower_bounds(hg_lb_logits)
    xp = x_prompt.reshape(bp * tp, d)
    xs = x_sample.reshape(bs * ts, d)
    zeros_state = jnp.zeros((bp, HG_HEADS, HG_DK, HG_DV), F32)
    bufs_p, bufs_s, st_p, st_s = None, None, [], []
    for l in range(DEPTH):
        lw = _layer_weights(l, lbs, w_in, hg_norm_w, w_br_hg, w_br_att, w_out, ln1_g, ln1_b, ln2_g, ln2_b,
                            w_rg, b_rg, w_re, b_re, w_gate, w_up, w_down)
        xp, bufs_p, sp = _mixer_and_ffn(xp, lw, bp, tp, zeros_state, None, l, bufs_p)
        past = (cache_k[l].reshape(bs, p_len, ATT_WIDTH).astype(MXU_DTYPE),
                cache_v[l].reshape(bs, p_len, ATT_WIDTH).astype(MXU_DTYPE),
                cache_idx_k[l].astype(MXU_DTYPE))
        xs, bufs_s, ss = _mixer_and_ffn(xs, lw, bs, ts, state_hgrn[l].astype(F32), past, l, bufs_s)
        st_p.append(sp)
        st_s.append(ss)

    def shaped(bufs, b, t):
        k, v, ik = bufs
        return (k.reshape(DEPTH, b, t, ATT_HEADS, ATT_DIM), v.reshape(DEPTH, b, t, ATT_HEADS, ATT_DIM),
                ik.reshape(DEPTH, b, t, IDX_DIM))

    kp, vp, ikp = shaped(bufs_p, bp, tp)
    ks, vs, iks = shaped(bufs_s, bs, ts)
    return (xp.reshape(bp, tp, d), xs.reshape(bs, ts, d), kp, vp, ikp, jnp.stack(st_p), ks, vs, iks,
            jnp.stack(st_s).astype(state_hgrn.dtype))
```

```python
import functools

import jax
import jax.numpy as jnp
from jax import lax
from jax.experimental import pallas as pl
from jax.experimental.pallas import tpu as pltpu

F32 = jnp.float32
I32 = jnp.int32
MXU_DTYPE = jnp.bfloat16

D_MODEL = 1024
DEPTH = 4
CHUNK = 64
HG_HEADS = 4
HG_DK = 128
HG_DV = 128
HG_WIDTH = HG_HEADS * HG_DK
ATT_HEADS = 8
ATT_DIM = 64
ATT_WIDTH = ATT_HEADS * ATT_DIM
IDX_HEADS = 4
IDX_DIM = 64
TOPK_MAX = 256
ATT_SCALE = ATT_DIM ** -0.5
LOG2_E = 1.4426950408889634
IDX_SCALE = IDX_DIM ** -0.5
IDX_W_SCALE = IDX_HEADS ** -0.5
N_GROUPS = 4
EXPERTS_PER_GROUP = 4
N_EXPERTS = N_GROUPS * EXPERTS_PER_GROUP
D_EXPERT = 256
DN_ALPHA = (2 * DEPTH) ** 0.25
LN_EPS = 1e-5
RMS_EPS = 1e-6
IN_SIZES = (HG_WIDTH, HG_WIDTH, HG_HEADS * HG_DV, HG_HEADS * HG_DV,
            ATT_WIDTH, ATT_WIDTH, ATT_WIDTH, IDX_HEADS * IDX_DIM, IDX_DIM, IDX_HEADS,
            D_MODEL, D_MODEL)

LANES = 128
SUBLANES = 8
PACKED_ROWS = 16
I16 = jnp.int16
HALF_BITS, HALF_MIN = 16, -2 ** 15
SUBCHUNK = 16
SPAN_MAX = 60.0
VMEM_LIMIT = 56 * 1024 * 1024
INT_MIN = -2 ** 31
NEG_BIG = -1e30
ROUTER_LANES = LANES
EXPERT_LANE0 = N_GROUPS


def _params(*sem):
    return pltpu.CompilerParams(dimension_semantics=sem, vmem_limit_bytes=VMEM_LIMIT)


def _mm(a, b):
    return jnp.dot(a.astype(MXU_DTYPE), b.astype(MXU_DTYPE), preferred_element_type=F32)


def _mm_nt(a, b):
    return lax.dot_general(a.astype(MXU_DTYPE), b.astype(MXU_DTYPE), (((1,), (1,)), ((), ())),
                           preferred_element_type=F32)


def _sigmoid(x):
    return 1.0 / (1.0 + jnp.exp(-x))


def _silu(x):
    return x * _sigmoid(x)


def _layer_norm(r, g, b):
    mu = jnp.mean(r, axis=-1, keepdims=True)
    d = r - mu
    var = jnp.mean(d * d, axis=-1, keepdims=True)
    return d * lax.rsqrt(var + LN_EPS) * g + b


def _in_proj_kernel(x_ref, wh_ref, wa_ref, wi_ref, wg_ref, waqT_ref, wiqT_ref, wiwT_ref,
                    lbp_ref, *refs):
    (qh_ref, lf_ref, kk_ref, vh_ref, og_ref, k32_ref, v32_ref, kb_ref, ik32_ref, kib_ref,
     aqT_ref, vT_ref, iqT_ref, iwT_ref, sgh_ref, sga_ref) = refs[-16:]
    xb = x_ref[...].astype(MXU_DTYPE)
    W = HG_WIDTH
    qh_ref[...] = _silu(_mm(xb, wh_ref[:, 0:W]))
    z = _mm(xb, wh_ref[:, W:2 * W])
    log_lb = lbp_ref[0:1, :]
    log_1mlb = lbp_ref[1:2, :]
    one_mlb = lbp_ref[2:3, :]
    log_sig = jnp.minimum(z, 0.0) - jnp.log(1.0 + jnp.exp(-jnp.abs(z)))
    b = log_1mlb + log_sig
    lf_ref[...] = jnp.maximum(log_lb, b) + jnp.log(1.0 + jnp.exp(-jnp.abs(log_lb - b)))
    kk_ref[...] = one_mlb * _sigmoid(-z)
    vh_ref[...] = _mm(xb, wh_ref[:, 2 * W:3 * W])
    og_ref[...] = _silu(_mm(xb, wh_ref[:, 3 * W:4 * W]))
    A = ATT_WIDTH
    k = _mm(xb, wa_ref[:, 0:A])
    v = _mm(xb, wa_ref[:, A:2 * A])
    kb_ref[...] = k.astype(kb_ref.dtype)
    n_tok = k.shape[0]
    for h in range(ATT_HEADS):
        k32_ref[pl.ds(h, n_tok, stride=ATT_HEADS), :] = k[:, h * ATT_DIM:(h + 1) * ATT_DIM]
        v32_ref[pl.ds(h, n_tok, stride=ATT_HEADS), :] = v[:, h * ATT_DIM:(h + 1) * ATT_DIM]
    aqT_ref[...] = (_mm_nt(waqT_ref[...], xb) * (ATT_SCALE * LOG2_E)).astype(aqT_ref.dtype)
    kt = vT_ref.shape[-1]
    for t in range(vT_ref.shape[0]):
        vT_ref[t] = v[t * kt:(t + 1) * kt, :].T.astype(vT_ref.dtype)
    ik = _mm(xb, wi_ref[...])[:, 0:IDX_DIM]
    ik32_ref[...] = ik
    kib_ref[...] = ik.astype(kib_ref.dtype)
    iqT_ref[...] = _mm_nt(wiqT_ref[...], xb).astype(iqT_ref.dtype)
    iwT_ref[...] = _mm_nt(wiwT_ref[...], xb) * (IDX_SCALE * IDX_W_SCALE)
    D = D_MODEL
    sgh_ref[...] = _sigmoid(_mm(xb, wg_ref[:, 0:D]))
    sga_ref[...] = _sigmoid(_mm(xb, wg_ref[:, D:2 * D]))


def _in_proj(x, lw, tm, kt, layer, bufs):
    n = x.shape[0]
    tm = min(tm, n)
    assert tm % kt == 0 and n % tm == 0
    grid = (n // tm,)
    full = lambda a: pl.BlockSpec(a.shape, lambda i: (0,) * a.ndim)
    rows = lambda c: pl.BlockSpec((tm, c), lambda i: (i, 0))
    cols = lambda r: pl.BlockSpec((r, tm), lambda i: (0, i))
    lrows = lambda c: pl.BlockSpec((None, tm, c), lambda i: (layer, i, 0))
    weights = (lw["w_h"], lw["w_a"], lw["w_i"], lw["w_g"], lw["w_aqT"], lw["w_iqT"], lw["w_iwT"],
               lw["lbp"])
    out_shape = (
        jax.ShapeDtypeStruct((n, HG_WIDTH), F32),
        jax.ShapeDtypeStruct((n, HG_WIDTH), F32),
        jax.ShapeDtypeStruct((n, HG_WIDTH), F32),
        jax.ShapeDtypeStruct((n, HG_WIDTH), F32),
        jax.ShapeDtypeStruct((n, HG_WIDTH), F32),
        jax.ShapeDtypeStruct((DEPTH, n * ATT_HEADS, ATT_DIM), F32),
        jax.ShapeDtypeStruct((DEPTH, n * ATT_HEADS, ATT_DIM), F32),
        jax.ShapeDtypeStruct((n, ATT_WIDTH), MXU_DTYPE),
        jax.ShapeDtypeStruct((DEPTH, n, IDX_DIM), F32),
        jax.ShapeDtypeStruct((n, IDX_DIM), MXU_DTYPE),
        jax.ShapeDtypeStruct((ATT_WIDTH, n), MXU_DTYPE),
        jax.ShapeDtypeStruct((n // kt, ATT_WIDTH, kt), MXU_DTYPE),
        jax.ShapeDtypeStruct((IDX_HEADS * IDX_DIM, n), MXU_DTYPE),
        jax.ShapeDtypeStruct((SUBLANES, n), F32),
        jax.ShapeDtypeStruct((n, D_MODEL), F32),
        jax.ShapeDtypeStruct((n, D_MODEL), F32),
    )
    hrows = pl.BlockSpec((None, tm * ATT_HEADS, ATT_DIM), lambda i: (layer, i, 0))
    out_specs = (rows(HG_WIDTH),) * 5 + (hrows, hrows, rows(ATT_WIDTH),
                                         lrows(IDX_DIM), rows(IDX_DIM)) + (
        cols(ATT_WIDTH), pl.BlockSpec((tm // kt, ATT_WIDTH, kt), lambda i: (i, 0, 0)),
        cols(IDX_HEADS * IDX_DIM), cols(SUBLANES), rows(D_MODEL), rows(D_MODEL))
    in_specs = [rows(D_MODEL)] + [full(w) for w in weights]
    aliases = {}
    if bufs is not None:
        first = len(in_specs)
        in_specs += [pl.BlockSpec(memory_space=pl.ANY)] * len(bufs)
        aliases = {first: 5, first + 1: 6, first + 2: 8}
    return pl.pallas_call(
        _in_proj_kernel,
        grid=grid,
        in_specs=in_specs,
        out_specs=out_specs,
        out_shape=out_shape,
        input_output_aliases=aliases,
        compiler_params=_params("parallel"),
        name="in_proj",
    )(x, *weights, *(bufs or ()))


def _hgrn_kernel(q_ref, f_ref, k_ref, v_ref, og_ref, nw_ref, s0_ref, y_ref, s_ref, st_scr, *, n_chunks,
                 unroll):
    g = pl.program_id(1)

    @pl.when(g == 0)
    def _():
        for h in range(HG_HEADS):
            st_scr[h] = s0_ref[h].T

    C, SC, R8 = CHUNK, SUBCHUNK, SUBLANES
    row = lax.broadcasted_iota(I32, (C, C), 0)
    col = lax.broadcasted_iota(I32, (C, C), 1)
    tri = (row >= col).astype(F32)
    row_c = lax.broadcasted_iota(I32, (C, 1), 0)
    row_8 = lax.broadcasted_iota(I32, (R8, 1), 0)
    row_sc = lax.broadcasted_iota(I32, (SC, 1), 0)
    lane_c = lax.broadcasted_iota(I32, (1, C), 1)
    ones = jnp.ones((HG_DK, LANES), MXU_DTYPE)
    nw = nw_ref[...]

    heads = range(HG_HEADS)
    hcols = [slice(h * HG_DK, (h + 1) * HG_DK) for h in heads]

    n_units = unroll * HG_HEADS

    def chunk(c, carry):
        units = range(n_units)
        sls = [pl.ds(pl.multiple_of((c * unroll + n // HG_HEADS) * C, C), C) for n in units]
        hcs = [hcols[n % HG_HEADS] for n in units]
        q = [q_ref[sls[n], hcs[n]] for n in units]
        k = [k_ref[sls[n], hcs[n]] for n in units]
        v = [v_ref[sls[n], hcs[n]] for n in units]
        bc = [jnp.dot(tri, f_ref[sls[n], hcs[n]], precision=lax.Precision.HIGHEST, preferred_element_type=F32)
              for n in units]
        bt = [b[C - 1:C, :] for b in bc]
        upd = [_mm(v[n].T, k[n] * jnp.exp(bt[n] - bc[n])) for n in units]
        o = [None] * n_units
        for n in units:
            h = n % HG_HEADS
            o[n] = _mm_nt(q[n] * jnp.exp(bc[n]), st_scr[h])
            st_scr[h] = st_scr[h] * jnp.exp(bt[n]) + upd[n]
        heads = units
        def anchored(h, i, own_rows):
            r0 = i * SC
            anchor = bc[h][r0:r0 + 1, :]
            qd = q[h][r0:r0 + SC, :] * jnp.exp(bc[h][r0:r0 + SC, :] - anchor)
            last = r0 + SC if own_rows else r0
            cap = SPAN_MAX if own_rows else 0.0
            kd_i = jnp.where(row_c < last, k[h] * jnp.exp(jnp.minimum(anchor - bc[h], cap)), 0.0)
            return _mm_nt(qd, kd_i)

        def intra_factored():
            out = []
            for h in heads:
                blocks = [jnp.where(lane_c <= i * SC + row_sc, anchored(h, i, True), 0.0) for i in range(C // SC)]
                out.append(jnp.concatenate(blocks, axis=0))
            return tuple(out)

        def intra_exact():
            a_off = [[None] * n_units for _ in range(C // SC)]
            for i in range(1, C // SC):
                for h in heads:
                    a_off[i][h] = anchored(h, i, False)
            red = []
            for h in heads:
                prods = []
                for i in range(C // SC):
                    r0 = i * SC
                    q_i, k_i, bc_i = q[h][r0:r0 + SC, :], k[h][r0:r0 + SC, :], bc[h][r0:r0 + SC, :]
                    for s in range(SC):
                        lo = (s // R8) * R8
                        k_s = k_i[s:s + 1, :]
                        b_s = bc_i[s:s + 1, :]
                        e = jnp.where(row_8 + lo >= s, jnp.exp(bc_i[lo:lo + R8, :] - b_s), 0.0)
                        prods.append(q_i[lo:lo + R8, :] * e * k_s)
                        for r in range(lo + R8, SC, R8):
                            prods.append(q_i[r:r + R8, :] * jnp.exp(bc_i[r:r + R8, :] - b_s) * k_s)
                red.append(_mm(jnp.concatenate(prods, axis=0), ones))
            out = []
            for h in heads:
                blocks = []
                off = 0
                for i in range(C // SC):
                    r0 = i * SC
                    a_i = jnp.zeros((SC, C), F32) if i == 0 else a_off[i][h]
                    groups = [a_i[r:r + R8, :] for r in range(0, SC, R8)]
                    for s in range(SC):
                        for r in range((s // R8) * R8, SC, R8):
                            groups[r // R8] = jnp.where(lane_c == r0 + s, red[h][off:off + R8, 0:C],
                                                        groups[r // R8])
                            off += R8
                    blocks.extend(groups)
                out.append(jnp.concatenate(blocks, axis=0))
            return tuple(out)

        span = None
        for h in heads:
            for i in range(C // SC):
                d = bc[h][i * SC:i * SC + 1, :] - bc[h][(i + 1) * SC - 1:(i + 1) * SC, :]
                span = d if span is None else jnp.maximum(span, d)
        a = lax.cond(jnp.max(span) <= SPAN_MAX, intra_factored, intra_exact)
        for h in heads:
            o[h] = o[h] + _mm(a[h], v[h])
        for h in heads:
            on = o[h] * lax.rsqrt(jnp.mean(o[h] * o[h], axis=-1, keepdims=True) + RMS_EPS) * nw
            y_ref[sls[h], hcs[h]] = (on * og_ref[sls[h], hcs[h]]).astype(y_ref.dtype)
        return carry

    lax.fori_loop(0, n_chunks // unroll, chunk, 0)

    @pl.when(g == pl.num_programs(1) - 1)
    def _():
        for h in range(HG_HEADS):
            s_ref[h] = st_scr[h].T


def _hgrn(qh, lf, kk, vh, og, norm_w, s0, batch, seq, chunks_per_step):
    n_chunks = seq // CHUNK
    g_sz = min(chunks_per_step, n_chunks)
    steps = n_chunks // g_sz
    tb = g_sz * CHUNK
    r3 = lambda a: a.reshape(batch, seq, HG_WIDTH)
    tok = pl.BlockSpec((None, tb, HG_WIDTH), lambda b, g: (b, g, 0))
    st = pl.BlockSpec((None, HG_HEADS, HG_DK, HG_DV), lambda b, g: (b, 0, 0, 0))
    y, s = pl.pallas_call(
        functools.partial(_hgrn_kernel, n_chunks=g_sz, unroll=HGRN_CHUNKS_PER_TRIP if g_sz % HGRN_CHUNKS_PER_TRIP == 0 else 1),
        grid=(batch, steps),
        in_specs=[tok, tok, tok, tok, tok, pl.BlockSpec((1, HG_DV), lambda b, g: (0, 0)), st],
        out_specs=(tok, st),
        out_shape=(jax.ShapeDtypeStruct((batch, seq, HG_WIDTH), MXU_DTYPE),
                   jax.ShapeDtypeStruct((batch, HG_HEADS, HG_DK, HG_DV), F32)),
        scratch_shapes=[pltpu.VMEM((HG_HEADS, HG_DV, HG_DK), F32)],
        compiler_params=_params("parallel", "arbitrary"),
        name="hgrn",
    )(r3(qh), r3(lf), r3(kk), r3(vh), r3(og), norm_w.reshape(1, HG_DV), s0)
    return y.reshape(batch * seq, HG_WIDTH), s


def _dsa_kernel(kib_ref, iqT_ref, iwT_ref, kb_ref, aqT_ref, vT_ref, o_ref, keys_scr, acc_scr, qh_scr, p_scr,
                lg_scr, half_scr, p2_scr, lg2_scr,
                *, past, topk, tq, kt):
    i = pl.program_id(1)
    qpos0 = past + i * tq
    nk = (qpos0 + tq + kt - 1) // kt
    lane_q = lax.broadcasted_iota(I32, (1, tq), 1)
    qchunk = (qpos0 + lane_q) // CHUNK
    row_k = lax.broadcasted_iota(I32, (kt, 1), 0)
    tile = lambda j: pl.ds(pl.multiple_of(j * kt, kt), kt)
    fold = lambda a: a.reshape(kt // SUBLANES, SUBLANES, tq)

    def score_tiles(js, masked):
        raw = [[jnp.dot(kib_ref[tile(j), :], iqT_ref[h * IDX_DIM:(h + 1) * IDX_DIM, :],
                        preferred_element_type=F32) for h in range(IDX_HEADS)] for j in js]
        for j, raw_j in zip(js, raw):
            sc = jnp.zeros((kt, tq), F32)
            for h in range(IDX_HEADS):
                sc = sc + jnp.maximum(raw_j[h], 0.0) * iwT_ref[h:h + 1, :]
            bits = lax.bitcast_convert_type(sc, I32)
            key = jnp.where(bits < 0, INT_MIN - bits, bits)
            if masked:
                kchunk = (j * kt + row_k) // CHUNK
                key = jnp.where(kchunk <= qchunk, key, INT_MIN)
            keys_scr[tile(j), :] = key
            half_scr[tile(j), :] = (key >> HALF_BITS).astype(I16)

    def score_pair(jj, carry):
        score_tiles([SCORE_TILES_PER_TRIP * jj + u for u in range(SCORE_TILES_PER_TRIP)], False)
        return carry

    def score_one(j, carry):
        score_tiles([j], False)
        return carry

    lax.fori_loop(0, (nk - 1) // SCORE_TILES_PER_TRIP, score_pair, 0)
    lax.fori_loop(SCORE_TILES_PER_TRIP * ((nk - 1) // SCORE_TILES_PER_TRIP), nk - 1, score_one, 0)
    score_tiles([nk - 1], True)

    def count(pred):
        def one(j, acc):
            m = pred(keys_scr[tile(j), :], j * kt + row_k).astype(I32)
            return acc + jnp.sum(fold(m), axis=0)

        def two(jj, acc):
            return one(2 * jj + 1, one(2 * jj, acc))

        acc = lax.fori_loop(0, nk // 2, two, jnp.zeros((SUBLANES, tq), I32))
        acc = lax.fori_loop(2 * (nk // 2), nk, one, acc)
        return jnp.sum(acc, axis=0, keepdims=True)

    fold16 = lambda a: a.reshape(kt // PACKED_ROWS, PACKED_ROWS, tq)

    def count16(cand):
        c16 = jnp.broadcast_to(cand, (PACKED_ROWS, tq)).astype(I16)

        def one(j, acc):
            m = (fold16(half_scr[tile(j), :]) >= c16).astype(I16)
            for r in range(kt // PACKED_ROWS):
                acc = acc + m[r]
            return acc

        def four(jj, acc):
            for u in range(4):
                acc = one(4 * jj + u, acc)
            return acc

        acc = lax.fori_loop(0, nk // 4, four, jnp.zeros((PACKED_ROWS, tq), I16))
        acc = lax.fori_loop(4 * (nk // 4), nk, one, acc)
        return jnp.sum(acc.astype(I32), axis=0, keepdims=True)

    def select16(kth, n_init):
        def step(it, carry):
            lo, n_lo = carry
            cand = lo + jnp.left_shift(jnp.int32(1), HALF_BITS - 1 - it)
            cnt = count16(cand)
            take = cnt >= kth
            return jnp.where(take, cand, lo), jnp.where(take, cnt, n_lo)
        return lax.fori_loop(0, HALF_BITS, step, (jnp.full((1, tq), HALF_MIN, I32), n_init))

    t_hi, n_ge_hi = select16(topk, jnp.zeros((1, tq), I32))

    def split_lo(j, acc):
        t = keys_scr[tile(j), :]
        hi = t >> HALF_BITS
        lo_s = (t & (2 ** HALF_BITS - 1)) + HALF_MIN
        half_scr[tile(j), :] = jnp.where(hi == t_hi, lo_s, HALF_MIN).astype(I16)
        return acc + jnp.sum(fold((hi > t_hi).astype(I32)), axis=0)

    n_above = jnp.sum(lax.fori_loop(0, nk, split_lo, jnp.zeros((SUBLANES, tq), I32)), axis=0, keepdims=True)
    t_lo, n_eq_hi = select16(topk - n_above, n_ge_hi - n_above)
    thr = t_hi * 2 ** HALF_BITS + (t_lo - HALF_MIN)
    n_ge = n_above + n_eq_hi
    has_k = thr > INT_MIN
    n_gt = n_above + jnp.where(t_lo < -HALF_MIN - 1, count16(jnp.minimum(t_lo + 1, -HALF_MIN - 1)), 0)
    need = topk - n_gt
    surplus = jnp.logical_and(has_k, n_ge > topk)

    lmax_bits = max(1, (keys_scr.shape[0]).bit_length())

    def bis(it, jb):
        cand = jb + jnp.left_shift(jnp.int32(1), lmax_bits - 1 - it)
        cnt = count(lambda t, pos: jnp.logical_and(t == thr, pos < cand))
        return jnp.where(cnt <= need, cand, jb)

    any_surplus = jnp.max(surplus.astype(I32)) > 0
    jbound = lax.fori_loop(0, jnp.where(any_surplus, lmax_bits, 0), bis, jnp.zeros((1, tq), I32))

    def demote(j, carry):
        t = keys_scr[tile(j), :]
        drop = jnp.logical_and(surplus, jnp.logical_and(t == thr, j * kt + row_k >= jbound))
        keys_scr[tile(j), :] = jnp.where(drop, thr - 1, t)
        return carry

    lax.fori_loop(0, jnp.where(any_surplus, nk, 0), demote, 0)
    thr_sel = jnp.maximum(thr, INT_MIN + 1)

    half = lax.broadcasted_iota(I32, (LANES, 1), 0) // ATT_DIM
    for h in range(ATT_HEADS):
        pair = aqT_ref[(h // 2) * LANES:(h // 2 + 1) * LANES, :]
        qh_scr[h] = jnp.where(half == (h % 2), pair, jnp.zeros_like(pair))

    def logits(j, h):
        kh = kb_ref[tile(j), (h // 2) * LANES:(h // 2 + 1) * LANES]
        return jnp.dot(kh, qh_scr[h], preferred_element_type=F32)

    acc_scr[...] = jnp.zeros_like(acc_scr)
    heads = range(ATT_HEADS)
    ones_rows = jnp.ones((PACKED_ROWS, kt), MXU_DTYPE)

    def qk_and_numerators(j_next, lg_next, lg_cur, p_cur, m_new):
        tmax = []
        if lg_next is not None:
            jc = jnp.minimum(j_next, nk - 1)
            bias = jnp.where(jnp.logical_and(j_next < nk, keys_scr[tile(jc), :] >= thr_sel), 0.0, NEG_BIG)
        for h in heads:
            if lg_next is not None:
                x = logits(jc, h) + bias
                lg_next[h] = x
                tmax.append(jnp.max(jnp.max(fold(x), axis=0), axis=0, keepdims=True))
            if lg_cur is not None:
                p_cur[h] = jnp.exp2(lg_cur[h] - m_new[h]).astype(p_cur.dtype)
        return tuple(tmax)

    def half_step(j, lg_cur, p_cur, lg_next, ms, ls, tmax):
        m_new = [jnp.maximum(ms[h], tmax[h]) for h in heads]
        alpha = [jnp.exp2(ms[h] - m_new[h]) for h in heads]
        tmax_next = qk_and_numerators(j + 1, lg_next, lg_cur, p_cur, m_new)
        l_new = []
        for h in heads:
            rows = slice(h * ATT_DIM, (h + 1) * ATT_DIM)
            pv = jnp.dot(jnp.concatenate([vT_ref[j, rows, :], ones_rows], axis=0), p_cur[h],
                         preferred_element_type=F32)
            acc_scr[rows, :] = acc_scr[rows, :] * alpha[h] + pv[0:ATT_DIM, :]
            l_new.append(alpha[h] * ls[h] + pv[ATT_DIM:ATT_DIM + SUBLANES, :])
        return tuple(m_new), tuple(l_new), tmax_next

    def attend_pair(jj, carry):
        ms, ls, tmax = carry
        ms, ls, tmax = half_step(2 * jj, lg_scr, p_scr, lg2_scr, ms, ls, tmax)
        return half_step(2 * jj + 1, lg2_scr, p2_scr, lg_scr, ms, ls, tmax)

    def attend_last(j, carry):
        ms, ls, tmax = carry
        ms, ls, _ = half_step(nk - 1, lg_scr, p_scr, None, ms, ls, tmax)
        return ms, ls, tmax

    init = (tuple(jnp.full((1, tq), NEG_BIG, F32) for _ in heads),
            tuple(jnp.zeros((SUBLANES, tq), F32) for _ in heads),
            qk_and_numerators(0, lg_scr, None, None, None))
    carry = lax.fori_loop(0, nk // 2, attend_pair, init)
    _, ls, _ = lax.fori_loop(0, nk % 2, attend_last, carry)
    for h in heads:
        rows = slice(h * ATT_DIM, (h + 1) * ATT_DIM)
        acc_scr[rows, :] = acc_scr[rows, :] / ls[h][0:1, :]
    o_ref[...] = acc_scr[...].T.astype(o_ref.dtype)


def _dsa(kib, iqT, iwT, kb, aqT, vT, batch, n_q, l_pad, past, topk, tq, kt):
    assert kt % tq == 0 and past % tq == 0 and tq % CHUNK == 0 and n_q % tq == 0 and l_pad % kt == 0
    nq = n_q // tq
    qcol = lambda r: pl.BlockSpec((r, tq), lambda b, i: (0, b * nq + i))
    return pl.pallas_call(
        functools.partial(_dsa_kernel, past=past, topk=topk, tq=tq, kt=kt),
        grid=(batch, nq),
        in_specs=[pl.BlockSpec((None, l_pad, IDX_DIM), lambda b, i: (b, 0, 0)),
                  qcol(IDX_HEADS * IDX_DIM), qcol(SUBLANES),
                  pl.BlockSpec((None, l_pad, ATT_WIDTH), lambda b, i: (b, 0, 0)),
                  qcol(ATT_WIDTH),
                  pl.BlockSpec((l_pad // kt, ATT_WIDTH, kt), lambda b, i: (b, 0, 0))],
        out_specs=pl.BlockSpec((tq, ATT_WIDTH), lambda b, i: (b * nq + i, 0)),
        out_shape=jax.ShapeDtypeStruct((batch * n_q, ATT_WIDTH), MXU_DTYPE),
        scratch_shapes=[pltpu.VMEM((l_pad, tq), I32), pltpu.VMEM((ATT_WIDTH, tq), F32),
                        pltpu.VMEM((ATT_HEADS, LANES, tq), MXU_DTYPE),
                        pltpu.VMEM((ATT_HEADS, kt, tq), MXU_DTYPE),
                        pltpu.VMEM((ATT_HEADS, kt, tq), F32),
                        pltpu.VMEM((l_pad, tq), jnp.int16),
                        pltpu.VMEM((ATT_HEADS, kt, tq), MXU_DTYPE),
                        pltpu.VMEM((ATT_HEADS, kt, tq), F32)],
        compiler_params=_params("parallel", "arbitrary"),
        name="dsa",
    )(kib, iqT, iwT, kb, aqT, vT)


def _out_proj_kernel(yh_ref, oa_ref, sgh_ref, sga_ref, x_ref, wbh_ref, wba_ref, wo_ref, g_ref, b_ref,
                     wr_ref, br_ref, x1_ref, comb_ref):
    tm = x_ref.shape[0]
    parts = [slice(p * (tm // OUT_PROJ_PARTS), (p + 1) * (tm // OUT_PROJ_PARTS)) for p in range(OUT_PROJ_PARTS)]
    br_hg = [jnp.dot(yh_ref[r, :], wbh_ref[...], preferred_element_type=F32) for r in parts]
    br_att = [jnp.dot(oa_ref[r, :], wba_ref[...], preferred_element_type=F32) for r in parts]
    merged = [sgh_ref[r, :] * bh + sga_ref[r, :] * ba for r, bh, ba in zip(parts, br_hg, br_att)]
    out = [_mm(m, wo_ref[...]) for m in merged]
    x1 = [_layer_norm(DN_ALPHA * x_ref[r, :] + o, g_ref[...], b_ref[...]) for r, o in zip(parts, out)]
    for r, v in zip(parts, x1):
        x1_ref[r, :] = v
    logits = [_mm(v, wr_ref[...]) + br_ref[...] for v in x1]
    for r, lg in zip(parts, logits):
        comb_ref[r, :] = _route(lg)


def _route(lg):
    lane = lax.broadcasted_iota(I32, lg.shape, 1).astype(F32)
    ninf = -jnp.inf
    gmask = lane < N_GROUPS
    gl = jnp.where(gmask, lg, ninf)
    gmax = jnp.max(gl, axis=1, keepdims=True)
    gsel = jnp.min(jnp.where(gl == gmax, lane, float(ROUTER_LANES)), axis=1, keepdims=True)
    g_w = 1.0 / jnp.sum(jnp.where(gmask, jnp.exp(gl - gmax), 0.0), axis=1, keepdims=True)
    e0 = EXPERT_LANE0 + EXPERTS_PER_GROUP * gsel
    emask = jnp.logical_and(lane >= e0, lane < e0 + EXPERTS_PER_GROUP)
    el = jnp.where(emask, lg, ninf)
    emax = jnp.max(el, axis=1, keepdims=True)
    ee = jnp.where(emask, jnp.exp(el - emax), 0.0)
    prob = ee / jnp.sum(ee, axis=1, keepdims=True)
    pm = jnp.where(emask, prob, -1.0)
    p1 = jnp.max(pm, axis=1, keepdims=True)
    i1 = jnp.min(jnp.where(pm == p1, lane, float(ROUTER_LANES)), axis=1, keepdims=True)
    pm2 = jnp.where(lane == i1, -1.0, pm)
    p2 = jnp.max(pm2, axis=1, keepdims=True)
    i2 = jnp.min(jnp.where(pm2 == p2, lane, float(ROUTER_LANES)), axis=1, keepdims=True)
    tot = p1 + p2
    return (jnp.where(lane == i1, g_w * (p1 / tot), 0.0)
            + jnp.where(lane == i2, g_w * (p2 / tot), 0.0))


def _out_proj(yh, oa, sgh, sga, x, lw, tm):
    n = x.shape[0]
    tm = min(tm, n)
    full = lambda a: pl.BlockSpec(a.shape, lambda i: (0,) * a.ndim)
    rows = lambda c: pl.BlockSpec((tm, c), lambda i: (i, 0))
    weights = (lw["w_br_hg"], lw["w_br_att"], lw["w_out"], lw["ln1_g"], lw["ln1_b"], lw["w_r"], lw["b_r"])
    return pl.pallas_call(
        _out_proj_kernel,
        grid=(n // tm,),
        in_specs=[rows(HG_WIDTH), rows(ATT_WIDTH), rows(D_MODEL), rows(D_MODEL), rows(D_MODEL)]
        + [full(w) for w in weights],
        out_specs=(rows(D_MODEL), rows(ROUTER_LANES)),
        out_shape=(jax.ShapeDtypeStruct((n, D_MODEL), F32), jax.ShapeDtypeStruct((n, ROUTER_LANES), F32)),
        compiler_params=_params("parallel"),
        name="out_proj",
    )(yh, oa, sgh, sga, x, *weights)


def _moe_kernel(x_ref, comb_ref, wgu_ref, wd_ref, g_ref, b_ref, o_ref, xb_scr, hb_scr, acc_scr):
    grp = pl.program_id(1)

    @pl.when(grp == 0)
    def _():
        xb_scr[...] = x_ref[...].astype(xb_scr.dtype)
        acc_scr[...] = jnp.zeros_like(acc_scr)

    xb = xb_scr[...]
    comb = comb_ref[...]
    lane = lax.broadcasted_iota(I32, comb.shape, 1)
    F = D_EXPERT
    for e in range(EXPERTS_PER_GROUP):
        gu = jnp.dot(xb, wgu_ref[e], preferred_element_type=F32)
        h = _silu(gu[:, 0:F]) * gu[:, F:2 * F]
        c = jnp.sum(jnp.where(lane == EXPERT_LANE0 + grp * EXPERTS_PER_GROUP + e, comb, 0.0),
                    axis=1, keepdims=True)
        hb_scr[:, e * F:(e + 1) * F] = (h * c).astype(hb_scr.dtype)
    acc_scr[...] += jnp.dot(hb_scr[...], wd_ref[...], preferred_element_type=F32)

    @pl.when(grp == pl.num_programs(1) - 1)
    def _():
        o_ref[...] = _layer_norm(DN_ALPHA * x_ref[...] + acc_scr[...], g_ref[...], b_ref[...])


def _moe(x1, comb, lw, tm):
    n = x1.shape[0]
    tm = min(tm, n)
    E, F = EXPERTS_PER_GROUP, D_EXPERT
    return pl.pallas_call(
        _moe_kernel,
        grid=(n // tm, N_GROUPS),
        in_specs=[pl.BlockSpec((tm, D_MODEL), lambda i, g: (i, 0)),
                  pl.BlockSpec((tm, ROUTER_LANES), lambda i, g: (i, 0)),
                  pl.BlockSpec((None, E, D_MODEL, 2 * F), lambda i, g: (g, 0, 0, 0)),
                  pl.BlockSpec((None, E * F, D_MODEL), lambda i, g: (g, 0, 0)),
                  pl.BlockSpec((1, D_MODEL), lambda i, g: (0, 0)),
                  pl.BlockSpec((1, D_MODEL), lambda i, g: (0, 0))],
        out_specs=pl.BlockSpec((tm, D_MODEL), lambda i, g: (i, 0)),
        out_shape=jax.ShapeDtypeStruct((n, D_MODEL), F32),
        scratch_shapes=[pltpu.VMEM((tm, D_MODEL), MXU_DTYPE), pltpu.VMEM((tm, E * F), MXU_DTYPE),
                        pltpu.VMEM((tm, D_MODEL), F32)],
        compiler_params=_params("parallel", "arbitrary"),
        name="moe",
    )(x1, comb, lw["w_gu"], lw["w_d"], lw["ln2_g"], lw["ln2_b"])


def _layer_weights(l, lbs, w_in, hg_norm_w, w_br_hg, w_br_att, w_out, ln1_g, ln1_b, ln2_g, ln2_b,
                   w_rg, b_rg, w_re, b_re, w_gate, w_up, w_down):
    md = MXU_DTYPE
    offs = [0]
    for s in IN_SIZES:
        offs.append(offs[-1] + s)
    w = w_in[l]
    seg = lambda a, b: w[:, offs[a]:offs[b]]
    idx_cols = jnp.concatenate([seg(8, 9), jnp.zeros((D_MODEL, LANES - IDX_DIM), F32)], axis=1)
    iw_rows = jnp.concatenate([seg(9, 10).T, jnp.zeros((SUBLANES - IDX_HEADS, D_MODEL), F32)], axis=0)
    lb = lbs[l]
    lbp = jnp.concatenate([jnp.log(lb)[None], jnp.log1p(-lb)[None], (1.0 - lb)[None],
                           jnp.zeros((SUBLANES - 3, HG_WIDTH), F32)], axis=0)
    w_r = jnp.concatenate([w_rg[l], w_re[l], jnp.zeros((D_MODEL, ROUTER_LANES - N_GROUPS - N_EXPERTS), F32)], axis=1)
    b_r = jnp.concatenate([b_rg[l], b_re[l], jnp.zeros((ROUTER_LANES - N_GROUPS - N_EXPERTS,), F32)])[None]
    return {
        "w_h": seg(0, 4).astype(md), "w_a": seg(5, 7).astype(md), "w_i": idx_cols.astype(md),
        "w_g": seg(10, 12).astype(md),
        "w_aqT": seg(4, 5).T.astype(md), "w_iqT": seg(7, 8).T.astype(md),
        "w_iwT": iw_rows.astype(md), "lbp": lbp,
        "norm_w": hg_norm_w[l],
        "w_br_hg": w_br_hg[l].astype(md), "w_br_att": w_br_att[l].astype(md), "w_out": w_out[l].astype(md),
        "ln1_g": ln1_g[l][None], "ln1_b": ln1_b[l][None], "ln2_g": ln2_g[l][None], "ln2_b": ln2_b[l][None],
        "w_r": w_r.astype(md), "b_r": b_r,
        "w_gu": jnp.concatenate([w_gate[l], w_up[l]], axis=-1).astype(md),
        "w_d": w_down[l].reshape(N_GROUPS, EXPERTS_PER_GROUP * D_EXPERT, D_MODEL).astype(md),
    }


def _lower_bounds(lb_logits):
    p = jax.nn.softmax(lb_logits.astype(F32), axis=0)
    return jnp.concatenate([jnp.zeros_like(p[:1]), jnp.cumsum(p[1:], axis=0)], axis=0)


TM_IN = 256
TM_OUT = 512
OUT_PROJ_PARTS = 2
TM_MOE = 1024
HGRN_CHUNKS_PER_STEP = 16
HGRN_CHUNKS_PER_TRIP = 8
DSA_TQ_PROMPT = 256
DSA_TQ_SAMPLE = 128
DSA_KT = 256
SCORE_TILES_PER_TRIP = 3


def _mixer_and_ffn(x, lw, batch, seq, s0, past, layer, bufs):
    tq, kt = (DSA_TQ_PROMPT if past is None else DSA_TQ_SAMPLE), DSA_KT
    (qh, lf, kk, vh, og, k_buf, v_buf, kb, ik_buf, kib, aqT, vT, iqT, iwT, sgh, sga) = _in_proj(
        x, lw, TM_IN, kt, layer, bufs)
    y_hg, s_new = _hgrn(qh, lf, kk, vh, og, lw["norm_w"], s0, batch, seq, HGRN_CHUNKS_PER_STEP)
    if past is None:
        l_tot = seq
        l_pad = -(-l_tot // kt) * kt
        assert l_pad == l_tot and seq % tq == 0
        o_att = _dsa(kib.reshape(batch, seq, IDX_DIM), iqT, iwT, kb.reshape(batch, seq, ATT_WIDTH), aqT, vT,
                     batch, seq, l_pad, 0, min(TOPK_MAX, l_tot // 4), tq, kt)
    else:
        k_past, v_past, ki_past = past
        p_len = k_past.shape[1]
        l_tot = p_len + seq
        n_q = -(-seq // tq) * tq
        l_pad = -(-(p_len + n_q) // kt) * kt
        padk = lambda a, new: jnp.concatenate(
            [a, new, jnp.zeros((batch, l_pad - l_tot, a.shape[2]), a.dtype)], axis=1)
        k_all = padk(k_past, kb.reshape(batch, seq, ATT_WIDTH))
        ki_all = padk(ki_past, kib.reshape(batch, seq, IDX_DIM))
        v_all = padk(v_past, v_buf[layer].astype(v_past.dtype).reshape(batch, seq, ATT_WIDTH))
        vT_all = jnp.transpose(v_all.reshape(batch * (l_pad // kt), kt, ATT_WIDTH), (0, 2, 1))
        padq = lambda a: jnp.pad(a.reshape(a.shape[0], batch, seq), ((0, 0), (0, 0), (0, n_q - seq))
                                 ).reshape(a.shape[0], batch * n_q)
        o_pad = _dsa(ki_all, padq(iqT), padq(iwT), k_all, padq(aqT), vT_all,
                     batch, n_q, l_pad, p_len, min(TOPK_MAX, l_tot // 4), tq, kt)
        o_att = o_pad.reshape(batch, n_q, ATT_WIDTH)[:, :seq].reshape(batch * seq, ATT_WIDTH)
    x1, comb = _out_proj(y_hg, o_att, sgh, sga, x, lw, TM_OUT)
    x2 = _moe(x1, comb, lw, TM_MOE)
    return x2, (k_buf, v_buf, ik_buf), s_new


def kernel(x_prompt, x_sample, cache_k, cache_v, cache_idx_k, state_hgrn, w_in, hg_lb_logits, hg_norm_w,
           w_br_hg, w_br_att, w_out, ln1_g, ln1_b, ln2_g, ln2_b, w_rg, b_rg, w_re, b_re, w_gate, w_up, w_down):
    bp, tp, d = x_prompt.shape
    bs, ts, _ = x_sample.shape
    p_len = cache_k.shape[2]
    lbs = _lower_bounds(hg_lb_logits)
    xp = x_prompt.reshape(bp * tp, d)
    xs = x_sample.reshape(bs * ts, d)
    zeros_state = jnp.zeros((bp, HG_HEADS, HG_DK, HG_DV), F32)
    bufs_p, bufs_s, st_p, st_s = None, None, [], []
    for l in range(DEPTH):
        lw = _layer_weights(l, lbs, w_in, hg_norm_w, w_br_hg, w_br_att, w_out, ln1_g, ln1_b, ln2_g, ln2_b,
                            w_rg, b_rg, w_re, b_re, w_gate, w_up, w_down)
        xp, bufs_p, sp = _mixer_and_ffn(xp, lw, bp, tp, zeros_state, None, l, bufs_p)
        past = (cache_k[l].reshape(bs, p_len, ATT_WIDTH).astype(MXU_DTYPE),
                cache_v[l].reshape(bs, p_len, ATT_WIDTH).astype(MXU_DTYPE),
                cache_idx_k[l].astype(MXU_DTYPE))
        xs, bufs_s, ss = _mixer_and_ffn(xs, lw, bs, ts, state_hgrn[l].astype(F32), past, l, bufs_s)
        st_p.append(sp)
        st_s.append(ss)

    def shaped(bufs, b, t):
        k, v, ik = bufs
        return (k.reshape(DEPTH, b, t, ATT_HEADS, ATT_DIM), v.reshape(DEPTH, b, t, ATT_HEADS, ATT_DIM),
                ik.reshape(DEPTH, b, t, IDX_DIM))

    kp, vp, ikp = shaped(bufs_p, bp, tp)
    ks, vs, iks = shaped(bufs_s, bs, ts)
    return (xp.reshape(bp, tp, d), xs.reshape(bs, ts, d), kp, vp, ikp, jnp.stack(st_p), ks, vs, iks,
            jnp.stack(st_s).astype(state_hgrn.dtype))
```

```python
import functools

import jax
import jax.numpy as jnp
from jax import lax
from jax.experimental import pallas as pl
from jax.experimental.pallas import tpu as pltpu

F32 = jnp.float32
I32 = jnp.int32
MXU_DTYPE = jnp.bfloat16

D_MODEL = 1024
DEPTH = 4
CHUNK = 64
HG_HEADS = 4
HG_DK = 128
HG_DV = 128
HG_WIDTH = HG_HEADS * HG_DK
ATT_HEADS = 8
ATT_DIM = 64
ATT_WIDTH = ATT_HEADS * ATT_DIM
IDX_HEADS = 4
IDX_DIM = 64
TOPK_MAX = 256
ATT_SCALE = ATT_DIM ** -0.5
LOG2_E = 1.4426950408889634
IDX_SCALE = IDX_DIM ** -0.5
IDX_W_SCALE = IDX_HEADS ** -0.5
N_GROUPS = 4
EXPERTS_PER_GROUP = 4
N_EXPERTS = N_GROUPS * EXPERTS_PER_GROUP
D_EXPERT = 256
DN_ALPHA = (2 * DEPTH) ** 0.25
LN_EPS = 1e-5
RMS_EPS = 1e-6
IN_SIZES = (HG_WIDTH, HG_WIDTH, HG_HEADS * HG_DV, HG_HEADS * HG_DV,
            ATT_WIDTH, ATT_WIDTH, ATT_WIDTH, IDX_HEADS * IDX_DIM, IDX_DIM, IDX_HEADS,
            D_MODEL, D_MODEL)

LANES = 128
SUBLANES = 8
PACKED_ROWS = 16
I16 = jnp.int16
HALF_BITS, HALF_MIN = 16, -2 ** 15
SUBCHUNK = 16
SPAN_MAX = 60.0
VMEM_LIMIT = 56 * 1024 * 1024
INT_MIN = -2 ** 31
NEG_BIG = -1e30
ROUTER_LANES = LANES
EXPERT_LANE0 = N_GROUPS


def _params(*sem):
    return pltpu.CompilerParams(dimension_semantics=sem, vmem_limit_bytes=VMEM_LIMIT)


def _mm(a, b):
    return jnp.dot(a.astype(MXU_DTYPE), b.astype(MXU_DTYPE), preferred_element_type=F32)


def _mm_nt(a, b):
    return lax.dot_general(a.astype(MXU_DTYPE), b.astype(MXU_DTYPE), (((1,), (1,)), ((), ())),
                           preferred_element_type=F32)


def _sigmoid(x):
    return 1.0 / (1.0 + jnp.exp(-x))


def _silu(x):
    return x * _sigmoid(x)


def _layer_norm(r, g, b):
    mu = jnp.mean(r, axis=-1, keepdims=True)
    d = r - mu
    var = jnp.mean(d * d, axis=-1, keepdims=True)
    return d * lax.rsqrt(var + LN_EPS) * g + b


def _in_proj_kernel(x_ref, wh_ref, wa_ref, wi_ref, wg_ref, waqT_ref, wiqT_ref, wiwT_ref,
                    lbp_ref, *refs):
    (qh_ref, lf_ref, kk_ref, vh_ref, og_ref, k32_ref, v32_ref, kb_ref, ik32_ref, kib_ref,
     aqT_ref, vT_ref, iqT_ref, iwT_ref, sgh_ref, sga_ref) = refs[-16:]
    xb = x_ref[...].astype(MXU_DTYPE)
    W = HG_WIDTH
    qh_ref[...] = _silu(_mm(xb, wh_ref[:, 0:W]))
    z = _mm(xb, wh_ref[:, W:2 * W])
    log_lb = lbp_ref[0:1, :]
    log_1mlb = lbp_ref[1:2, :]
    one_mlb = lbp_ref[2:3, :]
    log_sig = jnp.minimum(z, 0.0) - jnp.log(1.0 + jnp.exp(-jnp.abs(z)))
    b = log_1mlb + log_sig
    lf_ref[...] = jnp.maximum(log_lb, b) + jnp.log(1.0 + jnp.exp(-jnp.abs(log_lb - b)))
    kk_ref[...] = one_mlb * _sigmoid(-z)
    vh_ref[...] = _mm(xb, wh_ref[:, 2 * W:3 * W])
    og_ref[...] = _silu(_mm(xb, wh_ref[:, 3 * W:4 * W]))
    A = ATT_WIDTH
    k = _mm(xb, wa_ref[:, 0:A])
    v = _mm(xb, wa_ref[:, A:2 * A])
    kb_ref[...] = k.astype(kb_ref.dtype)
    n_tok = k.shape[0]
    for h in range(ATT_HEADS):
        k32_ref[pl.ds(h, n_tok, stride=ATT_HEADS), :] = k[:, h * ATT_DIM:(h + 1) * ATT_DIM]
        v32_ref[pl.ds(h, n_tok, stride=ATT_HEADS), :] = v[:, h * ATT_DIM:(h + 1) * ATT_DIM]
    aqT_ref[...] = (_mm_nt(waqT_ref[...], xb) * (ATT_SCALE * LOG2_E)).astype(aqT_ref.dtype)
    kt = vT_ref.shape[-1]
    for t in range(vT_ref.shape[0]):
        vT_ref[t] = v[t * kt:(t + 1) * kt, :].T.astype(vT_ref.dtype)
    ik = _mm(xb, wi_ref[...])[:, 0:IDX_DIM]
    ik32_ref[...] = ik
    kib_ref[...] = ik.astype(kib_ref.dtype)
    iqT_ref[...] = _mm_nt(wiqT_ref[...], xb).astype(iqT_ref.dtype)
    iwT_ref[...] = _mm_nt(wiwT_ref[...], xb) * (IDX_SCALE * IDX_W_SCALE)
    D = D_MODEL
    sgh_ref[...] = _sigmoid(_mm(xb, wg_ref[:, 0:D]))
    sga_ref[...] = _sigmoid(_mm(xb, wg_ref[:, D:2 * D]))


def _in_proj(x, lw, tm, kt, layer, bufs):
    n = x.shape[0]
    tm = min(tm, n)
    assert tm % kt == 0 and n % tm == 0
    grid = (n // tm,)
    full = lambda a: pl.BlockSpec(a.shape, lambda i: (0,) * a.ndim)
    rows = lambda c: pl.BlockSpec((tm, c), lambda i: (i, 0))
    cols = lambda r: pl.BlockSpec((r, tm), lambda i: (0, i))
    lrows = lambda c: pl.BlockSpec((None, tm, c), lambda i: (layer, i, 0))
    weights = (lw["w_h"], lw["w_a"], lw["w_i"], lw["w_g"], lw["w_aqT"], lw["w_iqT"], lw["w_iwT"],
               lw["lbp"])
    out_shape = (
        jax.ShapeDtypeStruct((n, HG_WIDTH), F32),
        jax.ShapeDtypeStruct((n, HG_WIDTH), F32),
        jax.ShapeDtypeStruct((n, HG_WIDTH), F32),
        jax.ShapeDtypeStruct((n, HG_WIDTH), F32),
        jax.ShapeDtypeStruct((n, HG_WIDTH), F32),
        jax.ShapeDtypeStruct((DEPTH, n * ATT_HEADS, ATT_DIM), F32),
        jax.ShapeDtypeStruct((DEPTH, n * ATT_HEADS, ATT_DIM), F32),
        jax.ShapeDtypeStruct((n, ATT_WIDTH), MXU_DTYPE),
        jax.ShapeDtypeStruct((DEPTH, n, IDX_DIM), F32),
        jax.ShapeDtypeStruct((n, IDX_DIM), MXU_DTYPE),
        jax.ShapeDtypeStruct((ATT_WIDTH, n), MXU_DTYPE),
        jax.ShapeDtypeStruct((n // kt, ATT_WIDTH, kt), MXU_DTYPE),
        jax.ShapeDtypeStruct((IDX_HEADS * IDX_DIM, n), MXU_DTYPE),
        jax.ShapeDtypeStruct((SUBLANES, n), F32),
        jax.ShapeDtypeStruct((n, D_MODEL), F32),
        jax.ShapeDtypeStruct((n, D_MODEL), F32),
    )
    hrows = pl.BlockSpec((None, tm * ATT_HEADS, ATT_DIM), lambda i: (layer, i, 0))
    out_specs = (rows(HG_WIDTH),) * 5 + (hrows, hrows, rows(ATT_WIDTH),
                                         lrows(IDX_DIM), rows(IDX_DIM)) + (
        cols(ATT_WIDTH), pl.BlockSpec((tm // kt, ATT_WIDTH, kt), lambda i: (i, 0, 0)),
        cols(IDX_HEADS * IDX_DIM), cols(SUBLANES), rows(D_MODEL), rows(D_MODEL))
    in_specs = [rows(D_MODEL)] + [full(w) for w in weights]
    aliases = {}
    if bufs is not None:
        first = len(in_specs)
        in_specs += [pl.BlockSpec(memory_space=pl.ANY)] * len(bufs)
        aliases = {first: 5, first + 1: 6, first + 2: 8}
    return pl.pallas_call(
        _in_proj_kernel,
        grid=grid,
        in_specs=in_specs,
        out_specs=out_specs,
        out_shape=out_shape,
        input_output_aliases=aliases,
        compiler_params=_params("parallel"),
        name="in_proj",
    )(x, *weights, *(bufs or ()))


def _hgrn_kernel(q_ref, f_ref, k_ref, v_ref, og_ref, nw_ref, s0_ref, y_ref, s_ref, st_scr, *, n_chunks,
                 unroll):
    g = pl.program_id(1)

    @pl.when(g == 0)
    def _():
        for h in range(HG_HEADS):
            st_scr[h] = s0_ref[h].T

    C, SC, R8 = CHUNK, SUBCHUNK, SUBLANES
    row = lax.broadcasted_iota(I32, (C, C), 0)
    col = lax.broadcasted_iota(I32, (C, C), 1)
    tri = (row >= col).astype(F32)
    row_c = lax.broadcasted_iota(I32, (C, 1), 0)
    row_8 = lax.broadcasted_iota(I32, (R8, 1), 0)
    row_sc = lax.broadcasted_iota(I32, (SC, 1), 0)
    lane_c = lax.broadcasted_iota(I32, (1, C), 1)
    ones = jnp.ones((HG_DK, LANES), MXU_DTYPE)
    nw = nw_ref[...]

    heads = range(HG_HEADS)
    hcols = [slice(h * HG_DK, (h + 1) * HG_DK) for h in heads]

    n_units = unroll * HG_HEADS

    def chunk(c, carry):
        units = range(n_units)
        sls = [pl.ds(pl.multiple_of((c * unroll + n // HG_HEADS) * C, C), C) for n in units]
        hcs = [hcols[n % HG_HEADS] for n in units]
        q = [q_ref[sls[n], hcs[n]] for n in units]
        k = [k_ref[sls[n], hcs[n]] for n in units]
        v = [v_ref[sls[n], hcs[n]] for n in units]
        bc = [jnp.dot(tri, f_ref[sls[n], hcs[n]], precision=lax.Precision.HIGHEST, preferred_element_type=F32)
              for n in units]
        bt = [b[C - 1:C, :] for b in bc]
        upd = [_mm(v[n].T, k[n] * jnp.exp(bt[n] - bc[n])) for n in units]
        o = [None] * n_units
        for n in units:
            h = n % HG_HEADS
            o[n] = _mm_nt(q[n] * jnp.exp(bc[n]), st_scr[h])
            st_scr[h] = st_scr[h] * jnp.exp(bt[n]) + upd[n]
        heads = units
        def anchored(h, i, own_rows):
            r0 = i * SC
            anchor = bc[h][r0:r0 + 1, :]
            qd = q[h][r0:r0 + SC, :] * jnp.exp(bc[h][r0:r0 + SC, :] - anchor)
            last = r0 + SC if own_rows else r0
            cap = SPAN_MAX if own_rows else 0.0
            kd_i = jnp.where(row_c < last, k[h] * jnp.exp(jnp.minimum(anchor - bc[h], cap)), 0.0)
            return _mm_nt(qd, kd_i)

        def intra_factored():
            out = []
            for h in heads:
                blocks = [jnp.where(lane_c <= i * SC + row_sc, anchored(h, i, True), 0.0) for i in range(C // SC)]
                out.append(jnp.concatenate(blocks, axis=0))
            return tuple(out)

        def intra_exact():
            a_off = [[None] * n_units for _ in range(C // SC)]
            for i in range(1, C // SC):
                for h in heads:
                    a_off[i][h] = anchored(h, i, False)
            red = []
            for h in heads:
                prods = []
                for i in range(C // SC):
                    r0 = i * SC
                    q_i, k_i, bc_i = q[h][r0:r0 + SC, :], k[h][r0:r0 + SC, :], bc[h][r0:r0 + SC, :]
                    for s in range(SC):
                        lo = (s // R8) * R8
                        k_s = k_i[s:s + 1, :]
                        b_s = bc_i[s:s + 1, :]
                        e = jnp.where(row_8 + lo >= s, jnp.exp(bc_i[lo:lo + R8, :] - b_s), 0.0)
                        prods.append(q_i[lo:lo + R8, :] * e * k_s)
                        for r in range(lo + R8, SC, R8):
                            prods.append(q_i[r:r + R8, :] * jnp.exp(bc_i[r:r + R8, :] - b_s) * k_s)
                red.append(_mm(jnp.concatenate(prods, axis=0), ones))
            out = []
            for h in heads:
                blocks = []
                off = 0
                for i in range(C // SC):
                    r0 = i * SC
                    a_i = jnp.zeros((SC, C), F32) if i == 0 else a_off[i][h]
                    groups = [a_i[r:r + R8, :] for r in range(0, SC, R8)]
                    for s in range(SC):
                        for r in range((s // R8) * R8, SC, R8):
                            groups[r // R8] = jnp.where(lane_c == r0 + s, red[h][off:off + R8, 0:C],
                                                        groups[r // R8])
                            off += R8
                    blocks.extend(groups)
                out.append(jnp.concatenate(blocks, axis=0))
            return tuple(out)

        span = None
        for h in heads:
            for i in range(C // SC):
                d = bc[h][i * SC:i * SC + 1, :] - bc[h][(i + 1) * SC - 1:(i + 1) * SC, :]
                span = d if span is None else jnp.maximum(span, d)
        a = lax.cond(jnp.max(span) <= SPAN_MAX, intra_factored, intra_exact)
        for h in heads:
            o[h] = o[h] + _mm(a[h], v[h])
        for h in heads:
            on = o[h] * lax.rsqrt(jnp.mean(o[h] * o[h], axis=-1, keepdims=True) + RMS_EPS) * nw
            y_ref[sls[h], hcs[h]] = (on * og_ref[sls[h], hcs[h]]).astype(y_ref.dtype)
        return carry

    lax.fori_loop(0, n_chunks // unroll, chunk, 0)

    @pl.when(g == pl.num_programs(1) - 1)
    def _():
        for h in range(HG_HEADS):
            s_ref[h] = st_scr[h].T


def _hgrn(qh, lf, kk, vh, og, norm_w, s0, batch, seq, chunks_per_step):
    n_chunks = seq // CHUNK
    g_sz = min(chunks_per_step, n_chunks)
    steps = n_chunks // g_sz
    tb = g_sz * CHUNK
    r3 = lambda a: a.reshape(batch, seq, HG_WIDTH)
    tok = pl.BlockSpec((None, tb, HG_WIDTH), lambda b, g: (b, g, 0))
    st = pl.BlockSpec((None, HG_HEADS, HG_DK, HG_DV), lambda b, g: (b, 0, 0, 0))
    y, s = pl.pallas_call(
        functools.partial(_hgrn_kernel, n_chunks=g_sz, unroll=HGRN_CHUNKS_PER_TRIP if g_sz % HGRN_CHUNKS_PER_TRIP == 0 else 1),
        grid=(batch, steps),
        in_specs=[tok, tok, tok, tok, tok, pl.BlockSpec((1, HG_DV), lambda b, g: (0, 0)), st],
        out_specs=(tok, st),
        out_shape=(jax.ShapeDtypeStruct((batch, seq, HG_WIDTH), MXU_DTYPE),
                   jax.ShapeDtypeStruct((batch, HG_HEADS, HG_DK, HG_DV), F32)),
        scratch_shapes=[pltpu.VMEM((HG_HEADS, HG_DV, HG_DK), F32)],
        compiler_params=_params("parallel", "arbitrary"),
        name="hgrn",
    )(r3(qh), r3(lf), r3(kk), r3(vh), r3(og), norm_w.reshape(1, HG_DV), s0)
    return y.reshape(batch * seq, HG_WIDTH), s


def _dsa_kernel(kib_ref, iqT_ref, iwT_ref, kb_ref, aqT_ref, vT_ref, o_ref, keys_scr, acc_scr, qh_scr, p_scr,
                lg_scr, half_scr, p2_scr, lg2_scr,
                *, past, topk, tq, kt):
    i = pl.program_id(1)
    qpos0 = past + i * tq
    nk = (qpos0 + tq + kt - 1) // kt
    lane_q = lax.broadcasted_iota(I32, (1, tq), 1)
    qchunk = (qpos0 + lane_q) // CHUNK
    row_k = lax.broadcasted_iota(I32, (kt, 1), 0)
    tile = lambda j: pl.ds(pl.multiple_of(j * kt, kt), kt)
    fold = lambda a: a.reshape(kt // SUBLANES, SUBLANES, tq)

    def score_tiles(js, masked):
        raw = [[jnp.dot(kib_ref[tile(j), :], iqT_ref[h * IDX_DIM:(h + 1) * IDX_DIM, :],
                        preferred_element_type=F32) for h in range(IDX_HEADS)] for j in js]
        for j, raw_j in zip(js, raw):
            sc = jnp.zeros((kt, tq), F32)
            for h in range(IDX_HEADS):
                sc = sc + jnp.maximum(raw_j[h], 0.0) * iwT_ref[h:h + 1, :]
            bits = lax.bitcast_convert_type(sc, I32)
            key = jnp.where(bits < 0, INT_MIN - bits, bits)
            if masked:
                kchunk = (j * kt + row_k) // CHUNK
                key = jnp.where(kchunk <= qchunk, key, INT_MIN)
            keys_scr[tile(j), :] = key
            half_scr[tile(j), :] = (key >> HALF_BITS).astype(I16)

    def score_pair(jj, carry):
        score_tiles([SCORE_TILES_PER_TRIP * jj + u for u in range(SCORE_TILES_PER_TRIP)], False)
        return carry

    def score_one(j, carry):
        score_tiles([j], False)
        return carry

    lax.fori_loop(0, (nk - 1) // SCORE_TILES_PER_TRIP, score_pair, 0)
    lax.fori_loop(SCORE_TILES_PER_TRIP * ((nk - 1) // SCORE_TILES_PER_TRIP), nk - 1, score_one, 0)
    score_tiles([nk - 1], True)

    def count(pred):
        def one(j, acc):
            m = pred(keys_scr[tile(j), :], j * kt + row_k).astype(I32)
            return acc + jnp.sum(fold(m), axis=0)

        def two(jj, acc):
            return one(2 * jj + 1, one(2 * jj, acc))

        acc = lax.fori_loop(0, nk // 2, two, jnp.zeros((SUBLANES, tq), I32))
        acc = lax.fori_loop(2 * (nk // 2), nk, one, acc)
        return jnp.sum(acc, axis=0, keepdims=True)

    fold16 = lambda a: a.reshape(kt // PACKED_ROWS, PACKED_ROWS, tq)

    def count16(cand):
        c16 = jnp.broadcast_to(cand, (PACKED_ROWS, tq)).astype(I16)

        def one(j, acc):
            m = (fold16(half_scr[tile(j), :]) >= c16).astype(I16)
            for r in range(kt // PACKED_ROWS):
                acc = acc + m[r]
            return acc

        def four(jj, acc):
            for u in range(4):
                acc = one(4 * jj + u, acc)
            return acc

        acc = lax.fori_loop(0, nk // 4, four, jnp.zeros((PACKED_ROWS, tq), I16))
        acc = lax.fori_loop(4 * (nk // 4), nk, one, acc)
        return jnp.sum(acc.astype(I32), axis=0, keepdims=True)

    def select16(kth, n_init):
        def step(it, carry):
            lo, n_lo = carry
            cand = lo + jnp.left_shift(jnp.int32(1), HALF_BITS - 1 - it)
            cnt = count16(cand)
            take = cnt >= kth
            return jnp.where(take, cand, lo), jnp.where(take, cnt, n_lo)
        return lax.fori_loop(0, HALF_BITS, step, (jnp.full((1, tq), HALF_MIN, I32), n_init))

    t_hi, n_ge_hi = select16(topk, jnp.zeros((1, tq), I32))

    def split_lo(j, acc):
        t = keys_scr[tile(j), :]
        hi = t >> HALF_BITS
        lo_s = (t & (2 ** HALF_BITS - 1)) + HALF_MIN
        half_scr[tile(j), :] = jnp.where(hi == t_hi, lo_s, HALF_MIN).astype(I16)
        return acc + jnp.sum(fold((hi > t_hi).astype(I32)), axis=0)

    n_above = jnp.sum(lax.fori_loop(0, nk, split_lo, jnp.zeros((SUBLANES, tq), I32)), axis=0, keepdims=True)
    t_lo, n_eq_hi = select16(topk - n_above, n_ge_hi - n_above)
    thr = t_hi * 2 ** HALF_BITS + (t_lo - HALF_MIN)
    n_ge = n_above + n_eq_hi
    has_k = thr > INT_MIN
    n_gt = n_above + jnp.where(t_lo < -HALF_MIN - 1, count16(jnp.minimum(t_lo + 1, -HALF_MIN - 1)), 0)
    need = topk - n_gt
    surplus = jnp.logical_and(has_k, n_ge > topk)

    lmax_bits = max(1, (keys_scr.shape[0]).bit_length())

    def bis(it, jb):
        cand = jb + jnp.left_shift(jnp.int32(1), lmax_bits - 1 - it)
        cnt = count(lambda t, pos: jnp.logical_and(t == thr, pos < cand))
        return jnp.where(cnt <= need, cand, jb)

    any_surplus = jnp.max(surplus.astype(I32)) > 0
    jbound = lax.fori_loop(0, jnp.where(any_surplus, lmax_bits, 0), bis, jnp.zeros((1, tq), I32))

    def demote(j, carry):
        t = keys_scr[tile(j), :]
        drop = jnp.logical_and(surplus, jnp.logical_and(t == thr, j * kt + row_k >= jbound))
        keys_scr[tile(j), :] = jnp.where(drop, thr - 1, t)
        return carry

    lax.fori_loop(0, jnp.where(any_surplus, nk, 0), demote, 0)
    thr_sel = jnp.maximum(thr, INT_MIN + 1)

    half = lax.broadcasted_iota(I32, (LANES, 1), 0) // ATT_DIM
    for h in range(ATT_HEADS):
        pair = aqT_ref[(h // 2) * LANES:(h // 2 + 1) * LANES, :]
        qh_scr[h] = jnp.where(half == (h % 2), pair, jnp.zeros_like(pair))

    def logits(j, h):
        kh = kb_ref[tile(j), (h // 2) * LANES:(h // 2 + 1) * LANES]
        return jnp.dot(kh, qh_scr[h], preferred_element_type=F32)

    acc_scr[...] = jnp.zeros_like(acc_scr)
    heads = range(ATT_HEADS)
    ones_rows = jnp.ones((PACKED_ROWS, kt), MXU_DTYPE)

    def qk_and_numerators(j_next, lg_next, lg_cur, p_cur, m_new):
        tmax = []
        if lg_next is not None:
            jc = jnp.minimum(j_next, nk - 1)
            bias = jnp.where(jnp.logical_and(j_next < nk, keys_scr[tile(jc), :] >= thr_sel), 0.0, NEG_BIG)
        for h in heads:
            if lg_next is not None:
                x = logits(jc, h) + bias
                lg_next[h] = x
                tmax.append(jnp.max(jnp.max(fold(x), axis=0), axis=0, keepdims=True))
            if lg_cur is not None:
                p_cur[h] = jnp.exp2(lg_cur[h] - m_new[h]).astype(p_cur.dtype)
        return tuple(tmax)

    def half_step(j, lg_cur, p_cur, lg_next, ms, ls, tmax):
        m_new = [jnp.maximum(ms[h], tmax[h]) for h in heads]
        alpha = [jnp.exp2(ms[h] - m_new[h]) for h in heads]
        tmax_next = qk_and_numerators(j + 1, lg_next, lg_cur, p_cur, m_new)
        l_new = []
        for h in heads:
            rows = slice(h * ATT_DIM, (h + 1) * ATT_DIM)
            pv = jnp.dot(jnp.concatenate([vT_ref[j, rows, :], ones_rows], axis=0), p_cur[h],
                         preferred_element_type=F32)
            acc_scr[rows, :] = acc_scr[rows, :] * alpha[h] + pv[0:ATT_DIM, :]
            l_new.append(alpha[h] * ls[h] + pv[ATT_DIM:ATT_DIM + SUBLANES, :])
        return tuple(m_new), tuple(l_new), tmax_next

    def attend_pair(jj, carry):
        ms, ls, tmax = carry
        ms, ls, tmax = half_step(2 * jj, lg_scr, p_scr, lg2_scr, ms, ls, tmax)
        return half_step(2 * jj + 1, lg2_scr, p2_scr, lg_scr, ms, ls, tmax)

    def attend_last(j, carry):
        ms, ls, tmax = carry
        ms, ls, _ = half_step(nk - 1, lg_scr, p_scr, None, ms, ls, tmax)
        return ms, ls, tmax

    init = (tuple(jnp.full((1, tq), NEG_BIG, F32) for _ in heads),
            tuple(jnp.zeros((SUBLANES, tq), F32) for _ in heads),
            qk_and_numerators(0, lg_scr, None, None, None))
    carry = lax.fori_loop(0, nk // 2, attend_pair, init)
    _, ls, _ = lax.fori_loop(0, nk % 2, attend_last, carry)
    for h in heads:
        rows = slice(h * ATT_DIM, (h + 1) * ATT_DIM)
        acc_scr[rows, :] = acc_scr[rows, :] / ls[h][0:1, :]
    o_ref[...] = acc_scr[...].T.astype(o_ref.dtype)


def _dsa(kib, iqT, iwT, kb, aqT, vT, batch, n_q, l_pad, past, topk, tq, kt):
    assert kt % tq == 0 and past % tq == 0 and tq % CHUNK == 0 and n_q % tq == 0 and l_pad % kt == 0
    nq = n_q // tq
    qcol = lambda r: pl.BlockSpec((r, tq), lambda b, i: (0, b * nq + i))
    return pl.pallas_call(
        functools.partial(_dsa_kernel, past=past, topk=topk, tq=tq, kt=kt),
        grid=(batch, nq),
        in_specs=[pl.BlockSpec((None, l_pad, IDX_DIM), lambda b, i: (b, 0, 0)),
                  qcol(IDX_HEADS * IDX_DIM), qcol(SUBLANES),
                  pl.BlockSpec((None, l_pad, ATT_WIDTH), lambda b, i: (b, 0, 0)),
                  qcol(ATT_WIDTH),
                  pl.BlockSpec((l_pad // kt, ATT_WIDTH, kt), lambda b, i: (b, 0, 0))],
        out_specs=pl.BlockSpec((tq, ATT_WIDTH), lambda b, i: (b * nq + i, 0)),
        out_shape=jax.ShapeDtypeStruct((batch * n_q, ATT_WIDTH), MXU_DTYPE),
        scratch_shapes=[pltpu.VMEM((l_pad, tq), I32), pltpu.VMEM((ATT_WIDTH, tq), F32),
                        pltpu.VMEM((ATT_HEADS, LANES, tq), MXU_DTYPE),
                        pltpu.VMEM((ATT_HEADS, kt, tq), MXU_DTYPE),
                        pltpu.VMEM((ATT_HEADS, kt, tq), F32),
                        pltpu.VMEM((l_pad, tq), jnp.int16),
                        pltpu.VMEM((ATT_HEADS, kt, tq), MXU_DTYPE),
                        pltpu.VMEM((ATT_HEADS, kt, tq), F32)],
        compiler_params=_params("parallel", "arbitrary"),
        name="dsa",
    )(kib, iqT, iwT, kb, aqT, vT)


def _out_proj_kernel(yh_ref, oa_ref, sgh_ref, sga_ref, x_ref, wbh_ref, wba_ref, wo_ref, g_ref, b_ref,
                     wr_ref, br_ref, x1_ref, comb_ref):
    tm = x_ref.shape[0]
    parts = [slice(p * (tm // OUT_PROJ_PARTS), (p + 1) * (tm // OUT_PROJ_PARTS)) for p in range(OUT_PROJ_PARTS)]
    br_hg = [jnp.dot(yh_ref[r, :], wbh_ref[...], preferred_element_type=F32) for r in parts]
    br_att = [jnp.dot(oa_ref[r, :], wba_ref[...], preferred_element_type=F32) for r in parts]
    merged = [sgh_ref[r, :] * bh + sga_ref[r, :] * ba for r, bh, ba in zip(parts, br_hg, br_att)]
    out = [_mm(m, wo_ref[...]) for m in merged]
    x1 = [_layer_norm(DN_ALPHA * x_ref[r, :] + o, g_ref[...], b_ref[...]) for r, o in zip(parts, out)]
    for r, v in zip(parts, x1):
        x1_ref[r, :] = v
    logits = [_mm(v, wr_ref[...]) + br_ref[...] for v in x1]
    for r, lg in zip(parts, logits):
        comb_ref[r, :] = _route(lg)


def _route(lg):
    lane = lax.broadcasted_iota(I32, lg.shape, 1).astype(F32)
    ninf = -jnp.inf
    gmask = lane < N_GROUPS
    gl = jnp.where(gmask, lg, ninf)
    gmax = jnp.max(gl, axis=1, keepdims=True)
    gsel = jnp.min(jnp.where(gl == gmax, lane, float(ROUTER_LANES)), axis=1, keepdims=True)
    g_w = 1.0 / jnp.sum(jnp.where(gmask, jnp.exp(gl - gmax), 0.0), axis=1, keepdims=True)
    e0 = EXPERT_LANE0 + EXPERTS_PER_GROUP * gsel
    emask = jnp.logical_and(lane >= e0, lane < e0 + EXPERTS_PER_GROUP)
    el = jnp.where(emask, lg, ninf)
    emax = jnp.max(el, axis=1, keepdims=True)
    ee = jnp.where(emask, jnp.exp(el - emax), 0.0)
    prob = ee / jnp.sum(ee, axis=1, keepdims=True)
    pm = jnp.where(emask, prob, -1.0)
    p1 = jnp.max(pm, axis=1, keepdims=True)
    i1 = jnp.min(jnp.where(pm == p1, lane, float(ROUTER_LANES)), axis=1, keepdims=True)
    pm2 = jnp.where(lane == i1, -1.0, pm)
    p2 = jnp.max(pm2, axis=1, keepdims=True)
    i2 = jnp.min(jnp.where(pm2 == p2, lane, float(ROUTER_LANES)), axis=1, keepdims=True)
    tot = p1 + p2
    return (jnp.where(lane == i1, g_w * (p1 / tot), 0.0)
            + jnp.where(lane == i2, g_w * (p2 / tot), 0.0))


def _out_proj(yh, oa, sgh, sga, x, lw, tm):
    n = x.shape[0]
    tm = min(tm, n)
    full = lambda a: pl.BlockSpec(a.shape, lambda i: (0,) * a.ndim)
    rows = lambda c: pl.BlockSpec((tm, c), lambda i: (i, 0))
    weights = (lw["w_br_hg"], lw["w_br_att"], lw["w_out"], lw["ln1_g"], lw["ln1_b"], lw["w_r"], lw["b_r"])
    return pl.pallas_call(
        _out_proj_kernel,
        grid=(n // tm,),
        in_specs=[rows(HG_WIDTH), rows(ATT_WIDTH), rows(D_MODEL), rows(D_MODEL), rows(D_MODEL)]
        + [full(w) for w in weights],
        out_specs=(rows(D_MODEL), rows(ROUTER_LANES)),
        out_shape=(jax.ShapeDtypeStruct((n, D_MODEL), F32), jax.ShapeDtypeStruct((n, ROUTER_LANES), F32)),
        compiler_params=_params("parallel"),
        name="out_proj",
    )(yh, oa, sgh, sga, x, *weights)


def _moe_kernel(x_ref, comb_ref, wgu_ref, wd_ref, g_ref, b_ref, o_ref, xb_scr, hb_scr, acc_scr):
    grp = pl.program_id(1)

    @pl.when(grp == 0)
    def _():
        xb_scr[...] = x_ref[...].astype(xb_scr.dtype)
        acc_scr[...] = jnp.zeros_like(acc_scr)

    xb = xb_scr[...]
    comb = comb_ref[...]
    lane = lax.broadcasted_iota(I32, comb.shape, 1)
    F = D_EXPERT
    for e in range(EXPERTS_PER_GROUP):
        gu = jnp.dot(xb, wgu_ref[e], preferred_element_type=F32)
        h = _silu(gu[:, 0:F]) * gu[:, F:2 * F]
        c = jnp.sum(jnp.where(lane == EXPERT_LANE0 + grp * EXPERTS_PER_GROUP + e, comb, 0.0),
                    axis=1, keepdims=True)
        hb_scr[:, e * F:(e + 1) * F] = (h * c).astype(hb_scr.dtype)
    acc_scr[...] += jnp.dot(hb_scr[...], wd_ref[...], preferred_element_type=F32)

    @pl.when(grp == pl.num_programs(1) - 1)
    def _():
        o_ref[...] = _layer_norm(DN_ALPHA * x_ref[...] + acc_scr[...], g_ref[...], b_ref[...])


def _moe(x1, comb, lw, tm):
    n = x1.shape[0]
    tm = min(tm, n)
    E, F = EXPERTS_PER_GROUP, D_EXPERT
    return pl.pallas_call(
        _moe_kernel,
        grid=(n // tm, N_GROUPS),
        in_specs=[pl.BlockSpec((tm, D_MODEL), lambda i, g: (i, 0)),
                  pl.BlockSpec((tm, ROUTER_LANES), lambda i, g: (i, 0)),
                  pl.BlockSpec((None, E, D_MODEL, 2 * F), lambda i, g: (g, 0, 0, 0)),
                  pl.BlockSpec((None, E * F, D_MODEL), lambda i, g: (g, 0, 0)),
                  pl.BlockSpec((1, D_MODEL), lambda i, g: (0, 0)),
                  pl.BlockSpec((1, D_MODEL), lambda i, g: (0, 0))],
        out_specs=pl.BlockSpec((tm, D_MODEL), lambda i, g: (i, 0)),
        out_shape=jax.ShapeDtypeStruct((n, D_MODEL), F32),
        scratch_shapes=[pltpu.VMEM((tm, D_MODEL), MXU_DTYPE), pltpu.VMEM((tm, E * F), MXU_DTYPE),
                        pltpu.VMEM((tm, D_MODEL), F32)],
        compiler_params=_params("parallel", "arbitrary"),
        name="moe",
    )(x1, comb, lw["w_gu"], lw["w_d"], lw["ln2_g"], lw["ln2_b"])


def _layer_weights(l, lbs, w_in, hg_norm_w, w_br_hg, w_br_att, w_out, ln1_g, ln1_b, ln2_g, ln2_b,
                   w_rg, b_rg, w_re, b_re, w_gate, w_up, w_down):
    md = MXU_DTYPE
    offs = [0]
    for s in IN_SIZES:
        offs.append(offs[-1] + s)
    w = w_in[l]
    seg = lambda a, b: w[:, offs[a]:offs[b]]
    idx_cols = jnp.concatenate([seg(8, 9), jnp.zeros((D_MODEL, LANES - IDX_DIM), F32)], axis=1)
    iw_rows = jnp.concatenate([seg(9, 10).T, jnp.zeros((SUBLANES - IDX_HEADS, D_MODEL), F32)], axis=0)
    lb = lbs[l]
    lbp = jnp.concatenate([jnp.log(lb)[None], jnp.log1p(-lb)[None], (1.0 - lb)[None],
                           jnp.zeros((SUBLANES - 3, HG_WIDTH), F32)], axis=0)
    w_r = jnp.concatenate([w_rg[l], w_re[l], jnp.zeros((D_MODEL, ROUTER_LANES - N_GROUPS - N_EXPERTS), F32)], axis=1)
    b_r = jnp.concatenate([b_rg[l], b_re[l], jnp.zeros((ROUTER_LANES - N_GROUPS - N_EXPERTS,), F32)])[None]
    return {
        "w_h": seg(0, 4).astype(md), "w_a": seg(5, 7).astype(md), "w_i": idx_cols.astype(md),
        "w_g": seg(10, 12).astype(md),
        "w_aqT": seg(4, 5).T.astype(md), "w_iqT": seg(7, 8).T.astype(md),
        "w_iwT": iw_rows.astype(md), "lbp": lbp,
        "norm_w": hg_norm_w[l],
        "w_br_hg": w_br_hg[l].astype(md), "w_br_att": w_br_att[l].astype(md), "w_out": w_out[l].astype(md),
        "ln1_g": ln1_g[l][None], "ln1_b": ln1_b[l][None], "ln2_g": ln2_g[l][None], "ln2_b": ln2_b[l][None],
        "w_r": w_r.astype(md), "b_r": b_r,
        "w_gu": jnp.concatenate([w_gate[l], w_up[l]], axis=-1).astype(md),
        "w_d": w_down[l].reshape(N_GROUPS, EXPERTS_PER_GROUP * D_EXPERT, D_MODEL).astype(md),
    }


def _lower_bounds(lb_logits):
    p = jax.nn.softmax(lb_logits.astype(F32), axis=0)
    return jnp.concatenate([jnp.zeros_like(p[:1]), jnp.cumsum(p[1:], axis=0)], axis=0)


TM_IN = 256
TM_OUT = 1024
OUT_PROJ_PARTS = 2
TM_MOE = 1024
HGRN_CHUNKS_PER_STEP = 16
HGRN_CHUNKS_PER_TRIP = 8
DSA_TQ_PROMPT = 256
DSA_TQ_SAMPLE = 128
DSA_KT = 256
SCORE_TILES_PER_TRIP = 3


def _mixer_and_ffn(x, lw, batch, seq, s0, past, layer, bufs):
    tq, kt = (DSA_TQ_PROMPT if past is None else DSA_TQ_SAMPLE), DSA_KT
    (qh, lf, kk, vh, og, k_buf, v_buf, kb, ik_buf, kib, aqT, vT, iqT, iwT, sgh, sga) = _in_proj(
        x, lw, TM_IN, kt, layer, bufs)
    y_hg, s_new = _hgrn(qh, lf, kk, vh, og, lw["norm_w"], s0, batch, seq, HGRN_CHUNKS_PER_STEP)
    if past is None:
        l_tot = seq
        l_pad = -(-l_tot // kt) * kt
        assert l_pad == l_tot and seq % tq == 0
        o_att = _dsa(kib.reshape(batch, seq, IDX_DIM), iqT, iwT, kb.reshape(batch, seq, ATT_WIDTH), aqT, vT,
                     batch, seq, l_pad, 0, min(TOPK_MAX, l_tot // 4), tq, kt)
    else:
        k_past, v_past, ki_past = past
        p_len = k_past.shape[1]
        l_tot = p_len + seq
        n_q = -(-seq // tq) * tq
        l_pad = -(-(p_len + n_q) // kt) * kt
        padk = lambda a, new: jnp.concatenate(
            [a, new, jnp.zeros((batch, l_pad - l_tot, a.shape[2]), a.dtype)], axis=1)
        k_all = padk(k_past, kb.reshape(batch, seq, ATT_WIDTH))
        ki_all = padk(ki_past, kib.reshape(batch, seq, IDX_DIM))
        v_all = padk(v_past, v_buf[layer].astype(v_past.dtype).reshape(batch, seq, ATT_WIDTH))
        vT_all = jnp.transpose(v_all.reshape(batch * (l_pad // kt), kt, ATT_WIDTH), (0, 2, 1))
        padq = lambda a: jnp.pad(a.reshape(a.shape[0], batch, seq), ((0, 0), (0, 0), (0, n_q - seq))
                                 ).reshape(a.shape[0], batch * n_q)
        o_pad = _dsa(ki_all, padq(iqT), padq(iwT), k_all, padq(aqT), vT_all,
                     batch, n_q, l_pad, p_len, min(TOPK_MAX, l_tot // 4), tq, kt)
        o_att = o_pad.reshape(batch, n_q, ATT_WIDTH)[:, :seq].reshape(batch * seq, ATT_WIDTH)
    x1, comb = _out_proj(y_hg, o_att, sgh, sga, x, lw, TM_OUT)
    x2 = _moe(x1, comb, lw, TM_MOE)
    return x2, (k_buf, v_buf, ik_buf), s_new


def kernel(x_prompt, x_sample, cache_k, cache_v, cache_idx_k, state_hgrn, w_in, hg_lb_logits, hg_norm_w,
           w_br_hg, w_br_att, w_out, ln1_g, ln1_b, ln2_g, ln2_b, w_rg, b_rg, w_re, b_re, w_gate, w_up, w_down):
    bp, tp, d = x_prompt.shape
    bs, ts, _ = x_sample.shape
    p_len = cache_k.shape[2]
    lbs = _lower_bounds(hg_lb_logits)
    xp = x_prompt.reshape(bp * tp, d)
    xs = x_sample.reshape(bs * ts, d)
    zeros_state = jnp.zeros((bp, HG_HEADS, HG_DK, HG_DV), F32)
    bufs_p, bufs_s, st_p, st_s = None, None, [], []
    for l in range(DEPTH):
        lw = _layer_weights(l, lbs, w_in, hg_norm_w, w_br_hg, w_br_att, w_out, ln1_g, ln1_b, ln2_g, ln2_b,
                            w_rg, b_rg, w_re, b_re, w_gate, w_up, w_down)
        xp, bufs_p, sp = _mixer_and_ffn(xp, lw, bp, tp, zeros_state, None, l, bufs_p)
        past = (cache_k[l].reshape(bs, p_len, ATT_WIDTH).astype(MXU_DTYPE),
                cache_v[l].reshape(bs, p_len, ATT_WIDTH).astype(MXU_DTYPE),
                cache_idx_k[l].astype(MXU_DTYPE))
        xs, bufs_s, ss = _mixer_and_ffn(xs, lw, bs, ts, state_hgrn[l].astype(F32), past, l, bufs_s)
        st_p.append(sp)
        st_s.append(ss)

    def shaped(bufs, b, t):
        k, v, ik = bufs
        return (k.reshape(DEPTH, b, t, ATT_HEADS, ATT_DIM), v.reshape(DEPTH, b, t, ATT_HEADS, ATT_DIM),
                ik.reshape(DEPTH, b, t, IDX_DIM))

    kp, vp, ikp = shaped(bufs_p, bp, tp)
    ks, vs, iks = shaped(bufs_s, bs, ts)
    return (xp.reshape(bp, tp, d), xs.reshape(bs, ts, d), kp, vp, ikp, jnp.stack(st_p), ks, vs, iks,
            jnp.stack(st_s).astype(state_hgrn.dtype))
```
